```python
import math
import jax, jax.numpy as jnp
from jax import lax
import numpy as np

D_MODEL = 2048
BATCH = 16
SEQ = 256
DEPTH = 2
DEC_BATCH = 2
DEC_SEQ = 2048
PAST_LEN = 256

GRID_W = 64
N_AB_LAYERS = (DEPTH + 1) // 2
N_C_LAYERS = DEPTH // 2
N_MOD = 6

S5_WIDTH = D_MODEL // 2
S5_GROUP = 16
S5_GROUPS = S5_WIDTH // S5_GROUP
S5_STATE = 64
S5_DT_MIN = 0.001
S5_DT_MAX = 0.1

NA_WIDTH = D_MODEL - S5_WIDTH
NA_HEAD_DIM = 128
NA_HEADS = NA_WIDTH // NA_HEAD_DIM
NA_WIN_R = 8
NA_WIN_C = 16

POOL_SIZES = (2, 4, 8, 16)
POOL_GROUP_DIM = D_MODEL // len(POOL_SIZES)

FFN_DIM = 5632
N_EXPERTS = 8
TOP_K = 2
EXPERT_DIM = 7168

LN_EPS = 1e-5
DEEPNORM_ALPHA = (2.0 * DEPTH) ** 0.25
DEEPNORM_BETA = (8.0 * DEPTH) ** -0.25
NEG_INF = -1e30

kernel_name = 'hybrid_s5_natten_pool_moe_diffusion_step'


def layer_norm(x, g, b):
    xf = x.astype(jnp.float32)
    mu = jnp.mean(xf, axis=-1, keepdims=True)
    var = jnp.mean(jnp.square(xf - mu), axis=-1, keepdims=True)
    y = (xf - mu) * lax.rsqrt(var + LN_EPS) * g.astype(jnp.float32) + b.astype(jnp.float32)
    return y.astype(x.dtype)


def ada_mod(cond, w, b):
    return jnp.split(jax.nn.silu(cond) @ w + b, N_MOD, axis=-1)


def modulate(x, shift, scale):
    return x * (1 + scale[:, None]) + shift[:, None]


def post_norm(x, y, gate, g, b):
    return layer_norm(DEEPNORM_ALPHA * x + gate[:, None] * y, g, b)


def _linear_recurrence_op(e1, e2):
    a1, b1 = e1
    a2, b2 = e2
    return a1 * a2, a2 * b1 + b2


def s5_mixer(u, h0_re, h0_im, lam_re, lam_im, log_dt, b_re, b_im, c_re, c_im, d, glu_w, glu_b):
    bsz, L, _ = u.shape
    f32 = jnp.float32
    uf = u.astype(f32)
    ug = uf.reshape(bsz, L, S5_GROUPS, S5_GROUP)
    lam = lax.complex(jnp.minimum(lam_re.astype(f32), -1e-4), lam_im.astype(f32))
    dt = jnp.exp(log_dt.astype(f32))[..., None]
    lam_bar = jnp.exp(lam * dt)
    b_bar = ((lam_bar - 1) / lam)[..., None] * lax.complex(b_re.astype(f32), b_im.astype(f32))
    c_mat = lax.complex(c_re.astype(f32), c_im.astype(f32))
    h0 = lax.complex(h0_re.astype(f32), h0_im.astype(f32))
    y = uf * d.astype(f32)
    finals = []
    for direction, reverse in enumerate((False, True)):
        bu = jnp.einsum('blgc,gpc->blgp', ug, b_bar[direction])
        first = L - 1 if reverse else 0
        bu = bu.at[:, first].add(lam_bar[direction] * h0[:, direction])
        a = jnp.broadcast_to(lam_bar[direction], bu.shape)
        _, h = lax.associative_scan(_linear_recurrence_op, (a, bu), axis=1, reverse=reverse)
        y = y + jnp.einsum('blgp,gcp->blgc', h, c_mat[direction]).real.reshape(bsz, L, S5_WIDTH)
        finals.append(h[:, 0] if reverse else h[:, L - 1])
    y = jax.nn.gelu(y).astype(u.dtype)
    y = y * jax.nn.sigmoid(y @ glu_w + glu_b)
    fin = jnp.stack(finals, axis=1)
    return y, fin.real, fin.imag


def ab_project(h, w_in):
    bsz, L, _ = h.shape
    u, q, k, v = jnp.split(h @ w_in, [S5_WIDTH, S5_WIDTH + NA_WIDTH, S5_WIDTH + 2 * NA_WIDTH], axis=-1)
    def heads(t):
        return t.reshape(bsz, L, NA_HEADS, NA_HEAD_DIM).transpose(0, 2, 1, 3)
    return u, heads(q) * (NA_HEAD_DIM ** -0.5), heads(k), heads(v)


def merge_heads(o):
    bsz, H, L, dh = o.shape
    return o.transpose(0, 2, 1, 3).reshape(bsz, L, H * dh)


def context_attention(q, k, v):
    s = jnp.einsum('bhqd,bhkd->bhqk', q, k).astype(jnp.float32)
    p = jax.nn.softmax(s, axis=-1).astype(v.dtype)
    return jnp.einsum('bhqk,bhkd->bhqd', p, v)


def neighbourhood_attention(q, k, v, k_ctx, v_ctx, rpb):
    bsz, H, S, dh = q.shape
    rows = S // GRID_W
    win_r = min(NA_WIN_R, rows)
    f32 = jnp.float32
    def grid(t):
        return t.reshape(bsz, H, rows, GRID_W, dh)
    qg, kg, vg = grid(q), grid(k), grid(v)
    r = jnp.arange(rows)
    row_idx = jnp.clip(r - win_r // 2, 0, rows - win_r)[:, None] + jnp.arange(win_r)[None, :]
    k_rows = kg[:, :, row_idx]
    v_rows = vg[:, :, row_idx]
    col = jnp.arange(GRID_W)
    c_start = jnp.clip(col - NA_WIN_C // 2, 0, GRID_W - NA_WIN_C)
    col_mask = (col[None, :] >= c_start[:, None]) & (col[None, :] < c_start[:, None] + NA_WIN_C)
    dr_idx = row_idx - r[:, None] + (NA_WIN_R - 1)
    dc_idx = jnp.clip(col[None, :] - col[:, None] + (NA_WIN_C - 1), 0, 2 * NA_WIN_C - 2)
    bias = rpb.astype(f32)[:, dr_idx[:, None, :, None], dc_idx[None, :, None, :]]
    bias = jnp.where(col_mask[None, None, :, None, :], bias, NEG_INF)
    s_loc = jnp.einsum('bhrqd,bhrkwd->bhrqkw', qg, k_rows).astype(f32) + bias[None]
    s_loc = s_loc.reshape(bsz, H, rows, GRID_W, win_r * GRID_W)
    s_ctx = jnp.einsum('bhrqd,bhcd->bhrqc', qg, k_ctx).astype(f32)
    p = jax.nn.softmax(jnp.concatenate([s_loc, s_ctx], axis=-1), axis=-1).astype(v.dtype)
    n_loc = win_r * GRID_W
    p_loc = p[..., :n_loc].reshape(bsz, H, rows, GRID_W, win_r, GRID_W)
    p_ctx = p[..., n_loc:]
    o = jnp.einsum('bhrqkw,bhrkwd->bhrqd', p_loc, v_rows) + jnp.einsum('bhrqc,bhcd->bhrqd', p_ctx, v_ctx)
    return o.reshape(bsz, H, S, dh)


def ab_mixer_context(h, w_in, w_out, s5p):
    bsz = h.shape[0]
    u, q, k, v = ab_project(h, w_in)
    zero = jnp.zeros((bsz, 2, S5_GROUPS, S5_STATE), jnp.float32)
    y_s5, fin_re, fin_im = s5_mixer(u, zero, zero, *s5p)
    y_att = merge_heads(context_attention(q, k, v))
    y = jnp.concatenate([y_s5, y_att], axis=-1) @ w_out
    return y, fin_re, fin_im, k, v


def ab_mixer_latent(h, h0_re, h0_im, k_ctx, v_ctx, w_in, w_out, s5p, rpb):
    u, q, k, v = ab_project(h, w_in)
    y_s5, _, _ = s5_mixer(u, h0_re, h0_im, *s5p)
    y_att = merge_heads(neighbourhood_attention(q, k, v, k_ctx, v_ctx, rpb))
    return jnp.concatenate([y_s5, y_att], axis=-1) @ w_out


def pool_mixer(h, w_pool, pool_scale):
    bsz, L, D = h.shape
    hf = h.astype(jnp.float32)
    cs = jnp.concatenate([jnp.zeros((bsz, 1, D), jnp.float32), jnp.cumsum(hf, axis=1)], axis=1)
    t = jnp.arange(L)
    outs = []
    for g, w in enumerate(POOL_SIZES):
        lo = jnp.clip(t - w // 2, 0, L)
        hi = jnp.clip(t + w - w // 2, 0, L)
        sl = slice(g * POOL_GROUP_DIM, (g + 1) * POOL_GROUP_DIM)
        csg = cs[..., sl]
        mean = (csg[:, hi] - csg[:, lo]) / (hi - lo).astype(jnp.float32)[None, :, None]
        outs.append(mean - hf[..., sl])
    pooled = jnp.stack(outs, axis=2).astype(h.dtype)
    y = jnp.einsum('blgc,gcd->blgd', pooled, w_pool).reshape(bsz, L, D)
    return y * pool_scale


def swiglu(h, w_gate, w_up, w_down):
    return (jax.nn.silu(h @ w_gate) * (h @ w_up)) @ w_down


def moe_swiglu(h, router_w, router_b, w_gate, w_up, w_down):
    shp = h.shape
    t = h.reshape(-1, shp[-1])
    logits = (t @ router_w).astype(jnp.float32) + router_b.astype(jnp.float32)
    top_v, top_i = lax.top_k(logits, TOP_K)
    gates = jax.nn.softmax(top_v, axis=-1)
    dense_gate = jnp.sum(jax.nn.one_hot(top_i, N_EXPERTS, dtype=jnp.float32) * gates[..., None], axis=1)
    out = jnp.zeros_like(t)
    for e in range(N_EXPERTS):
        out = out + dense_gate[:, e:e + 1].astype(t.dtype) * swiglu(t, w_gate[e], w_up[e], w_down[e])
    return out.reshape(shp)


def setup_inputs(seed: int = 0) -> dict:
    key = jax.random.key(seed)
    ks = iter(jax.random.split(key, 40))
    def nrm(shape, scale=1.0):
        return jax.random.normal(next(ks), shape, jnp.float32) * scale
    s5_shape = (N_AB_LAYERS, 2, S5_GROUPS, S5_STATE)
    state_shape = (DEC_BATCH, N_AB_LAYERS, 2, S5_GROUPS, S5_STATE)
    cache_shape = (DEC_BATCH, N_AB_LAYERS, NA_HEADS, PAST_LEN, NA_HEAD_DIM)
    n_idx = jnp.arange(S5_STATE, dtype=jnp.float32)
    return {
        'x_prompt': nrm((BATCH, SEQ, D_MODEL)),
        'x_sample': nrm((DEC_BATCH, DEC_SEQ, D_MODEL)),
        'state_s5_re': nrm(state_shape),
        'state_s5_im': nrm(state_shape),
        'cache_k': nrm(cache_shape),
        'cache_v': nrm(cache_shape),
        'c': nrm((DEC_BATCH, D_MODEL)),
        'c_ctx': nrm((D_MODEL,)),
        'ada_w': nrm((DEPTH, D_MODEL, N_MOD * D_MODEL), 0.5 * D_MODEL ** -0.5),
        'ada_b': nrm((DEPTH, N_MOD * D_MODEL), 0.02),
        'ln_g': 1.0 + nrm((DEPTH, 2, D_MODEL), 0.02),
        'ln_b': nrm((DEPTH, 2, D_MODEL), 0.02),
        'ab_w_in': nrm((N_AB_LAYERS, D_MODEL, S5_WIDTH + 3 * NA_WIDTH), D_MODEL ** -0.5),
        'ab_w_out': nrm((N_AB_LAYERS, D_MODEL, D_MODEL), DEEPNORM_BETA * D_MODEL ** -0.5),
        's5_lambda_re': -0.5 + nrm(s5_shape, 0.01),
        's5_lambda_im': math.pi * n_idx + nrm(s5_shape, 0.01),
        's5_log_dt': jax.random.uniform(next(ks), (N_AB_LAYERS, 2, S5_GROUPS), jnp.float32,
                                        minval=math.log(S5_DT_MIN), maxval=math.log(S5_DT_MAX)),
        's5_b_re': nrm(s5_shape + (S5_GROUP,), (2 * S5_GROUP) ** -0.5),
        's5_b_im': nrm(s5_shape + (S5_GROUP,), (2 * S5_GROUP) ** -0.5),
        's5_c_re': nrm((N_AB_LAYERS, 2, S5_GROUPS, S5_GROUP, S5_STATE), S5_STATE ** -0.5),
        's5_c_im': nrm((N_AB_LAYERS, 2, S5_GROUPS, S5_GROUP, S5_STATE), S5_STATE ** -0.5),
        's5_d': nrm((N_AB_LAYERS, S5_WIDTH)),
        's5_glu_w': nrm((N_AB_LAYERS, S5_WIDTH, S5_WIDTH), S5_WIDTH ** -0.5),
        's5_glu_b': nrm((N_AB_LAYERS, S5_WIDTH), 0.02),
        'na_rpb': nrm((N_AB_LAYERS, NA_HEADS, 2 * NA_WIN_R - 1, 2 * NA_WIN_C - 1), 0.1),
        'ffn_w_gate': nrm((N_AB_LAYERS, D_MODEL, FFN_DIM), D_MODEL ** -0.5),
        'ffn_w_up': nrm((N_AB_LAYERS, D_MODEL, FFN_DIM), D_MODEL ** -0.5),
        'ffn_w_down': nrm((N_AB_LAYERS, FFN_DIM, D_MODEL), DEEPNORM_BETA * FFN_DIM ** -0.5),
        'pool_w': nrm((N_C_LAYERS, len(POOL_SIZES), POOL_GROUP_DIM, POOL_GROUP_DIM), DEEPNORM_BETA * POOL_GROUP_DIM ** -0.5),
        'pool_scale': 1.0 + nrm((N_C_LAYERS, D_MODEL), 0.02),
        'moe_router_w': nrm((N_C_LAYERS, D_MODEL, N_EXPERTS), D_MODEL ** -0.5),
        'moe_router_b': nrm((N_C_LAYERS, N_EXPERTS), 0.01),
        'moe_w_gate': nrm((N_C_LAYERS, N_EXPERTS, D_MODEL, EXPERT_DIM), D_MODEL ** -0.5),
        'moe_w_up': nrm((N_C_LAYERS, N_EXPERTS, D_MODEL, EXPERT_DIM), D_MODEL ** -0.5),
        'moe_w_down': nrm((N_C_LAYERS, N_EXPERTS, EXPERT_DIM, D_MODEL), DEEPNORM_BETA * EXPERT_DIM ** -0.5),
    }


def reference(x_prompt, x_sample, state_s5_re, state_s5_im, cache_k, cache_v, c, c_ctx,
              ada_w, ada_b, ln_g, ln_b, ab_w_in, ab_w_out,
              s5_lambda_re, s5_lambda_im, s5_log_dt, s5_b_re, s5_b_im, s5_c_re, s5_c_im,
              s5_d, s5_glu_w, s5_glu_b, na_rpb, ffn_w_gate, ffn_w_up, ffn_w_down,
              pool_w, pool_scale, moe_router_w, moe_router_b, moe_w_gate, moe_w_up, moe_w_down):
    xp, xs = x_prompt, x_sample
    cond_ctx = c_ctx[None, :]
    new_re, new_im, new_k, new_v = [], [], [], []
    for layer in range(DEPTH):
        i = layer // 2
        mp = ada_mod(cond_ctx, ada_w[layer], ada_b[layer])
        ms = ada_mod(c, ada_w[layer], ada_b[layer])
        hp = modulate(xp, mp[0], mp[1])
        hs = modulate(xs, ms[0], ms[1])
        if layer % 2 == 0:
            s5p = (s5_lambda_re[i], s5_lambda_im[i], s5_log_dt[i], s5_b_re[i], s5_b_im[i],
                   s5_c_re[i], s5_c_im[i], s5_d[i], s5_glu_w[i], s5_glu_b[i])
            yp, fin_re, fin_im, kp, vp = ab_mixer_context(hp, ab_w_in[i], ab_w_out[i], s5p)
            ys = ab_mixer_latent(hs, state_s5_re[:, i], state_s5_im[:, i], cache_k[:, i], cache_v[:, i],
                                 ab_w_in[i], ab_w_out[i], s5p, na_rpb[i])
            new_re.append(fin_re)
            new_im.append(fin_im)
            new_k.append(kp)
            new_v.append(vp)
        else:
            yp = pool_mixer(hp, pool_w[i], pool_scale[i])
            ys = pool_mixer(hs, pool_w[i], pool_scale[i])
        xp = post_norm(xp, yp, mp[2], ln_g[layer, 0], ln_b[layer, 0])
        xs = post_norm(xs, ys, ms[2], ln_g[layer, 0], ln_b[layer, 0])
        hp = modulate(xp, mp[3], mp[4])
        hs = modulate(xs, ms[3], ms[4])
        if layer % 2 == 0:
            yp = swiglu(hp, ffn_w_gate[i], ffn_w_up[i], ffn_w_down[i])
            ys = swiglu(hs, ffn_w_gate[i], ffn_w_up[i], ffn_w_down[i])
        else:
            yp = moe_swiglu(hp, moe_router_w[i], moe_router_b[i], moe_w_gate[i], moe_w_up[i], moe_w_down[i])
            ys = moe_swiglu(hs, moe_router_w[i], moe_router_b[i], moe_w_gate[i], moe_w_up[i], moe_w_down[i])
        xp = post_norm(xp, yp, mp[5], ln_g[layer, 1], ln_b[layer, 1])
        xs = post_norm(xs, ys, ms[5], ln_g[layer, 1], ln_b[layer, 1])
    return (xp, xs, jnp.stack(new_re, axis=1), jnp.stack(new_im, axis=1),
            jnp.stack(new_k, axis=1), jnp.stack(new_v, axis=1))
```

```python
import functools
import math

import jax
import jax.numpy as jnp
from jax import lax
from jax.experimental import pallas as pl
from jax.experimental.pallas import tpu as pltpu

F32 = jnp.float32
BF16 = jnp.bfloat16

D_MODEL = 2048
BATCH = 16
SEQ = 256
DEPTH = 2
DEC_BATCH = 2
DEC_SEQ = 2048
N_MOD = 6
N_PROMPT = BATCH * SEQ
N_TOK = N_PROMPT + DEC_BATCH * DEC_SEQ
N_COND = 1 + DEC_BATCH

S5_WIDTH = 1024
S5_GROUP = 16
S5_GROUPS = 64
S5_STATE = 64
S5_CHUNK = 16
S5_SEQ = 256
S5_NSEQ = N_TOK // S5_SEQ
S5_NCHUNK = S5_SEQ // S5_CHUNK
S5_ROWS = S5_NSEQ * S5_NCHUNK
S5_SEGS = DEC_SEQ // S5_SEQ
S5_CW = S5_CHUNK * S5_GROUP
S5_SW = 2 * S5_STATE

NA_WIDTH = 1024
NA_HEADS = 8
NA_HEAD_DIM = 128
NA_WIN_R = 8
NA_WIN_C = 16
GRID_W = 64
GRID_ROWS = DEC_SEQ // GRID_W
NA_QROWS = 4
NA_KROWS = 12
NA_NDR = 2 * NA_WIN_R - 1
NA_NDC = 2 * NA_WIN_C - 1
PAST_LEN = 256

POOL_SIZES = (2, 4, 8, 16)
POOL_GROUP_DIM = 512
POOL_HALO = 16

FFN_DIM = 5632
N_EXPERTS = 8
EXPERT_DIM = 7168
MOE_TILE = 256
MOE_ROWS = 2 * N_TOK + N_EXPERTS * MOE_TILE
MOE_NTILES = MOE_ROWS // MOE_TILE

LN_EPS = 1e-5
DEEPNORM_ALPHA = (2.0 * DEPTH) ** 0.25
NEG_INF = -1e30
QK_SCALE = NA_HEAD_DIM ** -0.5

VMEM_LIMIT = 52 * 1024 * 1024


def _params(sem, vmem=VMEM_LIMIT):
    return pltpu.CompilerParams(dimension_semantics=sem, vmem_limit_bytes=vmem)


def _sigmoid(x):
    return 1.0 / (1.0 + jnp.exp(-x))


def _dot(a, b):
    return jnp.dot(a, b, preferred_element_type=F32)


def _dot_nt(a, b):
    return lax.dot_general(a, b, (((1,), (1,)), ((), ())), preferred_element_type=F32)


def _dot_exact(a, b):
    return jnp.dot(a, b, preferred_element_type=F32, precision=lax.Precision.HIGHEST)


def _cond_of_row(row0):
    return jnp.where(row0 < N_PROMPT, 0, 1 + (row0 - N_PROMPT) // DEC_SEQ)


def _mod_spec(layer, k, tm):
    return pl.BlockSpec((None, None, None, 1, D_MODEL),
                        lambda i, *_: (layer, k, _cond_of_row(i * tm), 0, 0))


def _row_spec(layer_idx, k=None):
    if k is None:
        return pl.BlockSpec((None, 1, D_MODEL), lambda *_: (layer_idx, 0, 0))
    return pl.BlockSpec((None, None, 1, D_MODEL), lambda *_: (layer_idx, k, 0, 0))


def _post_norm(x, y, gate, g, b):
    v = DEEPNORM_ALPHA * x + gate * y
    mu = jnp.mean(v, axis=-1, keepdims=True)
    c = v - mu
    var = jnp.mean(c * c, axis=-1, keepdims=True)
    return c * lax.rsqrt(var + LN_EPS) * g + b


def _ada_kernel(c_ref, w_ref, b_ref, o_ref):
    c = c_ref[...]
    s = (c * _sigmoid(c)).astype(BF16)
    o_ref[...] = _dot(s, w_ref[...].astype(BF16)) + b_ref[...]


def _ada(cond8, ada_w, ada_b):
    tn = 1024
    n_out = N_MOD * D_MODEL
    return pl.pallas_call(
        _ada_kernel,
        out_shape=jax.ShapeDtypeStruct((DEPTH, 8, n_out), F32),
        grid=(DEPTH, n_out // tn),
        in_specs=[pl.BlockSpec((8, D_MODEL), lambda l, n: (0, 0)),
                  pl.BlockSpec((None, D_MODEL, tn), lambda l, n: (l, 0, n)),
                  pl.BlockSpec((None, 1, tn), lambda l, n: (l, 0, n))],
        out_specs=pl.BlockSpec((None, 8, tn), lambda l, n: (l, 0, n)),
        compiler_params=_params(("parallel", "parallel")),
        name="ada_mod",
    )(cond8, ada_w, ada_b.reshape(DEPTH, 1, n_out))


def _cast_kernel(w_ref, o_ref):
    o_ref[...] = w_ref[...].astype(BF16)


def _cast_bf16(w):
    shape = w.shape
    cols = shape[-1]
    rows = w.size // cols
    rb = 8
    while rb * 2 * cols * 4 <= 4 * 1024 * 1024 and rows % (rb * 2) == 0:
        rb *= 2
    out = pl.pallas_call(
        _cast_kernel,
        out_shape=jax.ShapeDtypeStruct((rows, cols), BF16),
        grid=(rows // rb,),
        in_specs=[pl.BlockSpec((rb, cols), lambda i: (i, 0))],
        out_specs=pl.BlockSpec((rb, cols), lambda i: (i, 0)),
        compiler_params=_params(("parallel",)),
        name="cast_bf16",
    )(w.reshape(rows, cols))
    return out.reshape(shape)


def _inproj_kernel(x_ref, sh_ref, sc_ref, w_ref, o_ref, h_ref):
    @pl.when(pl.program_id(1) == 0)
    def _():
        h_ref[...] = (x_ref[...] * (1.0 + sc_ref[...]) + sh_ref[...]).astype(BF16)

    o_ref[...] = _dot(h_ref[...], w_ref[...])


def _inproj(x, mod, w_bf, layer):
    tm, tn = 512, 1024
    n_out = w_bf.shape[1]
    return pl.pallas_call(
        _inproj_kernel,
        out_shape=jax.ShapeDtypeStruct((N_TOK, n_out), F32),
        grid=(N_TOK // tm, n_out // tn),
        in_specs=[pl.BlockSpec((tm, D_MODEL), lambda i, n: (i, 0)),
                  _mod_spec(layer, 0, tm), _mod_spec(layer, 1, tm),
                  pl.BlockSpec((D_MODEL, tn), lambda i, n: (0, n))],
        out_specs=pl.BlockSpec((tm, tn), lambda i, n: (i, n)),
        scratch_shapes=[pltpu.VMEM((tm, D_MODEL), BF16)],
        compiler_params=_params(("parallel", "arbitrary")),
        name="in_proj",
    )(x, mod, mod, w_bf)


def _cpow_table(re, im, n):
    out = [(jnp.ones_like(re), jnp.zeros_like(im))]
    for _ in range(n):
        pr, pi = out[-1]
        out.append((pr * re - pi * im, pr * im + pi * re))
    return out


def _lam_bar(lam):
    re = jnp.minimum(lam[0], -1e-4)
    im = lam[1]
    dt = jnp.exp(lam[2])
    mag = jnp.exp(re * dt)
    return re, im, mag * jnp.cos(im * dt), mag * jnp.sin(im * dt)


def _s5prep_kernel(lamc_ref, lamr_ref, ct_ref, bt_ref, b12_ref, toep_ref, s_ref, in_ref, co_ref):
    fwd = pl.program_id(0) == 0
    P, G, T = S5_STATE, S5_GROUP, S5_CHUNK

    lc = lamc_ref[...]
    _, _, lbr_c, lbi_c = _lam_bar((lc[:, 0:1], lc[:, 1:2], lc[:, 2:3]))
    pw_c = _cpow_table(lbr_c, lbi_c, T)

    def pick_cols(e_idx, shape):
        pr = jnp.zeros(shape, F32)
        pi = jnp.zeros(shape, F32)
        for e in range(T + 1):
            hit = e_idx == e
            pr = jnp.where(hit, pw_c[e][0], pr)
            pi = jnp.where(hit, pw_c[e][1], pi)
        return pr, pi

    wide = 2 * T * G
    lane = lax.broadcasted_iota(jnp.int32, (G, wide), 1)
    row = lax.broadcasted_iota(jnp.int32, (G, wide), 0)
    expand = (lane % G == row).astype(F32)
    ct_re = _dot_exact(ct_ref[0], expand)
    ct_im = _dot_exact(ct_ref[1], expand)

    blk = lax.broadcasted_iota(jnp.int32, (P, wide), 1) // G
    e_idx = jnp.where(fwd, blk - T, T - blk)
    pr, pi = pick_cols(e_idx, (P, wide))
    r_re = pr * ct_re - pi * ct_im
    r_im = pr * ct_im + pi * ct_re

    lr = lamr_ref[...]
    re_r, im_r, lbr_r, lbi_r = _lam_bar((lr[0:1, :], lr[1:2, :], lr[2:3, :]))
    den = re_r * re_r + im_r * im_r
    cf_re = ((lbr_r - 1.0) * re_r + lbi_r * im_r) / den
    cf_im = (lbi_r * re_r - (lbr_r - 1.0) * im_r) / den
    bb_re = cf_re[:, :P] * bt_ref[0] - cf_im[:, :P] * bt_ref[1]
    bb_im = cf_re[:, :P] * bt_ref[1] + cf_im[:, :P] * bt_ref[0]
    kext = _dot_exact(bb_re, r_re) - _dot_exact(bb_im, r_im)
    for i in range(T):
        off = (T - i) * G
        toep_ref[i * G:(i + 1) * G, :] = kext[:, off:off + T * G].astype(BF16)

    blk2 = lax.broadcasted_iota(jnp.int32, (P, T * G), 1) // G
    e_in = jnp.where(fwd, blk2 + 1, T - blk2)
    qr, qi = pick_cols(e_in, (P, T * G))
    c_re = ct_re[:, :T * G]
    c_im = ct_im[:, :T * G]
    in_ref[0:P, :] = (qr * c_re - qi * c_im).astype(BF16)
    in_ref[P:2 * P, :] = (-(qr * c_im + qi * c_re)).astype(BF16)

    pw_r = _cpow_table(lbr_r, lbi_r, T)
    b1 = b12_ref[0]
    b2 = b12_ref[1]
    for i in range(T):
        wr = jnp.where(fwd, pw_r[T - 1 - i][0], pw_r[i][0])
        wi = jnp.where(fwd, pw_r[T - 1 - i][1], pw_r[i][1])
        sr = wr * cf_re - wi * cf_im
        si = wr * cf_im + wi * cf_re
        s_ref[i * G:(i + 1) * G, :] = (sr * b1 + si * b2).astype(BF16)

    sgn = jnp.where(lax.broadcasted_iota(jnp.int32, (1, S5_SW), 1) < P, -1.0, 1.0)
    r16, i16 = pw_r[T]
    r256, i256 = r16, i16
    for _ in range(4):
        r256, i256 = r256 * r256 - i256 * i256, 2.0 * r256 * i256
    for k, rowv in enumerate((r16, sgn * i16, r256, sgn * i256)):
        co_ref[k:k + 1, :] = rowv
    co_ref[4:8, :] = jnp.zeros((4, S5_SW), F32)


def _s5_prep(lam_re, lam_im, log_dt, b_re, b_im, c_re, c_im):
    P, G = S5_STATE, S5_GROUP
    ldt = jnp.broadcast_to(log_dt[..., None], lam_re.shape)
    lamc = jnp.stack([lam_re, lam_im, ldt], axis=-1)
    dup = lambda a: jnp.concatenate([a, a], axis=-1)
    lamr = jnp.stack([dup(lam_re), dup(lam_im), dup(ldt)], axis=-2)
    ct = jnp.stack([jnp.swapaxes(c_re, -1, -2), jnp.swapaxes(c_im, -1, -2)], axis=2)
    bt_re = jnp.swapaxes(b_re, -1, -2)
    bt_im = jnp.swapaxes(b_im, -1, -2)
    bt = jnp.stack([bt_re, bt_im], axis=2)
    b12 = jnp.stack([jnp.concatenate([bt_re, bt_im], -1),
                     jnp.concatenate([-bt_im, bt_re], -1)], axis=2)
    blk = lambda *s: pl.BlockSpec((None, None) + s, lambda d, g: (d, g) + (0,) * len(s))
    return pl.pallas_call(
        _s5prep_kernel,
        out_shape=(jax.ShapeDtypeStruct((2, S5_GROUPS, S5_CW, S5_CW), BF16),
                   jax.ShapeDtypeStruct((2, S5_GROUPS, S5_CW, S5_SW), BF16),
                   jax.ShapeDtypeStruct((2, S5_GROUPS, S5_SW, S5_CW), BF16),
                   jax.ShapeDtypeStruct((2, S5_GROUPS, 8, S5_SW), F32)),
        grid=(2, S5_GROUPS),
        in_specs=[blk(P, 3), blk(3, 2 * P), blk(2, P, G), blk(2, G, P), blk(2, G, 2 * P)],
        out_specs=(blk(S5_CW, S5_CW), blk(S5_CW, S5_SW), blk(S5_SW, S5_CW), blk(8, S5_SW)),
        compiler_params=_params(("parallel", "parallel")),
        name="s5_prep",
    )(lamc, lamr, ct, bt, b12)


S5_GPS = 4
S5_LT = 1024


def _s5z_kernel(u_ref, s_ref, z_ref):
    for gg in range(S5_GPS):
        ug = u_ref[:, gg * S5_CW:(gg + 1) * S5_CW]
        for d in range(2):
            z_ref[d, :, gg * S5_SW:(gg + 1) * S5_SW] = _dot(ug, s_ref[d, gg])


def _s5_z(u_chunks, s_mat):
    return pl.pallas_call(
        _s5z_kernel,
        out_shape=jax.ShapeDtypeStruct((2, S5_ROWS, S5_GROUPS * S5_SW), F32),
        grid=(S5_GROUPS // S5_GPS,),
        in_specs=[pl.BlockSpec((S5_ROWS, S5_GPS * S5_CW), lambda g: (0, g)),
                  pl.BlockSpec((2, S5_GPS, S5_CW, S5_SW), lambda g: (0, g, 0, 0))],
        out_specs=pl.BlockSpec((2, S5_ROWS, S5_GPS * S5_SW), lambda g: (0, 0, g)),
        compiler_params=_params(("parallel",)),
        name="s5_chunk_state",
    )(u_chunks, s_mat)


def _s5scan_kernel(z_ref, co_ref, h0_ref, hin_ref, e_ref, g_ref):
    ns = S5_LT // S5_SW
    nq = S5_NSEQ
    lanes = [slice(j * S5_SW, (j + 1) * S5_SW) for j in range(ns)]

    def cmul(a1, a2, h):
        return a1 * h + a2 * pltpu.roll(h, S5_STATE, 1)

    for d in range(2):
        a16 = [(co_ref[d, 0:1, l], co_ref[d, 1:2, l]) for l in lanes]
        a256 = [(co_ref[d, 2:3, l], co_ref[d, 3:4, l]) for l in lanes]
        order = list(range(S5_NCHUNK)) if d == 0 else list(range(S5_NCHUNK - 1, -1, -1))

        def run(h, write):
            for n in order:
                rows = slice(n * nq, (n + 1) * nq)
                if write:
                    for j, l in enumerate(lanes):
                        hin_ref[d, rows, l] = h[j]
                h = [cmul(a16[j][0], a16[j][1], h[j]) + z_ref[d, rows, l] for j, l in enumerate(lanes)]
            return h

        end = run([jnp.zeros((nq, S5_SW), F32)] * ns, False)
        for j, l in enumerate(lanes):
            e_ref[d, :, l] = end[j]

        g_ref[0:BATCH, :] = jnp.zeros((BATCH, S5_LT), F32)
        segs = list(range(S5_SEGS)) if d == 0 else list(range(S5_SEGS - 1, -1, -1))
        for j, l in enumerate(lanes):
            for b in range(DEC_BATCH):
                g = jnp.broadcast_to(h0_ref[d, b:b + 1, l], (8, S5_SW))
                for k, s in enumerate(segs):
                    r = BATCH + b * S5_SEGS + s
                    g_ref[r:r + 1, l] = g[0:1, :]
                    if k + 1 < S5_SEGS:
                        g = cmul(a256[j][0], a256[j][1], g) + jnp.broadcast_to(e_ref[d, r:r + 1, l], (8, S5_SW))

        run([g_ref[:, l] for l in lanes], True)


def _s5_scan(z, coef, h0):
    nl = S5_GROUPS * S5_SW
    return pl.pallas_call(
        _s5scan_kernel,
        out_shape=(jax.ShapeDtypeStruct((2, S5_ROWS, nl), F32),
                   jax.ShapeDtypeStruct((2, S5_NSEQ, nl), F32)),
        grid=(nl // S5_LT,),
        in_specs=[pl.BlockSpec((2, S5_ROWS, S5_LT), lambda t: (0, 0, t)),
                  pl.BlockSpec((2, 8, S5_LT), lambda t: (0, 0, t)),
                  pl.BlockSpec((2, DEC_BATCH, S5_LT), lambda t: (0, 0, t))],
        out_specs=(pl.BlockSpec((2, S5_ROWS, S5_LT), lambda t: (0, 0, t)),
                   pl.BlockSpec((2, S5_NSEQ, S5_LT), lambda t: (0, 0, t))),
        scratch_shapes=[pltpu.VMEM((S5_NSEQ, S5_LT), F32)],
        compiler_params=_params(("parallel",)),
        name="s5_scan",
    )(z, coef, h0)


def _s5y_kernel(u_ref, t_ref, hin_ref, in_ref, y_ref):
    for gg in range(S5_GPS):
        ug = u_ref[:, gg * S5_CW:(gg + 1) * S5_CW]
        acc = None
        for d in range(2):
            h = hin_ref[d, :, gg * S5_SW:(gg + 1) * S5_SW].astype(BF16)
            t = _dot(ug, t_ref[d, gg]) + _dot(h, in_ref[d, gg])
            acc = t if acc is None else acc + t
        y_ref[:, gg * S5_CW:(gg + 1) * S5_CW] = acc


def _s5_y(u_chunks, toep, hin, in_mat):
    return pl.pallas_call(
        _s5y_kernel,
        out_shape=jax.ShapeDtypeStruct((S5_ROWS, S5_GROUPS * S5_CW), F32),
        grid=(S5_GROUPS // S5_GPS,),
        in_specs=[pl.BlockSpec((S5_ROWS, S5_GPS * S5_CW), lambda g: (0, g)),
                  pl.BlockSpec((2, S5_GPS, S5_CW, S5_CW), lambda g: (0, g, 0, 0)),
                  pl.BlockSpec((2, S5_ROWS, S5_GPS * S5_SW), lambda g: (0, 0, g)),
                  pl.BlockSpec((2, S5_GPS, S5_SW, S5_CW), lambda g: (0, g, 0, 0))],
        out_specs=pl.BlockSpec((S5_ROWS, S5_GPS * S5_CW), lambda g: (0, g)),
        compiler_params=_params(("parallel",)),
        name="s5_chunk_out",
    )(u_chunks, toep, hin, in_mat)


def _s5out_kernel(y_ref, u_ref, d_ref, w_ref, b_ref, o_ref):
    y = y_ref[...] + u_ref[...] * d_ref[...]
    y = y * (0.5 * (1.0 + jnp.tanh(math.sqrt(2.0 / math.pi) * (y + 0.044715 * (y * y * y)))))
    z = _dot(y.astype(BF16), w_ref[...]) + b_ref[...]
    o_ref[...] = (y * _sigmoid(z)).astype(BF16)


def _s5_out(y_scan, proj, s5_d, glu_w_bf, glu_b):
    tm = 512
    return pl.pallas_call(
        _s5out_kernel,
        out_shape=jax.ShapeDtypeStruct((N_TOK, S5_WIDTH), BF16),
        grid=(N_TOK // tm,),
        in_specs=[pl.BlockSpec((tm, S5_WIDTH), lambda i: (i, 0)),
                  pl.BlockSpec((tm, S5_WIDTH), lambda i: (i, 0)),
                  pl.BlockSpec((1, S5_WIDTH), lambda i: (0, 0)),
                  pl.BlockSpec((S5_WIDTH, S5_WIDTH), lambda i: (0, 0)),
                  pl.BlockSpec((1, S5_WIDTH), lambda i: (0, 0))],
        out_specs=pl.BlockSpec((tm, S5_WIDTH), lambda i: (i, 0)),
        compiler_params=_params(("parallel",)),
        name="s5_gelu_glu",
    )(y_scan, proj, s5_d.reshape(1, S5_WIDTH), glu_w_bf, glu_b.reshape(1, S5_WIDTH))


def _s5_mixer(proj, state_re, state_im, prep, s5_d, glu_w_bf, glu_b):
    toep, s_mat, in_mat, coef = prep
    u = proj[:, :S5_WIDTH].astype(BF16)
    u_chunks = (u.reshape(S5_NSEQ, S5_NCHUNK, S5_CHUNK, S5_GROUPS, S5_GROUP)
                .transpose(1, 0, 3, 2, 4).reshape(S5_ROWS, S5_GROUPS * S5_CW))
    z = _s5_z(u_chunks, s_mat)
    coef2 = coef.transpose(0, 2, 1, 3).reshape(2, 8, S5_GROUPS * S5_SW)
    h0 = jnp.concatenate([state_re, state_im], axis=-1)
    h0 = h0.transpose(1, 0, 2, 3).reshape(2, DEC_BATCH, S5_GROUPS * S5_SW)
    hin, ends = _s5_scan(z, coef2, h0)
    y = _s5_y(u_chunks, toep, hin, in_mat)
    y_tok = (y.reshape(S5_NCHUNK, S5_NSEQ, S5_GROUPS, S5_CHUNK, S5_GROUP)
             .transpose(1, 0, 3, 2, 4).reshape(N_TOK, S5_WIDTH))
    out = _s5_out(y_tok, proj, s5_d, glu_w_bf, glu_b)
    fin = ends[:, :BATCH].reshape(2, BATCH, S5_GROUPS, S5_SW).transpose(1, 0, 2, 3)
    return out, fin[..., :S5_STATE], fin[..., S5_STATE:]


def _softmax_rows(parts):
    m = parts[0].max(axis=-1, keepdims=True)
    for s in parts[1:]:
        m = jnp.maximum(m, s.max(axis=-1, keepdims=True))
    ps = [jnp.exp(s - m) for s in parts]
    tot = ps[0].sum(axis=-1, keepdims=True)
    for p in ps[1:]:
        tot = tot + p.sum(axis=-1, keepdims=True)
    inv = 1.0 / tot
    return [(p * inv).astype(BF16) for p in ps]


def _ctxattn_kernel(q_ref, k_ref, v_ref, o_ref, nk_ref, nv_ref):
    for h in range(NA_HEADS):
        sl = slice(h * NA_HEAD_DIM, (h + 1) * NA_HEAD_DIM)
        k = k_ref[:, sl]
        v = v_ref[:, sl]
        nk_ref[h] = k
        nv_ref[h] = v
        q = (q_ref[:, sl] * QK_SCALE).astype(BF16)
        (p,) = _softmax_rows([_dot_nt(q, k.astype(BF16))])
        o_ref[:, sl] = _dot(p, v.astype(BF16)).astype(BF16)


def _ctx_attention(proj):
    col = lambda c: pl.BlockSpec((SEQ, NA_WIDTH), lambda b: (b, c))
    cache = jax.ShapeDtypeStruct((BATCH, 1, NA_HEADS, SEQ, NA_HEAD_DIM), F32)
    cache_spec = pl.BlockSpec((None, None, NA_HEADS, SEQ, NA_HEAD_DIM), lambda b: (b, 0, 0, 0, 0))
    return pl.pallas_call(
        _ctxattn_kernel,
        out_shape=(jax.ShapeDtypeStruct((N_PROMPT, NA_WIDTH), BF16), cache, cache),
        grid=(BATCH,),
        in_specs=[col(1), col(2), col(3)],
        out_specs=(pl.BlockSpec((SEQ, NA_WIDTH), lambda b: (b, 0)), cache_spec, cache_spec),
        compiler_params=_params(("parallel",)),
        name="ctx_attention",
    )(proj, proj, proj)


def _na_geometry(r0):
    ks = min(max(r0 - NA_WIN_R // 2, 0), GRID_ROWS - NA_KROWS)
    tiles = []
    for a in range(NA_QROWS):
        rq = r0 + a
        rs = min(max(rq - NA_WIN_R // 2, 0), GRID_ROWS - NA_WIN_R)
        row = []
        for rl in range(NA_KROWS):
            rk = ks + rl
            row.append(rk - rq + NA_WIN_R - 1 if rs <= rk < rs + NA_WIN_R else None)
        tiles.append(row)
    return ks, tiles


NA_VARIANT_ROW0 = (0, NA_QROWS, GRID_ROWS - NA_QROWS)


def _nabias_kernel(rpb_ref, o_ref, t_ref):
    h = pl.program_id(0)
    shape = (GRID_W, 2 * GRID_W)
    cq = lax.broadcasted_iota(jnp.int32, shape, 0)
    lane = lax.broadcasted_iota(jnp.int32, shape, 1)
    ck = lane % GRID_W
    dc = jnp.clip(ck - cq + (NA_WIN_C - 1), 0, NA_NDC - 1)
    c0 = jnp.clip(cq - NA_WIN_C // 2, 0, GRID_W - NA_WIN_C)
    in_cols = (ck >= c0) & (ck < c0 + NA_WIN_C)
    neg = jnp.full(shape, NEG_INF, F32)
    for dr in range(NA_NDR):
        t = neg
        for v in range(NA_NDC):
            t = jnp.where(dc == v, rpb_ref[h, dr * NA_NDC + v], t)
        t_ref[dr] = jnp.where(in_cols, t, neg)
    left = lane < GRID_W
    for var, r0 in enumerate(NA_VARIANT_ROW0):
        _, tiles = _na_geometry(r0)
        for a in range(NA_QROWS):
            for m in range(NA_KROWS // 2):
                dl, dr_ = tiles[a][2 * m], tiles[a][2 * m + 1]
                tl = neg if dl is None else t_ref[dl]
                tr = neg if dr_ is None else t_ref[dr_]
                o_ref[var, a * GRID_W:(a + 1) * GRID_W, m * 2 * GRID_W:(m + 1) * 2 * GRID_W] = (
                    jnp.where(left, tl, tr))


def _na_bias(rpb):
    nq, nk = NA_QROWS * GRID_W, NA_KROWS * GRID_W
    return pl.pallas_call(
        _nabias_kernel,
        out_shape=jax.ShapeDtypeStruct((NA_HEADS, 3, nq, nk), F32),
        grid=(NA_HEADS,),
        in_specs=[pl.BlockSpec(memory_space=pltpu.SMEM)],
        out_specs=pl.BlockSpec((None, 3, nq, nk), lambda h: (h, 0, 0, 0)),
        scratch_shapes=[pltpu.VMEM((NA_NDR, GRID_W, 2 * GRID_W), F32)],
        compiler_params=_params(("parallel",)),
        name="na_bias",
    )(rpb.reshape(NA_HEADS, NA_NDR * NA_NDC))


def _naattn_kernel(q_ref, k_ref, v_ref, kc_ref, vc_ref, bias_ref, o_ref):
    kb = k_ref[...].astype(BF16)
    vb = v_ref[...].astype(BF16)
    kc = kc_ref[...].astype(BF16)
    vc = vc_ref[...].astype(BF16)
    nq = NA_QROWS * GRID_W
    for blk in range(GRID_ROWS // NA_QROWS):
        r0 = blk * NA_QROWS
        var = 0 if blk == 0 else (2 if r0 == NA_VARIANT_ROW0[2] else 1)
        ks, _ = _na_geometry(r0)
        keys = slice(ks * GRID_W, (ks + NA_KROWS) * GRID_W)
        q = (q_ref[r0 * GRID_W:r0 * GRID_W + nq, :] * QK_SCALE).astype(BF16)
        s_loc = _dot_nt(q, kb[keys]) + bias_ref[var]
        s_ctx = _dot_nt(q, kc)
        p_loc, p_ctx = _softmax_rows([s_loc, s_ctx])
        o = _dot(p_loc, vb[keys]) + _dot(p_ctx, vc)
        o_ref[r0 * GRID_W:r0 * GRID_W + nq, :] = o.astype(BF16)


def _na_attention(proj, cache_k, cache_v, bias):
    first = N_PROMPT // DEC_SEQ
    col = lambda c: pl.BlockSpec((DEC_SEQ, NA_HEAD_DIM), lambda b, h: (first + b, c * NA_HEADS + h))
    cache_spec = pl.BlockSpec((None, None, None, PAST_LEN, NA_HEAD_DIM), lambda b, h: (b, 0, h, 0, 0))
    return pl.pallas_call(
        _naattn_kernel,
        out_shape=jax.ShapeDtypeStruct((DEC_BATCH * DEC_SEQ, NA_WIDTH), BF16),
        grid=(DEC_BATCH, NA_HEADS),
        in_specs=[col(1), col(2), col(3), cache_spec, cache_spec,
                  pl.BlockSpec((None, 3, NA_QROWS * GRID_W, NA_KROWS * GRID_W), lambda b, h: (h, 0, 0, 0))],
        out_specs=pl.BlockSpec((DEC_SEQ, NA_HEAD_DIM), lambda b, h: (b, h)),
        compiler_params=_params(("parallel", "parallel")),
        name="na_attention",
    )(proj, proj, proj, cache_k, cache_v, bias)


def _outproj_kernel(a_ref, b_ref, w_ref, x_ref, gate_ref, g_ref, beta_ref, o_ref):
    half = a_ref.shape[1]
    y = _dot(a_ref[...], w_ref[0:half, :]) + _dot(b_ref[...], w_ref[half:2 * half, :])
    o_ref[...] = _post_norm(x_ref[...], y, gate_ref[...], g_ref[...], beta_ref[...])


def _out_proj(y_s5, y_att, w_bf, x, mod, ln_g, ln_b, layer):
    tm = 512
    half = y_s5.shape[1]
    return pl.pallas_call(
        _outproj_kernel,
        out_shape=jax.ShapeDtypeStruct((N_TOK, D_MODEL), F32),
        grid=(N_TOK // tm,),
        in_specs=[pl.BlockSpec((tm, half), lambda i: (i, 0)),
                  pl.BlockSpec((tm, half), lambda i: (i, 0)),
                  pl.BlockSpec((D_MODEL, D_MODEL), lambda i: (0, 0)),
                  pl.BlockSpec((tm, D_MODEL), lambda i: (i, 0)),
                  _mod_spec(layer, 2, tm), _row_spec(layer, 0), _row_spec(layer, 0)],
        out_specs=pl.BlockSpec((tm, D_MODEL), lambda i: (i, 0)),
        compiler_params=_params(("parallel",)),
        name="out_proj_norm",
    )(y_s5, y_att, w_bf, x, mod, ln_g, ln_b)


def _ffn_kernel(x_ref, sh_ref, sc_ref, gate_ref, g_ref, beta_ref, wg_ref, wu_ref, wd_ref, o_ref, h_ref, acc_ref):
    f = pl.program_id(1)

    @pl.when(f == 0)
    def _():
        h_ref[...] = (x_ref[...] * (1.0 + sc_ref[...]) + sh_ref[...]).astype(BF16)
        acc_ref[...] = jnp.zeros_like(acc_ref)

    h = h_ref[...]
    a = _dot(h, wg_ref[...])
    b = _dot(h, wu_ref[...])
    acc_ref[...] += _dot((a * _sigmoid(a) * b).astype(BF16), wd_ref[...])

    @pl.when(f == pl.num_programs(1) - 1)
    def _():
        o_ref[...] = _post_norm(x_ref[...], acc_ref[...], gate_ref[...], g_ref[...], beta_ref[...])


def _ffn(x, mod, ln_g, ln_b, wg_bf, wu_bf, wd_bf, layer):
    tm, tf = 512, 512
    return pl.pallas_call(
        _ffn_kernel,
        out_shape=jax.ShapeDtypeStruct((N_TOK, D_MODEL), F32),
        grid=(N_TOK // tm, FFN_DIM // tf),
        in_specs=[pl.BlockSpec((tm, D_MODEL), lambda i, f: (i, 0)),
                  _mod_spec(layer, 3, tm), _mod_spec(layer, 4, tm), _mod_spec(layer, 5, tm),
                  _row_spec(layer, 1), _row_spec(layer, 1),
                  pl.BlockSpec((D_MODEL, tf), lambda i, f: (0, f)),
                  pl.BlockSpec((D_MODEL, tf), lambda i, f: (0, f)),
                  pl.BlockSpec((tf, D_MODEL), lambda i, f: (f, 0))],
        out_specs=pl.BlockSpec((tm, D_MODEL), lambda i, f: (i, 0)),
        scratch_shapes=[pltpu.VMEM((tm, D_MODEL), BF16), pltpu.VMEM((tm, D_MODEL), F32)],
        compiler_params=_params(("parallel", "arbitrary")),
        name="ffn_norm",
    )(x, mod, mod, mod, ln_g, ln_b, wg_bf, wu_bf, wd_bf)


def _pool_kernel(x_ref, prev_ref, next_ref, sh_ref, sc_ref, gate_ref, g_ref, beta_ref, w_ref, ps_ref,
                 o_ref, ext_ref, y_ref):
    q = pl.program_id(0)
    latent = q >= BATCH
    seg = (q - BATCH) % S5_SEGS
    has_prev = latent & (seg > 0)
    has_next = latent & (seg < S5_SEGS - 1)
    seq_len = jnp.where(latent, DEC_SEQ, SEQ)
    t = jnp.where(latent, seg * SEQ, 0) + lax.broadcasted_iota(jnp.int32, (SEQ, 1), 0)

    scale = 1.0 + sc_ref[...]
    shift = sh_ref[...]
    x = x_ref[...]
    halo = POOL_HALO
    ext_ref[0:halo, :] = jnp.where(has_prev, prev_ref[...] * scale + shift, 0.0)
    ext_ref[halo:halo + SEQ, :] = x * scale + shift
    ext_ref[halo + SEQ:2 * halo + SEQ, :] = jnp.where(has_next, next_ref[...] * scale + shift, 0.0)

    for g, w in enumerate(POOL_SIZES):
        cols = slice(g * POOL_GROUP_DIM, (g + 1) * POOL_GROUP_DIM)
        total = None
        for k in range(-(w // 2), w - w // 2):
            part = ext_ref[halo + k:halo + k + SEQ, cols]
            total = part if total is None else total + part
        count = jnp.minimum(t + (w - w // 2), seq_len) - jnp.maximum(t - w // 2, 0)
        pooled = total / count.astype(F32) - ext_ref[halo:halo + SEQ, cols]
        y_ref[:, cols] = _dot(pooled.astype(BF16), w_ref[g])
    y = y_ref[...] * ps_ref[...]
    o_ref[...] = _post_norm(x, y, gate_ref[...], g_ref[...], beta_ref[...])


def _pool(x, mod, ln_g, ln_b, w_bf, pool_scale, layer):
    tm = SEQ
    nhb = N_TOK // POOL_HALO
    per = tm // POOL_HALO
    return pl.pallas_call(
        _pool_kernel,
        out_shape=jax.ShapeDtypeStruct((N_TOK, D_MODEL), F32),
        grid=(N_TOK // tm,),
        in_specs=[pl.BlockSpec((tm, D_MODEL), lambda i: (i, 0)),
                  pl.BlockSpec((POOL_HALO, D_MODEL), lambda i: (jnp.maximum(i * per - 1, 0), 0)),
                  pl.BlockSpec((POOL_HALO, D_MODEL), lambda i: (jnp.minimum((i + 1) * per, nhb - 1), 0)),
                  _mod_spec(layer, 0, tm), _mod_spec(layer, 1, tm), _mod_spec(layer, 2, tm),
                  _row_spec(layer, 0), _row_spec(layer, 0),
                  pl.BlockSpec((len(POOL_SIZES), POOL_GROUP_DIM, POOL_GROUP_DIM), lambda i: (0, 0, 0)),
                  pl.BlockSpec((1, D_MODEL), lambda i: (0, 0))],
        out_specs=pl.BlockSpec((tm, D_MODEL), lambda i: (i, 0)),
        scratch_shapes=[pltpu.VMEM((tm + 2 * POOL_HALO, D_MODEL), F32), pltpu.VMEM((tm, D_MODEL), F32)],
        compiler_params=_params(("parallel",)),
        name="pool_norm",
    )(x, x, x, mod, mod, mod, ln_g, ln_b, w_bf, pool_scale.reshape(1, D_MODEL))


ROUTER_LANES = 128


def _split_bf16(a):
    hi = a.astype(BF16)
    return hi, (a - hi.astype(F32)).astype(BF16)


def _router_kernel(x_ref, sh_ref, sc_ref, w_ref, b_ref, h_ref, info_ref):
    h = x_ref[...] * (1.0 + sc_ref[...]) + sh_ref[...]
    h_ref[...] = h
    hh, hl = _split_bf16(h)
    wh, wl = _split_bf16(w_ref[...])
    logits = _dot(hh, wh) + _dot(hl, wh) + _dot(hh, wl) + b_ref[...]
    lane = lax.broadcasted_iota(jnp.int32, logits.shape, 1)
    logits = jnp.where(lane < N_EXPERTS, logits, -jnp.inf)
    m1 = logits.max(axis=-1, keepdims=True)
    i1 = jnp.where(logits == m1, lane, ROUTER_LANES).min(axis=-1, keepdims=True)
    rest = jnp.where(lane == i1, -jnp.inf, logits)
    m2 = rest.max(axis=-1, keepdims=True)
    i2 = jnp.where(rest == m2, lane, ROUTER_LANES).min(axis=-1, keepdims=True)
    e = jnp.exp(m2 - m1)
    g1 = 1.0 / (1.0 + e)
    g2 = e / (1.0 + e)
    info = jnp.where(lane == 0, i1.astype(F32), jnp.where(lane == 1, i2.astype(F32),
                     jnp.where(lane == 2, g1, jnp.where(lane == 3, g2, 0.0))))
    info_ref[...] = info


def _router(x, mod, router_w, router_b, layer):
    tm = 512
    w = jnp.zeros((D_MODEL, ROUTER_LANES), F32).at[:, :N_EXPERTS].set(router_w)
    b = jnp.zeros((1, ROUTER_LANES), F32).at[0, :N_EXPERTS].set(router_b)
    return pl.pallas_call(
        _router_kernel,
        out_shape=(jax.ShapeDtypeStruct((N_TOK, D_MODEL), F32),
                   jax.ShapeDtypeStruct((N_TOK, ROUTER_LANES), F32)),
        grid=(N_TOK // tm,),
        in_specs=[pl.BlockSpec((tm, D_MODEL), lambda i: (i, 0)),
                  _mod_spec(layer, 3, tm), _mod_spec(layer, 4, tm),
                  pl.BlockSpec((D_MODEL, ROUTER_LANES), lambda i: (0, 0)),
                  pl.BlockSpec((1, ROUTER_LANES), lambda i: (0, 0))],
        out_specs=(pl.BlockSpec((tm, D_MODEL), lambda i: (i, 0)),
                   pl.BlockSpec((tm, ROUTER_LANES), lambda i: (i, 0))),
        compiler_params=_params(("parallel",)),
        name="moe_router",
    )(x, mod, mod, w, b)


def _routing_tables(info):
    experts = info[:, :2].astype(jnp.int32).reshape(-1)
    onehot = (experts[:, None] == jnp.arange(N_EXPERTS)[None, :]).astype(jnp.int32)
    counts = onehot.sum(axis=0)
    rank = ((jnp.cumsum(onehot, axis=0) - onehot) * onehot).sum(axis=1)
    padded = (counts + MOE_TILE - 1) // MOE_TILE * MOE_TILE
    ends = jnp.cumsum(padded)
    pos = (ends - padded)[experts] + rank
    src = jnp.zeros((MOE_ROWS,), jnp.int32).at[pos].set(jnp.arange(2 * N_TOK, dtype=jnp.int32) // 2)
    ntiles = (ends[-1] // MOE_TILE).astype(jnp.int32)
    tile_start = jnp.minimum(jnp.arange(MOE_NTILES, dtype=jnp.int32), ntiles - 1) * MOE_TILE
    tile_expert = jnp.sum(tile_start[:, None] >= ends[None, :], axis=1).astype(jnp.int32)
    pos = pos.reshape(N_TOK, 2).astype(jnp.int32)
    return src, tile_expert, ntiles.reshape(1), pos[:, 0], pos[:, 1]


def _row_copy(src_hbm, row, dst, r, sem):
    return pltpu.make_async_copy(src_hbm.at[pl.ds(row, 1), :], dst.at[pl.ds(r, 1), :], sem)


def _dispatch_kernel(src_ref, nt_ref, h_hbm, o_ref, buf, sem):
    m = pl.program_id(0)
    used = m < nt_ref[0]

    @pl.when(used)
    def _():
        def issue(r, c):
            _row_copy(h_hbm, src_ref[m * MOE_TILE + r], buf, r, sem).start()
            return c

        lax.fori_loop(0, MOE_TILE, issue, 0)

        def drain(r, c):
            _row_copy(h_hbm, 0, buf, r, sem).wait()
            return c

        lax.fori_loop(0, MOE_TILE, drain, 0)
        o_ref[...] = buf[...].astype(BF16)

    @pl.when(jnp.logical_not(used))
    def _():
        o_ref[...] = jnp.zeros_like(o_ref)


def _dispatch(h, src, ntiles):
    return pl.pallas_call(
        _dispatch_kernel,
        out_shape=jax.ShapeDtypeStruct((MOE_ROWS, D_MODEL), BF16),
        grid_spec=pltpu.PrefetchScalarGridSpec(
            num_scalar_prefetch=2,
            grid=(MOE_NTILES,),
            in_specs=[pl.BlockSpec(memory_space=pl.ANY)],
            out_specs=pl.BlockSpec((MOE_TILE, D_MODEL), lambda m, *_: (m, 0)),
            scratch_shapes=[pltpu.VMEM((MOE_TILE, D_MODEL), F32), pltpu.SemaphoreType.DMA(())],
        ),
        compiler_params=_params(("arbitrary",)),
        name="moe_dispatch",
    )(src, ntiles, h)


def _gmm_up_kernel(te_ref, nt_ref, x_ref, wg_ref, wu_ref, o_ref):
    used = pl.program_id(1) < nt_ref[0]

    @pl.when(used)
    def _():
        x = x_ref[...]
        a = _dot(x, wg_ref[...].astype(BF16))
        b = _dot(x, wu_ref[...].astype(BF16))
        o_ref[...] = (a * _sigmoid(a) * b).astype(BF16)

    @pl.when(jnp.logical_not(used))
    def _():
        o_ref[...] = jnp.zeros_like(o_ref)


def _gmm_up(xs, w_gate, w_up, tile_expert, ntiles):
    tf = 512
    row = lambda f, m, te, nt: (jnp.minimum(m, nt[0] - 1), 0)
    wsp = pl.BlockSpec((None, D_MODEL, tf), lambda f, m, te, nt: (te[m], 0, f))
    return pl.pallas_call(
        _gmm_up_kernel,
        out_shape=jax.ShapeDtypeStruct((MOE_ROWS, EXPERT_DIM), BF16),
        grid_spec=pltpu.PrefetchScalarGridSpec(
            num_scalar_prefetch=2,
            grid=(EXPERT_DIM // tf, MOE_NTILES),
            in_specs=[pl.BlockSpec((MOE_TILE, D_MODEL), row), wsp, wsp],
            out_specs=pl.BlockSpec((MOE_TILE, tf), lambda f, m, te, nt: (m, f)),
        ),
        compiler_params=_params(("parallel", "arbitrary")),
        name="moe_gate_up",
    )(tile_expert, ntiles, xs, w_gate, w_up)


def _gmm_down_kernel(te_ref, nt_ref, x_ref, w_ref, o_ref, wbf_ref):
    m = pl.program_id(1)
    used = m < nt_ref[0]
    fresh = jnp.logical_or(m == 0, te_ref[m] != te_ref[jnp.maximum(m - 1, 0)])

    @pl.when(jnp.logical_and(used, fresh))
    def _():
        wbf_ref[...] = w_ref[...].astype(BF16)

    @pl.when(used)
    def _():
        o_ref[...] = _dot(x_ref[...], wbf_ref[...])

    @pl.when(jnp.logical_not(used))
    def _():
        o_ref[...] = jnp.zeros_like(o_ref)


def _gmm_down(g, w_down, tile_expert, ntiles):
    tn = 512
    return pl.pallas_call(
        _gmm_down_kernel,
        out_shape=jax.ShapeDtypeStruct((MOE_ROWS, D_MODEL), F32),
        grid_spec=pltpu.PrefetchScalarGridSpec(
            num_scalar_prefetch=2,
            grid=(D_MODEL // tn, MOE_NTILES),
            in_specs=[pl.BlockSpec((MOE_TILE, EXPERT_DIM), lambda n, m, te, nt: (jnp.minimum(m, nt[0] - 1), 0)),
                      pl.BlockSpec((None, EXPERT_DIM, tn), lambda n, m, te, nt: (te[m], 0, n))],
            out_specs=pl.BlockSpec((MOE_TILE, tn), lambda n, m, te, nt: (m, n)),
            scratch_shapes=[pltpu.VMEM((EXPERT_DIM, tn), BF16)],
        ),
        compiler_params=_params(("parallel", "arbitrary")),
        name="moe_down",
    )(tile_expert, ntiles, g, w_down)


def _combine_kernel(p1_ref, p2_ref, y_hbm, x_ref, info_ref, gate_ref, g_ref, beta_ref, o_ref, b1, b2, sem):
    i = pl.program_id(0)
    tm = x_ref.shape[0]

    def issue(r, c):
        _row_copy(y_hbm, p1_ref[i * tm + r], b1, r, sem).start()
        _row_copy(y_hbm, p2_ref[i * tm + r], b2, r, sem).start()
        return c

    lax.fori_loop(0, tm, issue, 0)

    def drain(r, c):
        _row_copy(y_hbm, 0, b1, r, sem).wait()
        _row_copy(y_hbm, 0, b2, r, sem).wait()
        return c

    lax.fori_loop(0, tm, drain, 0)
    info = info_ref[...]
    y = info[:, 2:3] * b1[...] + info[:, 3:4] * b2[...]
    o_ref[...] = _post_norm(x_ref[...], y, gate_ref[...], g_ref[...], beta_ref[...])


def _combine(y_sorted, pos1, pos2, x, info, mod, ln_g, ln_b, layer):
    tm = 256
    return pl.pallas_call(
        _combine_kernel,
        out_shape=jax.ShapeDtypeStruct((N_TOK, D_MODEL), F32),
        grid_spec=pltpu.PrefetchScalarGridSpec(
            num_scalar_prefetch=2,
            grid=(N_TOK // tm,),
            in_specs=[pl.BlockSpec(memory_space=pl.ANY),
                      pl.BlockSpec((tm, D_MODEL), lambda i, *_: (i, 0)),
                      pl.BlockSpec((tm, ROUTER_LANES), lambda i, *_: (i, 0)),
                      _mod_spec(layer, 5, tm), _row_spec(layer, 1), _row_spec(layer, 1)],
            out_specs=pl.BlockSpec((tm, D_MODEL), lambda i, *_: (i, 0)),
            scratch_shapes=[pltpu.VMEM((tm, D_MODEL), F32), pltpu.VMEM((tm, D_MODEL), F32),
                            pltpu.SemaphoreType.DMA(())],
        ),
        compiler_params=_params(("arbitrary",)),
        name="moe_combine_norm",
    )(pos1, pos2, y_sorted, x, info, mod, ln_g, ln_b)


def _moe(x, mod, ln_g, ln_b, router_w, router_b, w_gate, w_up, w_down, layer):
    h, info = _router(x, mod, router_w, router_b, layer)
    src, tile_expert, ntiles, pos1, pos2 = _routing_tables(info)
    xs = _dispatch(h, src, ntiles)
    g = _gmm_up(xs, w_gate, w_up, tile_expert, ntiles)
    y = _gmm_down(g, w_down, tile_expert, ntiles)
    return _combine(y, pos1, pos2, x, info, mod, ln_g, ln_b, layer)


def kernel(x_prompt, x_sample, state_s5_re, state_s5_im, cache_k, cache_v, c, c_ctx, ada_w, ada_b, ln_g, ln_b, ab_w_in, ab_w_out, s5_lambda_re, s5_lambda_im, s5_log_dt, s5_b_re, s5_b_im, s5_c_re, s5_c_im, s5_d, s5_glu_w, s5_glu_b, na_rpb, ffn_w_gate, ffn_w_up, ffn_w_down, pool_w, pool_scale, moe_router_w, moe_router_b, moe_w_gate, moe_w_up, moe_w_down):
    x = jnp.concatenate([x_prompt.reshape(N_PROMPT, D_MODEL), x_sample.reshape(-1, D_MODEL)], axis=0)

    cond8 = jnp.zeros((8, D_MODEL), F32).at[0].set(c_ctx).at[1:N_COND].set(c)
    mod = _ada(cond8, ada_w, ada_b)
    mod = mod[:, :N_COND].reshape(DEPTH, N_COND, N_MOD, 1, D_MODEL).transpose(0, 2, 1, 3, 4)
    ln_g4 = ln_g.reshape(DEPTH, 2, 1, D_MODEL)
    ln_b4 = ln_b.reshape(DEPTH, 2, 1, D_MODEL)

    proj = _inproj(x, mod, _cast_bf16(ab_w_in[0]), 0)
    prep = _s5_prep(s5_lambda_re[0], s5_lambda_im[0], s5_log_dt[0], s5_b_re[0], s5_b_im[0],
                    s5_c_re[0], s5_c_im[0])
    y_s5, fin_re, fin_im = _s5_mixer(proj, state_s5_re[:, 0], state_s5_im[:, 0], prep, s5_d[0],
                                     _cast_bf16(s5_glu_w[0]), s5_glu_b[0])
    y_ctx, new_k, new_v = _ctx_attention(proj)
    y_na = _na_attention(proj, cache_k, cache_v, _na_bias(na_rpb[0]))
    y_att = jnp.concatenate([y_ctx, y_na], axis=0)
    x = _out_proj(y_s5, y_att, _cast_bf16(ab_w_out[0]), x, mod, ln_g4, ln_b4, 0)
    x = _ffn(x, mod, ln_g4, ln_b4, _cast_bf16(ffn_w_gate[0]), _cast_bf16(ffn_w_up[0]),
             _cast_bf16(ffn_w_down[0]), 0)

    x = _pool(x, mod, ln_g4, ln_b4, _cast_bf16(pool_w[0]), pool_scale[0], 1)
    x = _moe(x, mod, ln_g4, ln_b4, moe_router_w[0], moe_router_b[0], moe_w_gate[0], moe_w_up[0],
             moe_w_down[0], 1)

    y_prompt = x[:N_PROMPT].reshape(BATCH, SEQ, D_MODEL)
    y_sample = x[N_PROMPT:].reshape(DEC_BATCH, DEC_SEQ, D_MODEL)
    return (y_prompt, y_sample, fin_re[:, None], fin_im[:, None], new_k, new_v)
```

```python
import functools
import math

import jax
import jax.numpy as jnp
from jax import lax
from jax.experimental import pallas as pl
from jax.experimental.pallas import tpu as pltpu

F32 = jnp.float32
BF16 = jnp.bfloat16

D_MODEL = 2048
BATCH = 16
SEQ = 256
DEPTH = 2
DEC_BATCH = 2
DEC_SEQ = 2048
N_MOD = 6
N_PROMPT = BATCH * SEQ
N_TOK = N_PROMPT + DEC_BATCH * DEC_SEQ
N_COND = 1 + DEC_BATCH

S5_WIDTH = 1024
S5_GROUP = 16
S5_GROUPS = 64
S5_STATE = 64
S5_CHUNK = 16
S5_SEQ = 256
S5_NSEQ = N_TOK // S5_SEQ
S5_NCHUNK = S5_SEQ // S5_CHUNK
S5_ROWS = S5_NSEQ * S5_NCHUNK
S5_SEGS = DEC_SEQ // S5_SEQ
S5_CW = S5_CHUNK * S5_GROUP
S5_SW = 2 * S5_STATE

NA_WIDTH = 1024
NA_HEADS = 8
NA_HEAD_DIM = 128
NA_WIN_R = 8
NA_WIN_C = 16
GRID_W = 64
GRID_ROWS = DEC_SEQ // GRID_W
NA_QROWS = 4
NA_KROWS = 12
NA_NDR = 2 * NA_WIN_R - 1
NA_NDC = 2 * NA_WIN_C - 1
PAST_LEN = 256

POOL_SIZES = (2, 4, 8, 16)
POOL_GROUP_DIM = 512
POOL_HALO = 16

FFN_DIM = 5632
N_EXPERTS = 8
EXPERT_DIM = 7168
MOE_TILE = 256
MOE_ROWS = 2 * N_TOK + N_EXPERTS * MOE_TILE
MOE_NTILES = MOE_ROWS // MOE_TILE

LN_EPS = 1e-5
DEEPNORM_ALPHA = (2.0 * DEPTH) ** 0.25
NEG_INF = -1e30
QK_SCALE = NA_HEAD_DIM ** -0.5

VMEM_LIMIT = 52 * 1024 * 1024


def _params(sem, vmem=VMEM_LIMIT):
    return pltpu.CompilerParams(dimension_semantics=sem, vmem_limit_bytes=vmem)


def _sigmoid(x):
    return 1.0 / (1.0 + jnp.exp(-x))


def _dot(a, b):
    return jnp.dot(a, b, preferred_element_type=F32)


def _dot_nt(a, b):
    return lax.dot_general(a, b, (((1,), (1,)), ((), ())), preferred_element_type=F32)


def _dot_exact(a, b):
    return jnp.dot(a, b, preferred_element_type=F32, precision=lax.Precision.HIGHEST)


def _cond_of_row(row0):
    return jnp.where(row0 < N_PROMPT, 0, 1 + (row0 - N_PROMPT) // DEC_SEQ)


def _mod_spec(layer, k, tm):
    return pl.BlockSpec((None, None, None, 1, D_MODEL),
                        lambda i, *_: (layer, k, _cond_of_row(i * tm), 0, 0))


def _row_spec(layer_idx, k=None):
    if k is None:
        return pl.BlockSpec((None, 1, D_MODEL), lambda *_: (layer_idx, 0, 0))
    return pl.BlockSpec((None, None, 1, D_MODEL), lambda *_: (layer_idx, k, 0, 0))


def _post_norm(x, y, gate, g, b):
    v = DEEPNORM_ALPHA * x + gate * y
    mu = jnp.mean(v, axis=-1, keepdims=True)
    c = v - mu
    var = jnp.mean(c * c, axis=-1, keepdims=True)
    return c * lax.rsqrt(var + LN_EPS) * g + b


def _ada_kernel(c_ref, w_ref, b_ref, o_ref):
    c = c_ref[...]
    s = (c * _sigmoid(c)).astype(BF16)
    o_ref[...] = _dot(s, w_ref[...].astype(BF16)) + b_ref[...]


def _ada(cond8, ada_w, ada_b):
    tn = 1024
    n_out = N_MOD * D_MODEL
    return pl.pallas_call(
        _ada_kernel,
        out_shape=jax.ShapeDtypeStruct((DEPTH, 8, n_out), F32),
        grid=(DEPTH, n_out // tn),
        in_specs=[pl.BlockSpec((8, D_MODEL), lambda l, n: (0, 0)),
                  pl.BlockSpec((None, D_MODEL, tn), lambda l, n: (l, 0, n)),
                  pl.BlockSpec((None, 1, tn), lambda l, n: (l, 0, n))],
        out_specs=pl.BlockSpec((None, 8, tn), lambda l, n: (l, 0, n)),
        compiler_params=_params(("parallel", "parallel")),
        name="ada_mod",
    )(cond8, ada_w, ada_b.reshape(DEPTH, 1, n_out))


def _cast_kernel(w_ref, o_ref):
    o_ref[...] = w_ref[...].astype(BF16)


def _cast_bf16(w):
    shape = w.shape
    cols = shape[-1]
    rows = w.size // cols
    rb = 8
    while rb * 2 * cols * 4 <= 4 * 1024 * 1024 and rows % (rb * 2) == 0:
        rb *= 2
    out = pl.pallas_call(
        _cast_kernel,
        out_shape=jax.ShapeDtypeStruct((rows, cols), BF16),
        grid=(rows // rb,),
        in_specs=[pl.BlockSpec((rb, cols), lambda i: (i, 0))],
        out_specs=pl.BlockSpec((rb, cols), lambda i: (i, 0)),
        compiler_params=_params(("parallel",)),
        name="cast_bf16",
    )(w.reshape(rows, cols))
    return out.reshape(shape)


def _inproj_kernel(x_ref, sh_ref, sc_ref, w_ref, o_ref, h_ref):
    @pl.when(pl.program_id(1) == 0)
    def _():
        h_ref[...] = (x_ref[...] * (1.0 + sc_ref[...]) + sh_ref[...]).astype(BF16)

    o_ref[...] = _dot(h_ref[...], w_ref[...])


def _inproj(x, mod, w_bf, layer):
    tm, tn = 512, 1024
    n_out = w_bf.shape[1]
    return pl.pallas_call(
        _inproj_kernel,
        out_shape=jax.ShapeDtypeStruct((N_TOK, n_out), F32),
        grid=(N_TOK // tm, n_out // tn),
        in_specs=[pl.BlockSpec((tm, D_MODEL), lambda i, n: (i, 0)),
                  _mod_spec(layer, 0, tm), _mod_spec(layer, 1, tm),
                  pl.BlockSpec((D_MODEL, tn), lambda i, n: (0, n))],
        out_specs=pl.BlockSpec((tm, tn), lambda i, n: (i, n)),
        scratch_shapes=[pltpu.VMEM((tm, D_MODEL), BF16)],
        compiler_params=_params(("parallel", "arbitrary")),
        name="in_proj",
    )(x, mod, mod, w_bf)


def _cpow_table(re, im, n):
    out = [(jnp.ones_like(re), jnp.zeros_like(im))]
    for _ in range(n):
        pr, pi = out[-1]
        out.append((pr * re - pi * im, pr * im + pi * re))
    return out


def _lam_bar(lam):
    re = jnp.minimum(lam[0], -1e-4)
    im = lam[1]
    dt = jnp.exp(lam[2])
    mag = jnp.exp(re * dt)
    return re, im, mag * jnp.cos(im * dt), mag * jnp.sin(im * dt)


def _s5prep_kernel(lamc_ref, lamr_ref, ct_ref, bt_ref, b12_ref, toep_ref, s_ref, in_ref, co_ref):
    fwd = pl.program_id(0) == 0
    P, G, T = S5_STATE, S5_GROUP, S5_CHUNK

    lc = lamc_ref[...]
    _, _, lbr_c, lbi_c = _lam_bar((lc[:, 0:1], lc[:, 1:2], lc[:, 2:3]))
    pw_c = _cpow_table(lbr_c, lbi_c, T)

    def pick_cols(e_idx, shape):
        pr = jnp.zeros(shape, F32)
        pi = jnp.zeros(shape, F32)
        for e in range(T + 1):
            hit = e_idx == e
            pr = jnp.where(hit, pw_c[e][0], pr)
            pi = jnp.where(hit, pw_c[e][1], pi)
        return pr, pi

    wide = 2 * T * G
    lane = lax.broadcasted_iota(jnp.int32, (G, wide), 1)
    row = lax.broadcasted_iota(jnp.int32, (G, wide), 0)
    expand = (lane % G == row).astype(F32)
    ct_re = _dot_exact(ct_ref[0], expand)
    ct_im = _dot_exact(ct_ref[1], expand)

    blk = lax.broadcasted_iota(jnp.int32, (P, wide), 1) // G
    e_idx = jnp.where(fwd, blk - T, T - blk)
    pr, pi = pick_cols(e_idx, (P, wide))
    r_re = pr * ct_re - pi * ct_im
    r_im = pr * ct_im + pi * ct_re

    lr = lamr_ref[...]
    re_r, im_r, lbr_r, lbi_r = _lam_bar((lr[0:1, :], lr[1:2, :], lr[2:3, :]))
    den = re_r * re_r + im_r * im_r
    cf_re = ((lbr_r - 1.0) * re_r + lbi_r * im_r) / den
    cf_im = (lbi_r * re_r - (lbr_r - 1.0) * im_r) / den
    bb_re = cf_re[:, :P] * bt_ref[0] - cf_im[:, :P] * bt_ref[1]
    bb_im = cf_re[:, :P] * bt_ref[1] + cf_im[:, :P] * bt_ref[0]
    kext = _dot_exact(bb_re, r_re) - _dot_exact(bb_im, r_im)
    for i in range(T):
        off = (T - i) * G
        toep_ref[i * G:(i + 1) * G, :] = kext[:, off:off + T * G].astype(BF16)

    blk2 = lax.broadcasted_iota(jnp.int32, (P, T * G), 1) // G
    e_in = jnp.where(fwd, blk2 + 1, T - blk2)
    qr, qi = pick_cols(e_in, (P, T * G))
    c_re = ct_re[:, :T * G]
    c_im = ct_im[:, :T * G]
    in_ref[0:P, :] = (qr * c_re - qi * c_im).astype(BF16)
    in_ref[P:2 * P, :] = (-(qr * c_im + qi * c_re)).astype(BF16)

    pw_r = _cpow_table(lbr_r, lbi_r, T)
    b1 = b12_ref[0]
    b2 = b12_ref[1]
    for i in range(T):
        wr = jnp.where(fwd, pw_r[T - 1 - i][0], pw_r[i][0])
        wi = jnp.where(fwd, pw_r[T - 1 - i][1], pw_r[i][1])
        sr = wr * cf_re - wi * cf_im
        si = wr * cf_im + wi * cf_re
        s_ref[i * G:(i + 1) * G, :] = (sr * b1 + si * b2).astype(BF16)

    sgn = jnp.where(lax.broadcasted_iota(jnp.int32, (1, S5_SW), 1) < P, -1.0, 1.0)
    r16, i16 = pw_r[T]
    r256, i256 = r16, i16
    for _ in range(4):
        r256, i256 = r256 * r256 - i256 * i256, 2.0 * r256 * i256
    for k, rowv in enumerate((r16, sgn * i16, r256, sgn * i256)):
        co_ref[k:k + 1, :] = rowv
    co_ref[4:8, :] = jnp.zeros((4, S5_SW), F32)


def _s5_prep(lam_re, lam_im, log_dt, b_re, b_im, c_re, c_im):
    P, G = S5_STATE, S5_GROUP
    ldt = jnp.broadcast_to(log_dt[..., None], lam_re.shape)
    lamc = jnp.stack([lam_re, lam_im, ldt], axis=-1)
    dup = lambda a: jnp.concatenate([a, a], axis=-1)
    lamr = jnp.stack([dup(lam_re), dup(lam_im), dup(ldt)], axis=-2)
    ct = jnp.stack([jnp.swapaxes(c_re, -1, -2), jnp.swapaxes(c_im, -1, -2)], axis=2)
    bt_re = jnp.swapaxes(b_re, -1, -2)
    bt_im = jnp.swapaxes(b_im, -1, -2)
    bt = jnp.stack([bt_re, bt_im], axis=2)
    b12 = jnp.stack([jnp.concatenate([bt_re, bt_im], -1),
                     jnp.concatenate([-bt_im, bt_re], -1)], axis=2)
    blk = lambda *s: pl.BlockSpec((None, None) + s, lambda d, g: (d, g) + (0,) * len(s))
    return pl.pallas_call(
        _s5prep_kernel,
        out_shape=(jax.ShapeDtypeStruct((2, S5_GROUPS, S5_CW, S5_CW), BF16),
                   jax.ShapeDtypeStruct((2, S5_GROUPS, S5_CW, S5_SW), BF16),
                   jax.ShapeDtypeStruct((2, S5_GROUPS, S5_SW, S5_CW), BF16),
                   jax.ShapeDtypeStruct((2, S5_GROUPS, 8, S5_SW), F32)),
        grid=(2, S5_GROUPS),
        in_specs=[blk(P, 3), blk(3, 2 * P), blk(2, P, G), blk(2, G, P), blk(2, G, 2 * P)],
        out_specs=(blk(S5_CW, S5_CW), blk(S5_CW, S5_SW), blk(S5_SW, S5_CW), blk(8, S5_SW)),
        compiler_params=_params(("parallel", "parallel")),
        name="s5_prep",
    )(lamc, lamr, ct, bt, b12)


S5_GPS = 4
S5_LT = 1024


def _s5z_kernel(u_ref, s_ref, z_ref):
    for gg in range(S5_GPS):
        ug = u_ref[:, gg * S5_CW:(gg + 1) * S5_CW]
        for d in range(2):
            z_ref[d, :, gg * S5_SW:(gg + 1) * S5_SW] = _dot(ug, s_ref[d, gg])


def _s5_z(u_chunks, s_mat):
    return pl.pallas_call(
        _s5z_kernel,
        out_shape=jax.ShapeDtypeStruct((2, S5_ROWS, S5_GROUPS * S5_SW), F32),
        grid=(S5_GROUPS // S5_GPS,),
        in_specs=[pl.BlockSpec((S5_ROWS, S5_GPS * S5_CW), lambda g: (0, g)),
                  pl.BlockSpec((2, S5_GPS, S5_CW, S5_SW), lambda g: (0, g, 0, 0))],
        out_specs=pl.BlockSpec((2, S5_ROWS, S5_GPS * S5_SW), lambda g: (0, 0, g)),
        compiler_params=_params(("parallel",)),
        name="s5_chunk_state",
    )(u_chunks, s_mat)


def _s5scan_kernel(z_ref, co_ref, h0_ref, hin_ref, e_ref, g_ref):
    ns = S5_LT // S5_SW
    nq = S5_NSEQ
    lanes = [slice(j * S5_SW, (j + 1) * S5_SW) for j in range(ns)]

    def cmul(a1, a2, h):
        return a1 * h + a2 * pltpu.roll(h, S5_STATE, 1)

    for d in range(2):
        a16 = [(co_ref[d, 0:1, l], co_ref[d, 1:2, l]) for l in lanes]
        a256 = [(co_ref[d, 2:3, l], co_ref[d, 3:4, l]) for l in lanes]
        order = list(range(S5_NCHUNK)) if d == 0 else list(range(S5_NCHUNK - 1, -1, -1))

        def run(h, write):
            for n in order:
                rows = slice(n * nq, (n + 1) * nq)
                if write:
                    for j, l in enumerate(lanes):
                        hin_ref[d, rows, l] = h[j]
                h = [cmul(a16[j][0], a16[j][1], h[j]) + z_ref[d, rows, l] for j, l in enumerate(lanes)]
            return h

        end = run([jnp.zeros((nq, S5_SW), F32)] * ns, False)
        for j, l in enumerate(lanes):
            e_ref[d, :, l] = end[j]

        g_ref[0:BATCH, :] = jnp.zeros((BATCH, S5_LT), F32)
        segs = list(range(S5_SEGS)) if d == 0 else list(range(S5_SEGS - 1, -1, -1))
        for j, l in enumerate(lanes):
            for b in range(DEC_BATCH):
                g = jnp.broadcast_to(h0_ref[d, b:b + 1, l], (8, S5_SW))
                for k, s in enumerate(segs):
                    r = BATCH + b * S5_SEGS + s
                    g_ref[r:r + 1, l] = g[0:1, :]
                    if k + 1 < S5_SEGS:
                        g = cmul(a256[j][0], a256[j][1], g) + jnp.broadcast_to(e_ref[d, r:r + 1, l], (8, S5_SW))

        run([g_ref[:, l] for l in lanes], True)


def _s5_scan(z, coef, h0):
    nl = S5_GROUPS * S5_SW
    return pl.pallas_call(
        _s5scan_kernel,
        out_shape=(jax.ShapeDtypeStruct((2, S5_ROWS, nl), F32),
                   jax.ShapeDtypeStruct((2, S5_NSEQ, nl), F32)),
        grid=(nl // S5_LT,),
        in_specs=[pl.BlockSpec((2, S5_ROWS, S5_LT), lambda t: (0, 0, t)),
                  pl.BlockSpec((2, 8, S5_LT), lambda t: (0, 0, t)),
                  pl.BlockSpec((2, DEC_BATCH, S5_LT), lambda t: (0, 0, t))],
        out_specs=(pl.BlockSpec((2, S5_ROWS, S5_LT), lambda t: (0, 0, t)),
                   pl.BlockSpec((2, S5_NSEQ, S5_LT), lambda t: (0, 0, t))),
        scratch_shapes=[pltpu.VMEM((S5_NSEQ, S5_LT), F32)],
        compiler_params=_params(("parallel",)),
        name="s5_scan",
    )(z, coef, h0)


def _s5y_kernel(u_ref, t_ref, hin_ref, in_ref, y_ref):
    for gg in range(S5_GPS):
        ug = u_ref[:, gg * S5_CW:(gg + 1) * S5_CW]
        acc = None
        for d in range(2):
            h = hin_ref[d, :, gg * S5_SW:(gg + 1) * S5_SW].astype(BF16)
            t = _dot(ug, t_ref[d, gg]) + _dot(h, in_ref[d, gg])
            acc = t if acc is None else acc + t
        y_ref[:, gg * S5_CW:(gg + 1) * S5_CW] = acc


def _s5_y(u_chunks, toep, hin, in_mat):
    return pl.pallas_call(
        _s5y_kernel,
        out_shape=jax.ShapeDtypeStruct((S5_ROWS, S5_GROUPS * S5_CW), F32),
        grid=(S5_GROUPS // S5_GPS,),
        in_specs=[pl.BlockSpec((S5_ROWS, S5_GPS * S5_CW), lambda g: (0, g)),
                  pl.BlockSpec((2, S5_GPS, S5_CW, S5_CW), lambda g: (0, g, 0, 0)),
                  pl.BlockSpec((2, S5_ROWS, S5_GPS * S5_SW), lambda g: (0, 0, g)),
                  pl.BlockSpec((2, S5_GPS, S5_SW, S5_CW), lambda g: (0, g, 0, 0))],
        out_specs=pl.BlockSpec((S5_ROWS, S5_GPS * S5_CW), lambda g: (0, g)),
        compiler_params=_params(("parallel",)),
        name="s5_chunk_out",
    )(u_chunks, toep, hin, in_mat)


def _s5out_kernel(y_ref, u_ref, d_ref, w_ref, b_ref, o_ref):
    y = y_ref[...] + u_ref[...] * d_ref[...]
    y = y * (0.5 * (1.0 + jnp.tanh(math.sqrt(2.0 / math.pi) * (y + 0.044715 * (y * y * y)))))
    z = _dot(y.astype(BF16), w_ref[...]) + b_ref[...]
    o_ref[...] = (y * _sigmoid(z)).astype(BF16)


def _s5_out(y_scan, proj, s5_d, glu_w_bf, glu_b):
    tm = 512
    return pl.pallas_call(
        _s5out_kernel,
        out_shape=jax.ShapeDtypeStruct((N_TOK, S5_WIDTH), BF16),
        grid=(N_TOK // tm,),
        in_specs=[pl.BlockSpec((tm, S5_WIDTH), lambda i: (i, 0)),
                  pl.BlockSpec((tm, S5_WIDTH), lambda i: (i, 0)),
                  pl.BlockSpec((1, S5_WIDTH), lambda i: (0, 0)),
                  pl.BlockSpec((S5_WIDTH, S5_WIDTH), lambda i: (0, 0)),
                  pl.BlockSpec((1, S5_WIDTH), lambda i: (0, 0))],
        out_specs=pl.BlockSpec((tm, S5_WIDTH), lambda i: (i, 0)),
        compiler_params=_params(("parallel",)),
        name="s5_gelu_glu",
    )(y_scan, proj, s5_d.reshape(1, S5_WIDTH), glu_w_bf, glu_b.reshape(1, S5_WIDTH))


S5_LB = 128
S5_LBG = S5_LB // S5_GROUP


def _s5chunks_kernel(p_ref, o_ref, x_ref, sel_ref):
    kw = S5_CHUNK * S5_LB
    pad = (S5_LBG - 1) * S5_GROUP

    @pl.when(pl.program_id(0) == 0)
    def _():
        k = lax.broadcasted_iota(jnp.int32, sel_ref.shape, 0) - pad
        lane = lax.broadcasted_iota(jnp.int32, sel_ref.shape, 1)
        hit = (k >= 0) & ((k % S5_LB) // S5_GROUP == 0) & (lane == (k // S5_LB) * S5_GROUP + k % S5_GROUP)
        sel_ref[...] = jnp.where(hit, 1.0, 0.0).astype(BF16)

    for n in range(S5_NCHUNK):
        for i in range(S5_CHUNK):
            rows = p_ref[pl.ds(n * S5_CHUNK + i, S5_NSEQ, stride=S5_SEQ), :]
            x_ref[n * S5_NSEQ:(n + 1) * S5_NSEQ, i * S5_LB:(i + 1) * S5_LB] = rows.astype(BF16)
    x = x_ref[...]
    for gl in range(S5_LBG):
        off = pad - gl * S5_GROUP
        o_ref[:, gl * S5_CW:(gl + 1) * S5_CW] = _dot(x, sel_ref[off:off + kw, :]).astype(BF16)


def _s5_chunks(proj):
    kw = S5_CHUNK * S5_LB
    return pl.pallas_call(
        _s5chunks_kernel,
        out_shape=jax.ShapeDtypeStruct((S5_ROWS, S5_GROUPS * S5_CW), BF16),
        grid=(S5_WIDTH // S5_LB,),
        in_specs=[pl.BlockSpec((N_TOK, S5_LB), lambda b: (0, b))],
        out_specs=pl.BlockSpec((S5_ROWS, kw), lambda b: (0, b)),
        scratch_shapes=[pltpu.VMEM((S5_ROWS, kw), BF16), pltpu.VMEM((kw + S5_LB, S5_CW), BF16)],
        compiler_params=_params(("arbitrary",)),
        name="s5_chunk_layout",
    )(proj)


def _s5_mixer(proj, state_re, state_im, prep, s5_d, glu_w_bf, glu_b):
    toep, s_mat, in_mat, coef = prep
    u_chunks = _s5_chunks(proj)
    z = _s5_z(u_chunks, s_mat)
    coef2 = coef.transpose(0, 2, 1, 3).reshape(2, 8, S5_GROUPS * S5_SW)
    h0 = jnp.concatenate([state_re, state_im], axis=-1)
    h0 = h0.transpose(1, 0, 2, 3).reshape(2, DEC_BATCH, S5_GROUPS * S5_SW)
    hin, ends = _s5_scan(z, coef2, h0)
    y = _s5_y(u_chunks, toep, hin, in_mat)
    y_tok = (y.reshape(S5_NCHUNK, S5_NSEQ, S5_GROUPS, S5_CHUNK, S5_GROUP)
             .transpose(1, 0, 3, 2, 4).reshape(N_TOK, S5_WIDTH))
    out = _s5_out(y_tok, proj, s5_d, glu_w_bf, glu_b)
    fin = ends[:, :BATCH].reshape(2, BATCH, S5_GROUPS, S5_SW).transpose(1, 0, 2, 3)
    return out, fin[..., :S5_STATE], fin[..., S5_STATE:]


def _softmax_rows(parts):
    m = parts[0].max(axis=-1, keepdims=True)
    for s in parts[1:]:
        m = jnp.maximum(m, s.max(axis=-1, keepdims=True))
    ps = [jnp.exp(s - m) for s in parts]
    tot = ps[0].sum(axis=-1, keepdims=True)
    for p in ps[1:]:
        tot = tot + p.sum(axis=-1, keepdims=True)
    inv = 1.0 / tot
    return [(p * inv).astype(BF16) for p in ps]


def _ctxattn_kernel(q_ref, k_ref, v_ref, o_ref, nk_ref, nv_ref):
    for h in range(NA_HEADS):
        sl = slice(h * NA_HEAD_DIM, (h + 1) * NA_HEAD_DIM)
        k = k_ref[:, sl]
        v = v_ref[:, sl]
        nk_ref[h] = k
        nv_ref[h] = v
        q = (q_ref[:, sl] * QK_SCALE).astype(BF16)
        (p,) = _softmax_rows([_dot_nt(q, k.astype(BF16))])
        o_ref[:, sl] = _dot(p, v.astype(BF16)).astype(BF16)


def _ctx_attention(proj):
    col = lambda c: pl.BlockSpec((SEQ, NA_WIDTH), lambda b: (b, c))
    cache = jax.ShapeDtypeStruct((BATCH, 1, NA_HEADS, SEQ, NA_HEAD_DIM), F32)
    cache_spec = pl.BlockSpec((None, None, NA_HEADS, SEQ, NA_HEAD_DIM), lambda b: (b, 0, 0, 0, 0))
    return pl.pallas_call(
        _ctxattn_kernel,
        out_shape=(jax.ShapeDtypeStruct((N_PROMPT, NA_WIDTH), BF16), cache, cache),
        grid=(BATCH,),
        in_specs=[col(1), col(2), col(3)],
        out_specs=(pl.BlockSpec((SEQ, NA_WIDTH), lambda b: (b, 0)), cache_spec, cache_spec),
        compiler_params=_params(("parallel",)),
        name="ctx_attention",
    )(proj, proj, proj)


def _na_geometry(r0):
    ks = min(max(r0 - NA_WIN_R // 2, 0), GRID_ROWS - NA_KROWS)
    tiles = []
    for a in range(NA_QROWS):
        rq = r0 + a
        rs = min(max(rq - NA_WIN_R // 2, 0), GRID_ROWS - NA_WIN_R)
        row = []
        for rl in range(NA_KROWS):
            rk = ks + rl
            row.append(rk - rq + NA_WIN_R - 1 if rs <= rk < rs + NA_WIN_R else None)
        tiles.append(row)
    return ks, tiles


NA_VARIANT_ROW0 = (0, NA_QROWS, GRID_ROWS - NA_QROWS)


def _nabias_kernel(rpb_ref, o_ref, t_ref):
    h = pl.program_id(0)
    shape = (GRID_W, 2 * GRID_W)
    cq = lax.broadcasted_iota(jnp.int32, shape, 0)
    lane = lax.broadcasted_iota(jnp.int32, shape, 1)
    ck = lane % GRID_W
    dc = jnp.clip(ck - cq + (NA_WIN_C - 1), 0, NA_NDC - 1)
    c0 = jnp.clip(cq - NA_WIN_C // 2, 0, GRID_W - NA_WIN_C)
    in_cols = (ck >= c0) & (ck < c0 + NA_WIN_C)
    neg = jnp.full(shape, NEG_INF, F32)
    for dr in range(NA_NDR):
        t = neg
        for v in range(NA_NDC):
            t = jnp.where(dc == v, rpb_ref[h, dr * NA_NDC + v], t)
        t_ref[dr] = jnp.where(in_cols, t, neg)
    left = lane < GRID_W
    for var, r0 in enumerate(NA_VARIANT_ROW0):
        _, tiles = _na_geometry(r0)
        for a in range(NA_QROWS):
            for m in range(NA_KROWS // 2):
                dl, dr_ = tiles[a][2 * m], tiles[a][2 * m + 1]
                tl = neg if dl is None else t_ref[dl]
                tr = neg if dr_ is None else t_ref[dr_]
                o_ref[var, a * GRID_W:(a + 1) * GRID_W, m * 2 * GRID_W:(m + 1) * 2 * GRID_W] = (
                    jnp.where(left, tl, tr))


def _na_bias(rpb):
    nq, nk = NA_QROWS * GRID_W, NA_KROWS * GRID_W
    return pl.pallas_call(
        _nabias_kernel,
        out_shape=jax.ShapeDtypeStruct((NA_HEADS, 3, nq, nk), F32),
        grid=(NA_HEADS,),
        in_specs=[pl.BlockSpec(memory_space=pltpu.SMEM)],
        out_specs=pl.BlockSpec((None, 3, nq, nk), lambda h: (h, 0, 0, 0)),
        scratch_shapes=[pltpu.VMEM((NA_NDR, GRID_W, 2 * GRID_W), F32)],
        compiler_params=_params(("parallel",)),
        name="na_bias",
    )(rpb.reshape(NA_HEADS, NA_NDR * NA_NDC))


def _naattn_kernel(q_ref, k_ref, v_ref, kc_ref, vc_ref, bias_ref, o_ref):
    kb = k_ref[...].astype(BF16)
    vb = v_ref[...].astype(BF16)
    kc = kc_ref[...].astype(BF16)
    vc = vc_ref[...].astype(BF16)
    nq = NA_QROWS * GRID_W
    for blk in range(GRID_ROWS // NA_QROWS):
        r0 = blk * NA_QROWS
        var = 0 if blk == 0 else (2 if r0 == NA_VARIANT_ROW0[2] else 1)
        ks, _ = _na_geometry(r0)
        keys = slice(ks * GRID_W, (ks + NA_KROWS) * GRID_W)
        q = (q_ref[r0 * GRID_W:r0 * GRID_W + nq, :] * QK_SCALE).astype(BF16)
        s_loc = _dot_nt(q, kb[keys]) + bias_ref[var]
        s_ctx = _dot_nt(q, kc)
        p_loc, p_ctx = _softmax_rows([s_loc, s_ctx])
        o = _dot(p_loc, vb[keys]) + _dot(p_ctx, vc)
        o_ref[r0 * GRID_W:r0 * GRID_W + nq, :] = o.astype(BF16)


def _na_attention(proj, cache_k, cache_v, bias):
    first = N_PROMPT // DEC_SEQ
    col = lambda c: pl.BlockSpec((DEC_SEQ, NA_HEAD_DIM), lambda b, h: (first + b, c * NA_HEADS + h))
    cache_spec = pl.BlockSpec((None, None, None, PAST_LEN, NA_HEAD_DIM), lambda b, h: (b, 0, h, 0, 0))
    return pl.pallas_call(
        _naattn_kernel,
        out_shape=jax.ShapeDtypeStruct((DEC_BATCH * DEC_SEQ, NA_WIDTH), BF16),
        grid=(DEC_BATCH, NA_HEADS),
        in_specs=[col(1), col(2), col(3), cache_spec, cache_spec,
                  pl.BlockSpec((None, 3, NA_QROWS * GRID_W, NA_KROWS * GRID_W), lambda b, h: (h, 0, 0, 0))],
        out_specs=pl.BlockSpec((DEC_SEQ, NA_HEAD_DIM), lambda b, h: (b, h)),
        compiler_params=_params(("parallel", "parallel")),
        name="na_attention",
    )(proj, proj, proj, cache_k, cache_v, bias)


def _outproj_kernel(a_ref, b_ref, w_ref, x_ref, gate_ref, g_ref, beta_ref, o_ref):
    half = a_ref.shape[1]
    y = _dot(a_ref[...], w_ref[0:half, :]) + _dot(b_ref[...], w_ref[half:2 * half, :])
    o_ref[...] = _post_norm(x_ref[...], y, gate_ref[...], g_ref[...], beta_ref[...])


def _out_proj(y_s5, y_att, w_bf, x, mod, ln_g, ln_b, layer):
    tm = 512
    half = y_s5.shape[1]
    return pl.pallas_call(
        _outproj_kernel,
        out_shape=jax.ShapeDtypeStruct((N_TOK, D_MODEL), F32),
        grid=(N_TOK // tm,),
        in_specs=[pl.BlockSpec((tm, half), lambda i: (i, 0)),
                  pl.BlockSpec((tm, half), lambda i: (i, 0)),
                  pl.BlockSpec((D_MODEL, D_MODEL), lambda i: (0, 0)),
                  pl.BlockSpec((tm, D_MODEL), lambda i: (i, 0)),
                  _mod_spec(layer, 2, tm), _row_spec(layer, 0), _row_spec(layer, 0)],
        out_specs=pl.BlockSpec((tm, D_MODEL), lambda i: (i, 0)),
        compiler_params=_params(("parallel",)),
        name="out_proj_norm",
    )(y_s5, y_att, w_bf, x, mod, ln_g, ln_b)


def _ffn_kernel(x_ref, sh_ref, sc_ref, gate_ref, g_ref, beta_ref, wg_ref, wu_ref, wd_ref, o_ref, h_ref, acc_ref):
    f = pl.program_id(1)

    @pl.when(f == 0)
    def _():
        h_ref[...] = (x_ref[...] * (1.0 + sc_ref[...]) + sh_ref[...]).astype(BF16)
        acc_ref[...] = jnp.zeros_like(acc_ref)

    h = h_ref[...]
    a = _dot(h, wg_ref[...])
    b = _dot(h, wu_ref[...])
    acc_ref[...] += _dot((a * _sigmoid(a) * b).astype(BF16), wd_ref[...])

    @pl.when(f == pl.num_programs(1) - 1)
    def _():
        o_ref[...] = _post_norm(x_ref[...], acc_ref[...], gate_ref[...], g_ref[...], beta_ref[...])


def _ffn(x, mod, ln_g, ln_b, wg_bf, wu_bf, wd_bf, layer):
    tm, tf = 512, 512
    return pl.pallas_call(
        _ffn_kernel,
        out_shape=jax.ShapeDtypeStruct((N_TOK, D_MODEL), F32),
        grid=(N_TOK // tm, FFN_DIM // tf),
        in_specs=[pl.BlockSpec((tm, D_MODEL), lambda i, f: (i, 0)),
                  _mod_spec(layer, 3, tm), _mod_spec(layer, 4, tm), _mod_spec(layer, 5, tm),
                  _row_spec(layer, 1), _row_spec(layer, 1),
                  pl.BlockSpec((D_MODEL, tf), lambda i, f: (0, f)),
                  pl.BlockSpec((D_MODEL, tf), lambda i, f: (0, f)),
                  pl.BlockSpec((tf, D_MODEL), lambda i, f: (f, 0))],
        out_specs=pl.BlockSpec((tm, D_MODEL), lambda i, f: (i, 0)),
        scratch_shapes=[pltpu.VMEM((tm, D_MODEL), BF16), pltpu.VMEM((tm, D_MODEL), F32)],
        compiler_params=_params(("parallel", "arbitrary")),
        name="ffn_norm",
    )(x, mod, mod, mod, ln_g, ln_b, wg_bf, wu_bf, wd_bf)


def _pool_kernel(x_ref, prev_ref, next_ref, sh_ref, sc_ref, gate_ref, g_ref, beta_ref, w_ref, ps_ref,
                 o_ref, ext_ref, y_ref):
    q = pl.program_id(0)
    latent = q >= BATCH
    seg = (q - BATCH) % S5_SEGS
    has_prev = latent & (seg > 0)
    has_next = latent & (seg < S5_SEGS - 1)
    seq_len = jnp.where(latent, DEC_SEQ, SEQ)
    t = jnp.where(latent, seg * SEQ, 0) + lax.broadcasted_iota(jnp.int32, (SEQ, 1), 0)

    scale = 1.0 + sc_ref[...]
    shift = sh_ref[...]
    x = x_ref[...]
    halo = POOL_HALO
    ext_ref[0:halo, :] = jnp.where(has_prev, prev_ref[...] * scale + shift, 0.0)
    ext_ref[halo:halo + SEQ, :] = x * scale + shift
    ext_ref[halo + SEQ:2 * halo + SEQ, :] = jnp.where(has_next, next_ref[...] * scale + shift, 0.0)

    for g, w in enumerate(POOL_SIZES):
        cols = slice(g * POOL_GROUP_DIM, (g + 1) * POOL_GROUP_DIM)
        total = None
        for k in range(-(w // 2), w - w // 2):
            part = ext_ref[halo + k:halo + k + SEQ, cols]
            total = part if total is None else total + part
        count = jnp.minimum(t + (w - w // 2), seq_len) - jnp.maximum(t - w // 2, 0)
        pooled = total / count.astype(F32) - ext_ref[halo:halo + SEQ, cols]
        y_ref[:, cols] = _dot(pooled.astype(BF16), w_ref[g])
    y = y_ref[...] * ps_ref[...]
    o_ref[...] = _post_norm(x, y, gate_ref[...], g_ref[...], beta_ref[...])


def _pool(x, mod, ln_g, ln_b, w_bf, pool_scale, layer):
    tm = SEQ
    nhb = N_TOK // POOL_HALO
    per = tm // POOL_HALO
    return pl.pallas_call(
        _pool_kernel,
        out_shape=jax.ShapeDtypeStruct((N_TOK, D_MODEL), F32),
        grid=(N_TOK // tm,),
        in_specs=[pl.BlockSpec((tm, D_MODEL), lambda i: (i, 0)),
                  pl.BlockSpec((POOL_HALO, D_MODEL), lambda i: (jnp.maximum(i * per - 1, 0), 0)),
                  pl.BlockSpec((POOL_HALO, D_MODEL), lambda i: (jnp.minimum((i + 1) * per, nhb - 1), 0)),
                  _mod_spec(layer, 0, tm), _mod_spec(layer, 1, tm), _mod_spec(layer, 2, tm),
                  _row_spec(layer, 0), _row_spec(layer, 0),
                  pl.BlockSpec((len(POOL_SIZES), POOL_GROUP_DIM, POOL_GROUP_DIM), lambda i: (0, 0, 0)),
                  pl.BlockSpec((1, D_MODEL), lambda i: (0, 0))],
        out_specs=pl.BlockSpec((tm, D_MODEL), lambda i: (i, 0)),
        scratch_shapes=[pltpu.VMEM((tm + 2 * POOL_HALO, D_MODEL), F32), pltpu.VMEM((tm, D_MODEL), F32)],
        compiler_params=_params(("parallel",)),
        name="pool_norm",
    )(x, x, x, mod, mod, mod, ln_g, ln_b, w_bf, pool_scale.reshape(1, D_MODEL))


ROUTER_LANES = 128


def _split_bf16(a):
    hi = a.astype(BF16)
    return hi, (a - hi.astype(F32)).astype(BF16)


def _router_kernel(x_ref, sh_ref, sc_ref, w_ref, b_ref, h_ref, info_ref):
    h = x_ref[...] * (1.0 + sc_ref[...]) + sh_ref[...]
    h_ref[...] = h
    hh, hl = _split_bf16(h)
    wh, wl = _split_bf16(w_ref[...])
    logits = _dot(hh, wh) + _dot(hl, wh) + _dot(hh, wl) + b_ref[...]
    lane = lax.broadcasted_iota(jnp.int32, logits.shape, 1)
    logits = jnp.where(lane < N_EXPERTS, logits, -jnp.inf)
    m1 = logits.max(axis=-1, keepdims=True)
    i1 = jnp.where(logits == m1, lane, ROUTER_LANES).min(axis=-1, keepdims=True)
    rest = jnp.where(lane == i1, -jnp.inf, logits)
    m2 = rest.max(axis=-1, keepdims=True)
    i2 = jnp.where(rest == m2, lane, ROUTER_LANES).min(axis=-1, keepdims=True)
    e = jnp.exp(m2 - m1)
    g1 = 1.0 / (1.0 + e)
    g2 = e / (1.0 + e)
    info = jnp.where(lane == 0, i1.astype(F32), jnp.where(lane == 1, i2.astype(F32),
                     jnp.where(lane == 2, g1, jnp.where(lane == 3, g2, 0.0))))
    info_ref[...] = info


def _router(x, mod, router_w, router_b, layer):
    tm = 512
    w = jnp.zeros((D_MODEL, ROUTER_LANES), F32).at[:, :N_EXPERTS].set(router_w)
    b = jnp.zeros((1, ROUTER_LANES), F32).at[0, :N_EXPERTS].set(router_b)
    return pl.pallas_call(
        _router_kernel,
        out_shape=(jax.ShapeDtypeStruct((N_TOK, D_MODEL), F32),
                   jax.ShapeDtypeStruct((N_TOK, ROUTER_LANES), F32)),
        grid=(N_TOK // tm,),
        in_specs=[pl.BlockSpec((tm, D_MODEL), lambda i: (i, 0)),
                  _mod_spec(layer, 3, tm), _mod_spec(layer, 4, tm),
                  pl.BlockSpec((D_MODEL, ROUTER_LANES), lambda i: (0, 0)),
                  pl.BlockSpec((1, ROUTER_LANES), lambda i: (0, 0))],
        out_specs=(pl.BlockSpec((tm, D_MODEL), lambda i: (i, 0)),
                   pl.BlockSpec((tm, ROUTER_LANES), lambda i: (i, 0))),
        compiler_params=_params(("parallel",)),
        name="moe_router",
    )(x, mod, mod, w, b)


def _routing_tables(info):
    experts = info[:, :2].astype(jnp.int32).reshape(-1)
    onehot = (experts[:, None] == jnp.arange(N_EXPERTS)[None, :]).astype(jnp.int32)
    counts = onehot.sum(axis=0)
    rank = ((jnp.cumsum(onehot, axis=0) - onehot) * onehot).sum(axis=1)
    padded = (counts + MOE_TILE - 1) // MOE_TILE * MOE_TILE
    ends = jnp.cumsum(padded)
    pos = (ends - padded)[experts] + rank
    src = jnp.zeros((MOE_ROWS,), jnp.int32).at[pos].set(jnp.arange(2 * N_TOK, dtype=jnp.int32) // 2)
    ntiles = (ends[-1] // MOE_TILE).astype(jnp.int32)
    tile_start = jnp.minimum(jnp.arange(MOE_NTILES, dtype=jnp.int32), ntiles - 1) * MOE_TILE
    tile_expert = jnp.sum(tile_start[:, None] >= ends[None, :], axis=1).astype(jnp.int32)
    pos = pos.reshape(N_TOK, 2).astype(jnp.int32)
    return src, tile_expert, ntiles.reshape(1), pos[:, 0], pos[:, 1]


def _row_copy(src_hbm, row, dst, r, sem):
    return pltpu.make_async_copy(src_hbm.at[pl.ds(row, 1), :], dst.at[pl.ds(r, 1), :], sem)


def _dispatch_kernel(src_ref, nt_ref, h_hbm, o_ref, buf, sem):
    m = pl.program_id(0)
    used = m < nt_ref[0]

    @pl.when(used)
    def _():
        def issue(r, c):
            _row_copy(h_hbm, src_ref[m * MOE_TILE + r], buf, r, sem).start()
            return c

        lax.fori_loop(0, MOE_TILE, issue, 0, unroll=8)

        def drain(r, c):
            _row_copy(h_hbm, 0, buf, r, sem).wait()
            return c

        lax.fori_loop(0, MOE_TILE, drain, 0, unroll=8)
        o_ref[...] = buf[...].astype(BF16)

    @pl.when(jnp.logical_not(used))
    def _():
        o_ref[...] = jnp.zeros_like(o_ref)


def _dispatch(h, src, ntiles):
    return pl.pallas_call(
        _dispatch_kernel,
        out_shape=jax.ShapeDtypeStruct((MOE_ROWS, D_MODEL), BF16),
        grid_spec=pltpu.PrefetchScalarGridSpec(
            num_scalar_prefetch=2,
            grid=(MOE_NTILES,),
            in_specs=[pl.BlockSpec(memory_space=pl.ANY)],
            out_specs=pl.BlockSpec((MOE_TILE, D_MODEL), lambda m, *_: (m, 0)),
            scratch_shapes=[pltpu.VMEM((MOE_TILE, D_MODEL), F32), pltpu.SemaphoreType.DMA(())],
        ),
        compiler_params=_params(("arbitrary",)),
        name="moe_dispatch",
    )(src, ntiles, h)


def _gmm_up_kernel(te_ref, nt_ref, x_ref, wg_ref, wu_ref, o_ref, wgb_ref, wub_ref):
    m = pl.program_id(1)
    used = m < nt_ref[0]
    fresh = jnp.logical_or(m == 0, te_ref[m] != te_ref[jnp.maximum(m - 1, 0)])

    @pl.when(jnp.logical_and(used, fresh))
    def _():
        wgb_ref[...] = wg_ref[...].astype(BF16)
        wub_ref[...] = wu_ref[...].astype(BF16)

    @pl.when(used)
    def _():
        x = x_ref[...]
        a = _dot(x, wgb_ref[...])
        b = _dot(x, wub_ref[...])
        o_ref[...] = (a * _sigmoid(a) * b).astype(BF16)

    @pl.when(jnp.logical_not(used))
    def _():
        o_ref[...] = jnp.zeros_like(o_ref)


def _gmm_up(xs, w_gate, w_up, tile_expert, ntiles):
    tf = 1024
    row = lambda f, m, te, nt: (jnp.minimum(m, nt[0] - 1), 0)
    wsp = pl.BlockSpec((None, D_MODEL, tf), lambda f, m, te, nt: (te[m], 0, f))
    return pl.pallas_call(
        _gmm_up_kernel,
        out_shape=jax.ShapeDtypeStruct((MOE_ROWS, EXPERT_DIM), BF16),
        grid_spec=pltpu.PrefetchScalarGridSpec(
            num_scalar_prefetch=2,
            grid=(EXPERT_DIM // tf, MOE_NTILES),
            in_specs=[pl.BlockSpec((MOE_TILE, D_MODEL), row), wsp, wsp],
            out_specs=pl.BlockSpec((MOE_TILE, tf), lambda f, m, te, nt: (m, f)),
            scratch_shapes=[pltpu.VMEM((D_MODEL, tf), BF16), pltpu.VMEM((D_MODEL, tf), BF16)],
        ),
        compiler_params=_params(("parallel", "arbitrary")),
        name="moe_gate_up",
    )(tile_expert, ntiles, xs, w_gate, w_up)


def _gmm_down_kernel(te_ref, nt_ref, x_ref, w_ref, o_ref, wbf_ref):
    m = pl.program_id(1)
    used = m < nt_ref[0]
    fresh = jnp.logical_or(m == 0, te_ref[m] != te_ref[jnp.maximum(m - 1, 0)])

    @pl.when(jnp.logical_and(used, fresh))
    def _():
        wbf_ref[...] = w_ref[...].astype(BF16)

    @pl.when(used)
    def _():
        o_ref[...] = _dot(x_ref[...], wbf_ref[...])

    @pl.when(jnp.logical_not(used))
    def _():
        o_ref[...] = jnp.zeros_like(o_ref)


def _gmm_down(g, w_down, tile_expert, ntiles):
    tn = 512
    return pl.pallas_call(
        _gmm_down_kernel,
        out_shape=jax.ShapeDtypeStruct((MOE_ROWS, D_MODEL), F32),
        grid_spec=pltpu.PrefetchScalarGridSpec(
            num_scalar_prefetch=2,
            grid=(D_MODEL // tn, MOE_NTILES),
            in_specs=[pl.BlockSpec((MOE_TILE, EXPERT_DIM), lambda n, m, te, nt: (jnp.minimum(m, nt[0] - 1), 0)),
                      pl.BlockSpec((None, EXPERT_DIM, tn), lambda n, m, te, nt: (te[m], 0, n))],
            out_specs=pl.BlockSpec((MOE_TILE, tn), lambda n, m, te, nt: (m, n)),
            scratch_shapes=[pltpu.VMEM((EXPERT_DIM, tn), BF16)],
        ),
        compiler_params=_params(("parallel", "arbitrary")),
        name="moe_down",
    )(tile_expert, ntiles, g, w_down)


def _combine_kernel(p1_ref, p2_ref, y_hbm, x_ref, info_ref, gate_ref, g_ref, beta_ref, o_ref, b1, b2, sem):
    i = pl.program_id(0)
    tm = x_ref.shape[0]

    def issue(r, c):
        _row_copy(y_hbm, p1_ref[i * tm + r], b1, r, sem).start()
        _row_copy(y_hbm, p2_ref[i * tm + r], b2, r, sem).start()
        return c

    lax.fori_loop(0, tm, issue, 0, unroll=8)

    def drain(r, c):
        _row_copy(y_hbm, 0, b1, r, sem).wait()
        _row_copy(y_hbm, 0, b2, r, sem).wait()
        return c

    lax.fori_loop(0, tm, drain, 0, unroll=8)
    info = info_ref[...]
    y = info[:, 2:3] * b1[...] + info[:, 3:4] * b2[...]
    o_ref[...] = _post_norm(x_ref[...], y, gate_ref[...], g_ref[...], beta_ref[...])


def _combine(y_sorted, pos1, pos2, x, info, mod, ln_g, ln_b, layer):
    tm = 256
    return pl.pallas_call(
        _combine_kernel,
        out_shape=jax.ShapeDtypeStruct((N_TOK, D_MODEL), F32),
        grid_spec=pltpu.PrefetchScalarGridSpec(
            num_scalar_prefetch=2,
            grid=(N_TOK // tm,),
            in_specs=[pl.BlockSpec(memory_space=pl.ANY),
                      pl.BlockSpec((tm, D_MODEL), lambda i, *_: (i, 0)),
                      pl.BlockSpec((tm, ROUTER_LANES), lambda i, *_: (i, 0)),
                      _mod_spec(layer, 5, tm), _row_spec(layer, 1), _row_spec(layer, 1)],
            out_specs=pl.BlockSpec((tm, D_MODEL), lambda i, *_: (i, 0)),
            scratch_shapes=[pltpu.VMEM((tm, D_MODEL), F32), pltpu.VMEM((tm, D_MODEL), F32),
                            pltpu.SemaphoreType.DMA(())],
        ),
        compiler_params=_params(("arbitrary",)),
        name="moe_combine_norm",
    )(pos1, pos2, y_sorted, x, info, mod, ln_g, ln_b)


def _moe(x, mod, ln_g, ln_b, router_w, router_b, w_gate, w_up, w_down, layer):
    h, info = _router(x, mod, router_w, router_b, layer)
    src, tile_expert, ntiles, pos1, pos2 = _routing_tables(info)
    xs = _dispatch(h, src, ntiles)
    g = _gmm_up(xs, w_gate, w_up, tile_expert, ntiles)
    y = _gmm_down(g, w_down, tile_expert, ntiles)
    return _combine(y, pos1, pos2, x, info, mod, ln_g, ln_b, layer)


def kernel(x_prompt, x_sample, state_s5_re, state_s5_im, cache_k, cache_v, c, c_ctx, ada_w, ada_b, ln_g, ln_b, ab_w_in, ab_w_out, s5_lambda_re, s5_lambda_im, s5_log_dt, s5_b_re, s5_b_im, s5_c_re, s5_c_im, s5_d, s5_glu_w, s5_glu_b, na_rpb, ffn_w_gate, ffn_w_up, ffn_w_down, pool_w, pool_scale, moe_router_w, moe_router_b, moe_w_gate, moe_w_up, moe_w_down):
    x = jnp.concatenate([x_prompt.reshape(N_PROMPT, D_MODEL), x_sample.reshape(-1, D_MODEL)], axis=0)

    cond8 = jnp.zeros((8, D_MODEL), F32).at[0].set(c_ctx).at[1:N_COND].set(c)
    mod = _ada(cond8, ada_w, ada_b)
    mod = mod[:, :N_COND].reshape(DEPTH, N_COND, N_MOD, 1, D_MODEL).transpose(0, 2, 1, 3, 4)
    ln_g4 = ln_g.reshape(DEPTH, 2, 1, D_MODEL)
    ln_b4 = ln_b.reshape(DEPTH, 2, 1, D_MODEL)

    proj = _inproj(x, mod, _cast_bf16(ab_w_in[0]), 0)
    prep = _s5_prep(s5_lambda_re[0], s5_lambda_im[0], s5_log_dt[0], s5_b_re[0], s5_b_im[0],
                    s5_c_re[0], s5_c_im[0])
    y_s5, fin_re, fin_im = _s5_mixer(proj, state_s5_re[:, 0], state_s5_im[:, 0], prep, s5_d[0],
                                     _cast_bf16(s5_glu_w[0]), s5_glu_b[0])
    y_ctx, new_k, new_v = _ctx_attention(proj)
    y_na = _na_attention(proj, cache_k, cache_v, _na_bias(na_rpb[0]))
    y_att = jnp.concatenate([y_ctx, y_na], axis=0)
    x = _out_proj(y_s5, y_att, _cast_bf16(ab_w_out[0]), x, mod, ln_g4, ln_b4, 0)
    x = _ffn(x, mod, ln_g4, ln_b4, _cast_bf16(ffn_w_gate[0]), _cast_bf16(ffn_w_up[0]),
             _cast_bf16(ffn_w_down[0]), 0)

    x = _pool(x, mod, ln_g4, ln_b4, _cast_bf16(pool_w[0]), pool_scale[0], 1)
    x = _moe(x, mod, ln_g4, ln_b4, moe_router_w[0], moe_router_b[0], moe_w_gate[0], moe_w_up[0],
             moe_w_down[0], 1)

    y_prompt = x[:N_PROMPT].reshape(BATCH, SEQ, D_MODEL)
    y_sample = x[N_PROMPT:].reshape(DEC_BATCH, DEC_SEQ, D_MODEL)
    return (y_prompt, y_sample, fin_re[:, None], fin_im[:, None], new_k, new_v)
```

```python
import functools
import math

import jax
import jax.numpy as jnp
from jax import lax
from jax.experimental import pallas as pl
from jax.experimental.pallas import tpu as pltpu

F32 = jnp.float32
BF16 = jnp.bfloat16

D_MODEL = 2048
BATCH = 16
SEQ = 256
DEPTH = 2
DEC_BATCH = 2
DEC_SEQ = 2048
N_MOD = 6
N_PROMPT = BATCH * SEQ
N_TOK = N_PROMPT + DEC_BATCH * DEC_SEQ
N_COND = 1 + DEC_BATCH

S5_WIDTH = 1024
S5_GROUP = 16
S5_GROUPS = 64
S5_STATE = 64
S5_CHUNK = 16
S5_SEQ = 256
S5_NSEQ = N_TOK // S5_SEQ
S5_NCHUNK = S5_SEQ // S5_CHUNK
S5_ROWS = S5_NSEQ * S5_NCHUNK
S5_SEGS = DEC_SEQ // S5_SEQ
S5_CW = S5_CHUNK * S5_GROUP
S5_SW = 2 * S5_STATE

NA_WIDTH = 1024
NA_HEADS = 8
NA_HEAD_DIM = 128
NA_WIN_R = 8
NA_WIN_C = 16
GRID_W = 64
GRID_ROWS = DEC_SEQ // GRID_W
NA_QROWS = 4
NA_KROWS = 12
NA_NDR = 2 * NA_WIN_R - 1
NA_NDC = 2 * NA_WIN_C - 1
PAST_LEN = 256

POOL_SIZES = (2, 4, 8, 16)
POOL_GROUP_DIM = 512
POOL_HALO = 16

FFN_DIM = 5632
N_EXPERTS = 8
EXPERT_DIM = 7168
MOE_TILE = 256
MOE_PAIR = 2
MOE_SPAN = MOE_PAIR * MOE_TILE
MOE_ROWS = 2 * N_TOK + N_EXPERTS * MOE_SPAN
MOE_NTILES = MOE_ROWS // MOE_TILE

LN_EPS = 1e-5
DEEPNORM_ALPHA = (2.0 * DEPTH) ** 0.25
NEG_INF = -1e30
QK_SCALE = NA_HEAD_DIM ** -0.5

VMEM_LIMIT = 52 * 1024 * 1024


def _params(sem, vmem=VMEM_LIMIT):
    return pltpu.CompilerParams(dimension_semantics=sem, vmem_limit_bytes=vmem)


def _sigmoid(x):
    return 1.0 / (1.0 + jnp.exp(-x))


def _dot(a, b):
    return jnp.dot(a, b, preferred_element_type=F32)


def _dot_nt(a, b):
    return lax.dot_general(a, b, (((1,), (1,)), ((), ())), preferred_element_type=F32)


def _dot_exact(a, b):
    return jnp.dot(a, b, preferred_element_type=F32, precision=lax.Precision.HIGHEST)


def _cond_of_row(row0):
    return jnp.where(row0 < N_PROMPT, 0, 1 + (row0 - N_PROMPT) // DEC_SEQ)


def _mod_spec(layer, k, tm):
    return pl.BlockSpec((None, None, None, 1, D_MODEL),
                        lambda i, *_: (layer, k, _cond_of_row(i * tm), 0, 0))


def _row_spec(layer_idx, k=None):
    if k is None:
        return pl.BlockSpec((None, 1, D_MODEL), lambda *_: (layer_idx, 0, 0))
    return pl.BlockSpec((None, None, 1, D_MODEL), lambda *_: (layer_idx, k, 0, 0))


def _post_norm(x, y, gate, g, b):
    v = DEEPNORM_ALPHA * x + gate * y
    mu = jnp.mean(v, axis=-1, keepdims=True)
    c = v - mu
    var = jnp.mean(c * c, axis=-1, keepdims=True)
    return c * lax.rsqrt(var + LN_EPS) * g + b


def _ada_kernel(c_ref, w_ref, b_ref, o_ref):
    c = c_ref[...]
    s = (c * _sigmoid(c)).astype(BF16)
    o_ref[...] = _dot(s, w_ref[...].astype(BF16)) + b_ref[...]


def _ada(cond8, ada_w, ada_b):
    tn = 1024
    n_out = N_MOD * D_MODEL
    return pl.pallas_call(
        _ada_kernel,
        out_shape=jax.ShapeDtypeStruct((DEPTH, 8, n_out), F32),
        grid=(DEPTH, n_out // tn),
        in_specs=[pl.BlockSpec((8, D_MODEL), lambda l, n: (0, 0)),
                  pl.BlockSpec((None, D_MODEL, tn), lambda l, n: (l, 0, n)),
                  pl.BlockSpec((None, 1, tn), lambda l, n: (l, 0, n))],
        out_specs=pl.BlockSpec((None, 8, tn), lambda l, n: (l, 0, n)),
        compiler_params=_params(("parallel", "parallel")),
        name="ada_mod",
    )(cond8, ada_w, ada_b.reshape(DEPTH, 1, n_out))


def _cast_kernel(w_ref, o_ref):
    o_ref[...] = w_ref[...].astype(BF16)


def _cast_bf16(w):
    shape = w.shape
    cols = shape[-1]
    rows = w.size // cols
    rb = 8
    while rb * 2 * cols * 4 <= 4 * 1024 * 1024 and rows % (rb * 2) == 0:
        rb *= 2
    out = pl.pallas_call(
        _cast_kernel,
        out_shape=jax.ShapeDtypeStruct((rows, cols), BF16),
        grid=(rows // rb,),
        in_specs=[pl.BlockSpec((rb, cols), lambda i: (i, 0))],
        out_specs=pl.BlockSpec((rb, cols), lambda i: (i, 0)),
        compiler_params=_params(("parallel",)),
        name="cast_bf16",
    )(w.reshape(rows, cols))
    return out.reshape(shape)


def _inproj_kernel(x_ref, sh_ref, sc_ref, w_ref, o_ref, h_ref):
    @pl.when(pl.program_id(1) == 0)
    def _():
        h_ref[...] = (x_ref[...] * (1.0 + sc_ref[...]) + sh_ref[...]).astype(BF16)

    o_ref[...] = _dot(h_ref[...], w_ref[...])


def _inproj(x, mod, w_bf, layer):
    tm, tn = 512, 1024
    n_out = w_bf.shape[1]
    return pl.pallas_call(
        _inproj_kernel,
        out_shape=jax.ShapeDtypeStruct((N_TOK, n_out), F32),
        grid=(N_TOK // tm, n_out // tn),
        in_specs=[pl.BlockSpec((tm, D_MODEL), lambda i, n: (i, 0)),
                  _mod_spec(layer, 0, tm), _mod_spec(layer, 1, tm),
                  pl.BlockSpec((D_MODEL, tn), lambda i, n: (0, n))],
        out_specs=pl.BlockSpec((tm, tn), lambda i, n: (i, n)),
        scratch_shapes=[pltpu.VMEM((tm, D_MODEL), BF16)],
        compiler_params=_params(("parallel", "arbitrary")),
        name="in_proj",
    )(x, mod, mod, w_bf)


def _cpow_table(re, im, n):
    out = [(jnp.ones_like(re), jnp.zeros_like(im))]
    for _ in range(n):
        pr, pi = out[-1]
        out.append((pr * re - pi * im, pr * im + pi * re))
    return out


def _lam_bar(lam):
    re = jnp.minimum(lam[0], -1e-4)
    im = lam[1]
    dt = jnp.exp(lam[2])
    mag = jnp.exp(re * dt)
    return re, im, mag * jnp.cos(im * dt), mag * jnp.sin(im * dt)


def _s5prep_kernel(lamc_ref, lamr_ref, ct_ref, bt_ref, b12_ref, toep_ref, s_ref, in_ref, co_ref):
    fwd = pl.program_id(0) == 0
    P, G, T = S5_STATE, S5_GROUP, S5_CHUNK

    lc = lamc_ref[...]
    _, _, lbr_c, lbi_c = _lam_bar((lc[:, 0:1], lc[:, 1:2], lc[:, 2:3]))
    pw_c = _cpow_table(lbr_c, lbi_c, T)

    def pick_cols(e_idx, shape):
        pr = jnp.zeros(shape, F32)
        pi = jnp.zeros(shape, F32)
        for e in range(T + 1):
            hit = e_idx == e
            pr = jnp.where(hit, pw_c[e][0], pr)
            pi = jnp.where(hit, pw_c[e][1], pi)
        return pr, pi

    wide = 2 * T * G
    lane = lax.broadcasted_iota(jnp.int32, (G, wide), 1)
    row = lax.broadcasted_iota(jnp.int32, (G, wide), 0)
    expand = (lane % G == row).astype(F32)
    ct_re = _dot_exact(ct_ref[0], expand)
    ct_im = _dot_exact(ct_ref[1], expand)

    blk = lax.broadcasted_iota(jnp.int32, (P, wide), 1) // G
    e_idx = jnp.where(fwd, blk - T, T - blk)
    pr, pi = pick_cols(e_idx, (P, wide))
    r_re = pr * ct_re - pi * ct_im
    r_im = pr * ct_im + pi * ct_re

    lr = lamr_ref[...]
    re_r, im_r, lbr_r, lbi_r = _lam_bar((lr[0:1, :], lr[1:2, :], lr[2:3, :]))
    den = re_r * re_r + im_r * im_r
    cf_re = ((lbr_r - 1.0) * re_r + lbi_r * im_r) / den
    cf_im = (lbi_r * re_r - (lbr_r - 1.0) * im_r) / den
    bb_re = cf_re[:, :P] * bt_ref[0] - cf_im[:, :P] * bt_ref[1]
    bb_im = cf_re[:, :P] * bt_ref[1] + cf_im[:, :P] * bt_ref[0]
    kext = _dot_exact(bb_re, r_re) - _dot_exact(bb_im, r_im)
    for i in range(T):
        off = (T - i) * G
        toep_ref[i * G:(i + 1) * G, :] = kext[:, off:off + T * G].astype(BF16)

    blk2 = lax.broadcasted_iota(jnp.int32, (P, T * G), 1) // G
    e_in = jnp.where(fwd, blk2 + 1, T - blk2)
    qr, qi = pick_cols(e_in, (P, T * G))
    c_re = ct_re[:, :T * G]
    c_im = ct_im[:, :T * G]
    in_ref[0:P, :] = (qr * c_re - qi * c_im).astype(BF16)
    in_ref[P:2 * P, :] = (-(qr * c_im + qi * c_re)).astype(BF16)

    pw_r = _cpow_table(lbr_r, lbi_r, T)
    b1 = b12_ref[0]
    b2 = b12_ref[1]
    for i in range(T):
        wr = jnp.where(fwd, pw_r[T - 1 - i][0], pw_r[i][0])
        wi = jnp.where(fwd, pw_r[T - 1 - i][1], pw_r[i][1])
        sr = wr * cf_re - wi * cf_im
        si = wr * cf_im + wi * cf_re
        s_ref[i * G:(i + 1) * G, :] = (sr * b1 + si * b2).astype(BF16)

    sgn = jnp.where(lax.broadcasted_iota(jnp.int32, (1, S5_SW), 1) < P, -1.0, 1.0)
    r16, i16 = pw_r[T]
    r256, i256 = r16, i16
    for _ in range(4):
        r256, i256 = r256 * r256 - i256 * i256, 2.0 * r256 * i256
    for k, rowv in enumerate((r16, sgn * i16, r256, sgn * i256)):
        co_ref[k:k + 1, :] = rowv
    co_ref[4:8, :] = jnp.zeros((4, S5_SW), F32)


def _s5_prep(lam_re, lam_im, log_dt, b_re, b_im, c_re, c_im):
    P, G = S5_STATE, S5_GROUP
    ldt = jnp.broadcast_to(log_dt[..., None], lam_re.shape)
    lamc = jnp.stack([lam_re, lam_im, ldt], axis=-1)
    dup = lambda a: jnp.concatenate([a, a], axis=-1)
    lamr = jnp.stack([dup(lam_re), dup(lam_im), dup(ldt)], axis=-2)
    ct = jnp.stack([jnp.swapaxes(c_re, -1, -2), jnp.swapaxes(c_im, -1, -2)], axis=2)
    bt_re = jnp.swapaxes(b_re, -1, -2)
    bt_im = jnp.swapaxes(b_im, -1, -2)
    bt = jnp.stack([bt_re, bt_im], axis=2)
    b12 = jnp.stack([jnp.concatenate([bt_re, bt_im], -1),
                     jnp.concatenate([-bt_im, bt_re], -1)], axis=2)
    blk = lambda *s: pl.BlockSpec((None, None) + s, lambda d, g: (d, g) + (0,) * len(s))
    return pl.pallas_call(
        _s5prep_kernel,
        out_shape=(jax.ShapeDtypeStruct((2, S5_GROUPS, S5_CW, S5_CW), BF16),
                   jax.ShapeDtypeStruct((2, S5_GROUPS, S5_CW, S5_SW), BF16),
                   jax.ShapeDtypeStruct((2, S5_GROUPS, S5_SW, S5_CW), BF16),
                   jax.ShapeDtypeStruct((2, S5_GROUPS, 8, S5_SW), F32)),
        grid=(2, S5_GROUPS),
        in_specs=[blk(P, 3), blk(3, 2 * P), blk(2, P, G), blk(2, G, P), blk(2, G, 2 * P)],
        out_specs=(blk(S5_CW, S5_CW), blk(S5_CW, S5_SW), blk(S5_SW, S5_CW), blk(8, S5_SW)),
        compiler_params=_params(("parallel", "parallel")),
        name="s5_prep",
    )(lamc, lamr, ct, bt, b12)


S5_GPS = 4
S5_LT = 1024


def _s5z_kernel(u_ref, s_ref, z_ref):
    for gg in range(S5_GPS):
        ug = u_ref[:, gg * S5_CW:(gg + 1) * S5_CW]
        for d in range(2):
            z_ref[d, :, gg * S5_SW:(gg + 1) * S5_SW] = _dot(ug, s_ref[d, gg])


def _s5_z(u_chunks, s_mat):
    return pl.pallas_call(
        _s5z_kernel,
        out_shape=jax.ShapeDtypeStruct((2, S5_ROWS, S5_GROUPS * S5_SW), F32),
        grid=(S5_GROUPS // S5_GPS,),
        in_specs=[pl.BlockSpec((S5_ROWS, S5_GPS * S5_CW), lambda g: (0, g)),
                  pl.BlockSpec((2, S5_GPS, S5_CW, S5_SW), lambda g: (0, g, 0, 0))],
        out_specs=pl.BlockSpec((2, S5_ROWS, S5_GPS * S5_SW), lambda g: (0, 0, g)),
        compiler_params=_params(("parallel",)),
        name="s5_chunk_state",
    )(u_chunks, s_mat)


def _s5scan_kernel(z_ref, co_ref, h0_ref, hin_ref, e_ref, g_ref):
    ns = S5_LT // S5_SW
    nq = S5_NSEQ
    lanes = [slice(j * S5_SW, (j + 1) * S5_SW) for j in range(ns)]

    def cmul(a1, a2, h):
        return a1 * h + a2 * pltpu.roll(h, S5_STATE, 1)

    for d in range(2):
        a16 = [(co_ref[d, 0:1, l], co_ref[d, 1:2, l]) for l in lanes]
        a256 = [(co_ref[d, 2:3, l], co_ref[d, 3:4, l]) for l in lanes]
        order = list(range(S5_NCHUNK)) if d == 0 else list(range(S5_NCHUNK - 1, -1, -1))

        def run(h, write):
            for n in order:
                rows = slice(n * nq, (n + 1) * nq)
                if write:
                    for j, l in enumerate(lanes):
                        hin_ref[d, rows, l] = h[j]
                h = [cmul(a16[j][0], a16[j][1], h[j]) + z_ref[d, rows, l] for j, l in enumerate(lanes)]
            return h

        end = run([jnp.zeros((nq, S5_SW), F32)] * ns, False)
        for j, l in enumerate(lanes):
            e_ref[d, :, l] = end[j]

        g_ref[0:BATCH, :] = jnp.zeros((BATCH, S5_LT), F32)
        segs = list(range(S5_SEGS)) if d == 0 else list(range(S5_SEGS - 1, -1, -1))
        for j, l in enumerate(lanes):
            for b in range(DEC_BATCH):
                g = jnp.broadcast_to(h0_ref[d, b:b + 1, l], (8, S5_SW))
                for k, s in enumerate(segs):
                    r = BATCH + b * S5_SEGS + s
                    g_ref[r:r + 1, l] = g[0:1, :]
                    if k + 1 < S5_SEGS:
                        g = cmul(a256[j][0], a256[j][1], g) + jnp.broadcast_to(e_ref[d, r:r + 1, l], (8, S5_SW))

        run([g_ref[:, l] for l in lanes], True)


def _s5_scan(z, coef, h0):
    nl = S5_GROUPS * S5_SW
    return pl.pallas_call(
        _s5scan_kernel,
        out_shape=(jax.ShapeDtypeStruct((2, S5_ROWS, nl), F32),
                   jax.ShapeDtypeStruct((2, S5_NSEQ, nl), F32)),
        grid=(nl // S5_LT,),
        in_specs=[pl.BlockSpec((2, S5_ROWS, S5_LT), lambda t: (0, 0, t)),
                  pl.BlockSpec((2, 8, S5_LT), lambda t: (0, 0, t)),
                  pl.BlockSpec((2, DEC_BATCH, S5_LT), lambda t: (0, 0, t))],
        out_specs=(pl.BlockSpec((2, S5_ROWS, S5_LT), lambda t: (0, 0, t)),
                   pl.BlockSpec((2, S5_NSEQ, S5_LT), lambda t: (0, 0, t))),
        scratch_shapes=[pltpu.VMEM((S5_NSEQ, S5_LT), F32)],
        compiler_params=_params(("parallel",)),
        name="s5_scan",
    )(z, coef, h0)


S5_YG = 8


def _s5y_kernel(u_ref, t_ref, hin_ref, in_ref, o_ref, y_ref, sel_ref):
    kw = S5_YG * S5_CW
    pad = (S5_CHUNK - 1) * S5_GROUP

    @pl.when(pl.program_id(0) == 0)
    def _():
        k = lax.broadcasted_iota(jnp.int32, sel_ref.shape, 0) - pad
        lane = lax.broadcasted_iota(jnp.int32, sel_ref.shape, 1)
        hit = (k >= 0) & ((k % S5_CW) // S5_GROUP == 0) & (lane == (k // S5_CW) * S5_GROUP + k % S5_GROUP)
        sel_ref[...] = jnp.where(hit, 1.0, 0.0).astype(BF16)

    for gg in range(S5_YG):
        ug = u_ref[:, gg * S5_CW:(gg + 1) * S5_CW]
        acc = None
        for d in range(2):
            h = hin_ref[d, :, gg * S5_SW:(gg + 1) * S5_SW].astype(BF16)
            t = _dot(ug, t_ref[d, gg]) + _dot(h, in_ref[d, gg])
            acc = t if acc is None else acc + t
        y_ref[:, gg * S5_CW:(gg + 1) * S5_CW] = acc

    y = y_ref[...]
    hi, lo = _split_bf16(y)
    for j in range(S5_CHUNK):
        off = pad - j * S5_GROUP
        pick = sel_ref[off:off + kw, :]
        yj = _dot(hi, pick) + _dot(lo, pick)
        for n in range(S5_NCHUNK):
            o_ref[pl.ds(n * S5_CHUNK + j, S5_NSEQ, stride=S5_SEQ), :] = yj[n * S5_NSEQ:(n + 1) * S5_NSEQ, :]


def _s5_y(u_chunks, toep, hin, in_mat):
    kw = S5_YG * S5_CW
    lanes = S5_YG * S5_GROUP
    return pl.pallas_call(
        _s5y_kernel,
        out_shape=jax.ShapeDtypeStruct((N_TOK, S5_WIDTH), F32),
        grid=(S5_GROUPS // S5_YG,),
        in_specs=[pl.BlockSpec((S5_ROWS, kw), lambda g: (0, g)),
                  pl.BlockSpec((2, S5_YG, S5_CW, S5_CW), lambda g: (0, g, 0, 0)),
                  pl.BlockSpec((2, S5_ROWS, S5_YG * S5_SW), lambda g: (0, 0, g)),
                  pl.BlockSpec((2, S5_YG, S5_SW, S5_CW), lambda g: (0, g, 0, 0))],
        out_specs=pl.BlockSpec((N_TOK, lanes), lambda g: (0, g)),
        scratch_shapes=[pltpu.VMEM((S5_ROWS, kw), F32), pltpu.VMEM((kw + S5_CW, lanes), BF16)],
        compiler_params=_params(("arbitrary",)),
        name="s5_chunk_out",
    )(u_chunks, toep, hin, in_mat)


def _s5out_kernel(y_ref, u_ref, d_ref, w_ref, b_ref, o_ref):
    y = y_ref[...] + u_ref[...] * d_ref[...]
    y = y * (0.5 * (1.0 + jnp.tanh(math.sqrt(2.0 / math.pi) * (y + 0.044715 * (y * y * y)))))
    z = _dot(y.astype(BF16), w_ref[...]) + b_ref[...]
    o_ref[...] = (y * _sigmoid(z)).astype(BF16)


def _s5_out(y_scan, proj, s5_d, glu_w_bf, glu_b):
    tm = 512
    return pl.pallas_call(
        _s5out_kernel,
        out_shape=jax.ShapeDtypeStruct((N_TOK, S5_WIDTH), BF16),
        grid=(N_TOK // tm,),
        in_specs=[pl.BlockSpec((tm, S5_WIDTH), lambda i: (i, 0)),
                  pl.BlockSpec((tm, S5_WIDTH), lambda i: (i, 0)),
                  pl.BlockSpec((1, S5_WIDTH), lambda i: (0, 0)),
                  pl.BlockSpec((S5_WIDTH, S5_WIDTH), lambda i: (0, 0)),
                  pl.BlockSpec((1, S5_WIDTH), lambda i: (0, 0))],
        out_specs=pl.BlockSpec((tm, S5_WIDTH), lambda i: (i, 0)),
        compiler_params=_params(("parallel",)),
        name="s5_gelu_glu",
    )(y_scan, proj, s5_d.reshape(1, S5_WIDTH), glu_w_bf, glu_b.reshape(1, S5_WIDTH))


S5_LB = 128
S5_LBG = S5_LB // S5_GROUP


def _s5chunks_kernel(p_ref, o_ref, x_ref, sel_ref):
    kw = S5_CHUNK * S5_LB
    pad = (S5_LBG - 1) * S5_GROUP

    @pl.when(pl.program_id(0) == 0)
    def _():
        k = lax.broadcasted_iota(jnp.int32, sel_ref.shape, 0) - pad
        lane = lax.broadcasted_iota(jnp.int32, sel_ref.shape, 1)
        hit = (k >= 0) & ((k % S5_LB) // S5_GROUP == 0) & (lane == (k // S5_LB) * S5_GROUP + k % S5_GROUP)
        sel_ref[...] = jnp.where(hit, 1.0, 0.0).astype(BF16)

    for n in range(S5_NCHUNK):
        for i in range(S5_CHUNK):
            rows = p_ref[pl.ds(n * S5_CHUNK + i, S5_NSEQ, stride=S5_SEQ), :]
            x_ref[n * S5_NSEQ:(n + 1) * S5_NSEQ, i * S5_LB:(i + 1) * S5_LB] = rows.astype(BF16)
    x = x_ref[...]
    for gl in range(S5_LBG):
        off = pad - gl * S5_GROUP
        o_ref[:, gl * S5_CW:(gl + 1) * S5_CW] = _dot(x, sel_ref[off:off + kw, :]).astype(BF16)


def _s5_chunks(proj):
    kw = S5_CHUNK * S5_LB
    return pl.pallas_call(
        _s5chunks_kernel,
        out_shape=jax.ShapeDtypeStruct((S5_ROWS, S5_GROUPS * S5_CW), BF16),
        grid=(S5_WIDTH // S5_LB,),
        in_specs=[pl.BlockSpec((N_TOK, S5_LB), lambda b: (0, b))],
        out_specs=pl.BlockSpec((S5_ROWS, kw), lambda b: (0, b)),
        scratch_shapes=[pltpu.VMEM((S5_ROWS, kw), BF16), pltpu.VMEM((kw + S5_LB, S5_CW), BF16)],
        compiler_params=_params(("arbitrary",)),
        name="s5_chunk_layout",
    )(proj)


def _s5_mixer(proj, state_re, state_im, prep, s5_d, glu_w_bf, glu_b):
    toep, s_mat, in_mat, coef = prep
    u_chunks = _s5_chunks(proj)
    z = _s5_z(u_chunks, s_mat)
    coef2 = coef.transpose(0, 2, 1, 3).reshape(2, 8, S5_GROUPS * S5_SW)
    h0 = jnp.concatenate([state_re, state_im], axis=-1)
    h0 = h0.transpose(1, 0, 2, 3).reshape(2, DEC_BATCH, S5_GROUPS * S5_SW)
    hin, ends = _s5_scan(z, coef2, h0)
    y_tok = _s5_y(u_chunks, toep, hin, in_mat)
    out = _s5_out(y_tok, proj, s5_d, glu_w_bf, glu_b)
    fin = ends[:, :BATCH].reshape(2, BATCH, S5_GROUPS, S5_SW).transpose(1, 0, 2, 3)
    return out, fin[..., :S5_STATE], fin[..., S5_STATE:]


def _softmax_rows(parts):
    m = parts[0].max(axis=-1, keepdims=True)
    for s in parts[1:]:
        m = jnp.maximum(m, s.max(axis=-1, keepdims=True))
    ps = [jnp.exp(s - m) for s in parts]
    tot = ps[0].sum(axis=-1, keepdims=True)
    for p in ps[1:]:
        tot = tot + p.sum(axis=-1, keepdims=True)
    inv = 1.0 / tot
    return [(p * inv).astype(BF16) for p in ps]


def _ctxattn_kernel(q_ref, k_ref, v_ref, o_ref, nk_ref, nv_ref):
    for h in range(NA_HEADS):
        sl = slice(h * NA_HEAD_DIM, (h + 1) * NA_HEAD_DIM)
        k = k_ref[:, sl]
        v = v_ref[:, sl]
        nk_ref[h] = k
        nv_ref[h] = v
        q = (q_ref[:, sl] * QK_SCALE).astype(BF16)
        (p,) = _softmax_rows([_dot_nt(q, k.astype(BF16))])
        o_ref[:, sl] = _dot(p, v.astype(BF16)).astype(BF16)


def _ctx_attention(proj):
    col = lambda c: pl.BlockSpec((SEQ, NA_WIDTH), lambda b: (b, c))
    cache = jax.ShapeDtypeStruct((BATCH, 1, NA_HEADS, SEQ, NA_HEAD_DIM), F32)
    cache_spec = pl.BlockSpec((None, None, NA_HEADS, SEQ, NA_HEAD_DIM), lambda b: (b, 0, 0, 0, 0))
    return pl.pallas_call(
        _ctxattn_kernel,
        out_shape=(jax.ShapeDtypeStruct((N_PROMPT, NA_WIDTH), BF16), cache, cache),
        grid=(BATCH,),
        in_specs=[col(1), col(2), col(3)],
        out_specs=(pl.BlockSpec((SEQ, NA_WIDTH), lambda b: (b, 0)), cache_spec, cache_spec),
        compiler_params=_params(("parallel",)),
        name="ctx_attention",
    )(proj, proj, proj)


def _na_geometry(r0):
    ks = min(max(r0 - NA_WIN_R // 2, 0), GRID_ROWS - NA_KROWS)
    tiles = []
    for a in range(NA_QROWS):
        rq = r0 + a
        rs = min(max(rq - NA_WIN_R // 2, 0), GRID_ROWS - NA_WIN_R)
        row = []
        for rl in range(NA_KROWS):
            rk = ks + rl
            row.append(rk - rq + NA_WIN_R - 1 if rs <= rk < rs + NA_WIN_R else None)
        tiles.append(row)
    return ks, tiles


NA_VARIANT_ROW0 = (0, NA_QROWS, GRID_ROWS - NA_QROWS)


def _nabias_kernel(rpb_ref, o_ref, t_ref):
    h = pl.program_id(0)
    shape = (GRID_W, 2 * GRID_W)
    cq = lax.broadcasted_iota(jnp.int32, shape, 0)
    lane = lax.broadcasted_iota(jnp.int32, shape, 1)
    ck = lane % GRID_W
    dc = jnp.clip(ck - cq + (NA_WIN_C - 1), 0, NA_NDC - 1)
    c0 = jnp.clip(cq - NA_WIN_C // 2, 0, GRID_W - NA_WIN_C)
    in_cols = (ck >= c0) & (ck < c0 + NA_WIN_C)
    neg = jnp.full(shape, NEG_INF, F32)
    for dr in range(NA_NDR):
        t = neg
        for v in range(NA_NDC):
            t = jnp.where(dc == v, rpb_ref[h, dr * NA_NDC + v], t)
        t_ref[dr] = jnp.where(in_cols, t, neg)
    left = lane < GRID_W
    for var, r0 in enumerate(NA_VARIANT_ROW0):
        _, tiles = _na_geometry(r0)
        for a in range(NA_QROWS):
            for m in range(NA_KROWS // 2):
                dl, dr_ = tiles[a][2 * m], tiles[a][2 * m + 1]
                tl = neg if dl is None else t_ref[dl]
                tr = neg if dr_ is None else t_ref[dr_]
                o_ref[var, a * GRID_W:(a + 1) * GRID_W, m * 2 * GRID_W:(m + 1) * 2 * GRID_W] = (
                    jnp.where(left, tl, tr))


def _na_bias(rpb):
    nq, nk = NA_QROWS * GRID_W, NA_KROWS * GRID_W
    return pl.pallas_call(
        _nabias_kernel,
        out_shape=jax.ShapeDtypeStruct((NA_HEADS, 3, nq, nk), F32),
        grid=(NA_HEADS,),
        in_specs=[pl.BlockSpec(memory_space=pltpu.SMEM)],
        out_specs=pl.BlockSpec((None, 3, nq, nk), lambda h: (h, 0, 0, 0)),
        scratch_shapes=[pltpu.VMEM((NA_NDR, GRID_W, 2 * GRID_W), F32)],
        compiler_params=_params(("parallel",)),
        name="na_bias",
    )(rpb.reshape(NA_HEADS, NA_NDR * NA_NDC))


def _naattn_kernel(q_ref, k_ref, v_ref, kc_ref, vc_ref, bias_ref, o_ref):
    kb = k_ref[...].astype(BF16)
    vb = v_ref[...].astype(BF16)
    kc = kc_ref[...].astype(BF16)
    vc = vc_ref[...].astype(BF16)
    nq = NA_QROWS * GRID_W
    for blk in range(GRID_ROWS // NA_QROWS):
        r0 = blk * NA_QROWS
        var = 0 if blk == 0 else (2 if r0 == NA_VARIANT_ROW0[2] else 1)
        ks, _ = _na_geometry(r0)
        keys = slice(ks * GRID_W, (ks + NA_KROWS) * GRID_W)
        q = (q_ref[r0 * GRID_W:r0 * GRID_W + nq, :] * QK_SCALE).astype(BF16)
        s_loc = _dot_nt(q, kb[keys]) + bias_ref[var]
        s_ctx = _dot_nt(q, kc)
        p_loc, p_ctx = _softmax_rows([s_loc, s_ctx])
        o = _dot(p_loc, vb[keys]) + _dot(p_ctx, vc)
        o_ref[r0 * GRID_W:r0 * GRID_W + nq, :] = o.astype(BF16)


def _na_attention(proj, cache_k, cache_v, bias):
    first = N_PROMPT // DEC_SEQ
    col = lambda c: pl.BlockSpec((DEC_SEQ, NA_HEAD_DIM), lambda b, h: (first + b, c * NA_HEADS + h))
    cache_spec = pl.BlockSpec((None, None, None, PAST_LEN, NA_HEAD_DIM), lambda b, h: (b, 0, h, 0, 0))
    return pl.pallas_call(
        _naattn_kernel,
        out_shape=jax.ShapeDtypeStruct((DEC_BATCH * DEC_SEQ, NA_WIDTH), BF16),
        grid=(DEC_BATCH, NA_HEADS),
        in_specs=[col(1), col(2), col(3), cache_spec, cache_spec,
                  pl.BlockSpec((None, 3, NA_QROWS * GRID_W, NA_KROWS * GRID_W), lambda b, h: (h, 0, 0, 0))],
        out_specs=pl.BlockSpec((DEC_SEQ, NA_HEAD_DIM), lambda b, h: (b, h)),
        compiler_params=_params(("parallel", "parallel")),
        name="na_attention",
    )(proj, proj, proj, cache_k, cache_v, bias)


def _outproj_kernel(a_ref, b_ref, w_ref, x_ref, gate_ref, g_ref, beta_ref, o_ref):
    half = a_ref.shape[1]
    y = _dot(a_ref[...], w_ref[0:half, :]) + _dot(b_ref[...], w_ref[half:2 * half, :])
    o_ref[...] = _post_norm(x_ref[...], y, gate_ref[...], g_ref[...], beta_ref[...])


def _out_proj(y_s5, y_att, w_bf, x, mod, ln_g, ln_b, layer):
    tm = 512
    half = y_s5.shape[1]
    return pl.pallas_call(
        _outproj_kernel,
        out_shape=jax.ShapeDtypeStruct((N_TOK, D_MODEL), F32),
        grid=(N_TOK // tm,),
        in_specs=[pl.BlockSpec((tm, half), lambda i: (i, 0)),
                  pl.BlockSpec((tm, half), lambda i: (i, 0)),
                  pl.BlockSpec((D_MODEL, D_MODEL), lambda i: (0, 0)),
                  pl.BlockSpec((tm, D_MODEL), lambda i: (i, 0)),
                  _mod_spec(layer, 2, tm), _row_spec(layer, 0), _row_spec(layer, 0)],
        out_specs=pl.BlockSpec((tm, D_MODEL), lambda i: (i, 0)),
        compiler_params=_params(("parallel",)),
        name="out_proj_norm",
    )(y_s5, y_att, w_bf, x, mod, ln_g, ln_b)


def _ffn_kernel(x_ref, sh_ref, sc_ref, gate_ref, g_ref, beta_ref, wg_ref, wu_ref, wd_ref, o_ref, h_ref, acc_ref):
    f = pl.program_id(1)

    @pl.when(f == 0)
    def _():
        h_ref[...] = (x_ref[...] * (1.0 + sc_ref[...]) + sh_ref[...]).astype(BF16)
        acc_ref[...] = jnp.zeros_like(acc_ref)

    h = h_ref[...]
    a = _dot(h, wg_ref[...])
    b = _dot(h, wu_ref[...])
    acc_ref[...] += _dot((a * _sigmoid(a) * b).astype(BF16), wd_ref[...])

    @pl.when(f == pl.num_programs(1) - 1)
    def _():
        o_ref[...] = _post_norm(x_ref[...], acc_ref[...], gate_ref[...], g_ref[...], beta_ref[...])


def _ffn(x, mod, ln_g, ln_b, wg_bf, wu_bf, wd_bf, layer):
    tm, tf = 512, 512
    return pl.pallas_call(
        _ffn_kernel,
        out_shape=jax.ShapeDtypeStruct((N_TOK, D_MODEL), F32),
        grid=(N_TOK // tm, FFN_DIM // tf),
        in_specs=[pl.BlockSpec((tm, D_MODEL), lambda i, f: (i, 0)),
                  _mod_spec(layer, 3, tm), _mod_spec(layer, 4, tm), _mod_spec(layer, 5, tm),
                  _row_spec(layer, 1), _row_spec(layer, 1),
                  pl.BlockSpec((D_MODEL, tf), lambda i, f: (0, f)),
                  pl.BlockSpec((D_MODEL, tf), lambda i, f: (0, f)),
                  pl.BlockSpec((tf, D_MODEL), lambda i, f: (f, 0))],
        out_specs=pl.BlockSpec((tm, D_MODEL), lambda i, f: (i, 0)),
        scratch_shapes=[pltpu.VMEM((tm, D_MODEL), BF16), pltpu.VMEM((tm, D_MODEL), F32)],
        compiler_params=_params(("parallel", "arbitrary")),
        name="ffn_norm",
    )(x, mod, mod, mod, ln_g, ln_b, wg_bf, wu_bf, wd_bf)


def _pool_kernel(x_ref, prev_ref, next_ref, sh_ref, sc_ref, gate_ref, g_ref, beta_ref, w_ref, ps_ref,
                 o_ref, ext_ref, y_ref):
    q = pl.program_id(0)
    latent = q >= BATCH
    seg = (q - BATCH) % S5_SEGS
    has_prev = latent & (seg > 0)
    has_next = latent & (seg < S5_SEGS - 1)
    seq_len = jnp.where(latent, DEC_SEQ, SEQ)
    t = jnp.where(latent, seg * SEQ, 0) + lax.broadcasted_iota(jnp.int32, (SEQ, 1), 0)

    scale = 1.0 + sc_ref[...]
    shift = sh_ref[...]
    x = x_ref[...]
    halo = POOL_HALO
    ext_ref[0:halo, :] = jnp.where(has_prev, prev_ref[...] * scale + shift, 0.0)
    ext_ref[halo:halo + SEQ, :] = x * scale + shift
    ext_ref[halo + SEQ:2 * halo + SEQ, :] = jnp.where(has_next, next_ref[...] * scale + shift, 0.0)

    for g, w in enumerate(POOL_SIZES):
        cols = slice(g * POOL_GROUP_DIM, (g + 1) * POOL_GROUP_DIM)
        total = None
        for k in range(-(w // 2), w - w // 2):
            part = ext_ref[halo + k:halo + k + SEQ, cols]
            total = part if total is None else total + part
        count = jnp.minimum(t + (w - w // 2), seq_len) - jnp.maximum(t - w // 2, 0)
        pooled = total / count.astype(F32) - ext_ref[halo:halo + SEQ, cols]
        y_ref[:, cols] = _dot(pooled.astype(BF16), w_ref[g])
    y = y_ref[...] * ps_ref[...]
    o_ref[...] = _post_norm(x, y, gate_ref[...], g_ref[...], beta_ref[...])


def _pool(x, mod, ln_g, ln_b, w_bf, pool_scale, layer):
    tm = SEQ
    nhb = N_TOK // POOL_HALO
    per = tm // POOL_HALO
    return pl.pallas_call(
        _pool_kernel,
        out_shape=jax.ShapeDtypeStruct((N_TOK, D_MODEL), F32),
        grid=(N_TOK // tm,),
        in_specs=[pl.BlockSpec((tm, D_MODEL), lambda i: (i, 0)),
                  pl.BlockSpec((POOL_HALO, D_MODEL), lambda i: (jnp.maximum(i * per - 1, 0), 0)),
                  pl.BlockSpec((POOL_HALO, D_MODEL), lambda i: (jnp.minimum((i + 1) * per, nhb - 1), 0)),
                  _mod_spec(layer, 0, tm), _mod_spec(layer, 1, tm), _mod_spec(layer, 2, tm),
                  _row_spec(layer, 0), _row_spec(layer, 0),
                  pl.BlockSpec((len(POOL_SIZES), POOL_GROUP_DIM, POOL_GROUP_DIM), lambda i: (0, 0, 0)),
                  pl.BlockSpec((1, D_MODEL), lambda i: (0, 0))],
        out_specs=pl.BlockSpec((tm, D_MODEL), lambda i: (i, 0)),
        scratch_shapes=[pltpu.VMEM((tm + 2 * POOL_HALO, D_MODEL), F32), pltpu.VMEM((tm, D_MODEL), F32)],
        compiler_params=_params(("parallel",)),
        name="pool_norm",
    )(x, x, x, mod, mod, mod, ln_g, ln_b, w_bf, pool_scale.reshape(1, D_MODEL))


ROUTER_LANES = 128


def _split_bf16(a):
    hi = a.astype(BF16)
    return hi, (a - hi.astype(F32)).astype(BF16)


def _router_kernel(x_ref, sh_ref, sc_ref, w_ref, b_ref, h_ref, info_ref):
    h = x_ref[...] * (1.0 + sc_ref[...]) + sh_ref[...]
    h_ref[...] = h
    hh, hl = _split_bf16(h)
    wh, wl = _split_bf16(w_ref[...])
    logits = _dot(hh, wh) + _dot(hl, wh) + _dot(hh, wl) + b_ref[...]
    lane = lax.broadcasted_iota(jnp.int32, logits.shape, 1)
    logits = jnp.where(lane < N_EXPERTS, logits, -jnp.inf)
    m1 = logits.max(axis=-1, keepdims=True)
    i1 = jnp.where(logits == m1, lane, ROUTER_LANES).min(axis=-1, keepdims=True)
    rest = jnp.where(lane == i1, -jnp.inf, logits)
    m2 = rest.max(axis=-1, keepdims=True)
    i2 = jnp.where(rest == m2, lane, ROUTER_LANES).min(axis=-1, keepdims=True)
    e = jnp.exp(m2 - m1)
    g1 = 1.0 / (1.0 + e)
    g2 = e / (1.0 + e)
    info = jnp.where(lane == 0, i1.astype(F32), jnp.where(lane == 1, i2.astype(F32),
                     jnp.where(lane == 2, g1, jnp.where(lane == 3, g2, 0.0))))
    info_ref[...] = info


def _router(x, mod, router_w, router_b, layer):
    tm = 512
    w = jnp.zeros((D_MODEL, ROUTER_LANES), F32).at[:, :N_EXPERTS].set(router_w)
    b = jnp.zeros((1, ROUTER_LANES), F32).at[0, :N_EXPERTS].set(router_b)
    return pl.pallas_call(
        _router_kernel,
        out_shape=(jax.ShapeDtypeStruct((N_TOK, D_MODEL), F32),
                   jax.ShapeDtypeStruct((N_TOK, ROUTER_LANES), F32)),
        grid=(N_TOK // tm,),
        in_specs=[pl.BlockSpec((tm, D_MODEL), lambda i: (i, 0)),
                  _mod_spec(layer, 3, tm), _mod_spec(layer, 4, tm),
                  pl.BlockSpec((D_MODEL, ROUTER_LANES), lambda i: (0, 0)),
                  pl.BlockSpec((1, ROUTER_LANES), lambda i: (0, 0))],
        out_specs=(pl.BlockSpec((tm, D_MODEL), lambda i: (i, 0)),
                   pl.BlockSpec((tm, ROUTER_LANES), lambda i: (i, 0))),
        compiler_params=_params(("parallel",)),
        name="moe_router",
    )(x, mod, mod, w, b)


def _routing_tables(info):
    experts = info[:, :2].astype(jnp.int32).reshape(-1)
    onehot = (experts[:, None] == jnp.arange(N_EXPERTS)[None, :]).astype(jnp.int32)
    counts = onehot.sum(axis=0)
    rank = ((jnp.cumsum(onehot, axis=0) - onehot) * onehot).sum(axis=1)
    rows = (counts + MOE_TILE - 1) // MOE_TILE * MOE_TILE
    span = (rows + MOE_SPAN - 1) // MOE_SPAN * MOE_SPAN
    ends = jnp.cumsum(span)
    starts = ends - span
    pos = (starts[experts] + rank).astype(jnp.int32)
    src = jnp.zeros((MOE_ROWS,), jnp.int32).at[pos].set(jnp.arange(2 * N_TOK, dtype=jnp.int32) // 2)
    tile_row = jnp.arange(MOE_NTILES, dtype=jnp.int32) * MOE_TILE
    owner = jnp.minimum(jnp.sum(tile_row[:, None] >= ends[None, :], axis=1), N_EXPERTS - 1)
    used = (tile_row < (starts + rows)[owner]) & (tile_row < ends[-1])
    fetch = lax.cummax(jnp.where(used, jnp.arange(MOE_NTILES, dtype=jnp.int32), 0))
    tile_tab = (used.astype(jnp.int32), fetch.astype(jnp.int32), owner[fetch].astype(jnp.int32))
    pos = pos.reshape(N_TOK, 2)
    return src, tile_tab, pos[:, 0], pos[:, 1]


def _row_copy(src_hbm, row, dst, r, sem):
    return pltpu.make_async_copy(src_hbm.at[pl.ds(row, 1), :], dst.at[pl.ds(r, 1), :], sem)


def _dispatch_kernel(src_ref, used_ref, h_hbm, o_ref, buf, sem):
    m = pl.program_id(0)
    used = used_ref[m] == 1

    @pl.when(used)
    def _():
        def issue(r, c):
            _row_copy(h_hbm, src_ref[m * MOE_TILE + r], buf, r, sem).start()
            return c

        lax.fori_loop(0, MOE_TILE, issue, 0, unroll=8)

        def drain(r, c):
            _row_copy(h_hbm, 0, buf, r, sem).wait()
            return c

        lax.fori_loop(0, MOE_TILE, drain, 0, unroll=8)
        o_ref[...] = buf[...].astype(BF16)

    @pl.when(jnp.logical_not(used))
    def _():
        o_ref[...] = jnp.zeros_like(o_ref)


def _dispatch(h, src, tile_tab):
    return pl.pallas_call(
        _dispatch_kernel,
        out_shape=jax.ShapeDtypeStruct((MOE_ROWS, D_MODEL), BF16),
        grid_spec=pltpu.PrefetchScalarGridSpec(
            num_scalar_prefetch=2,
            grid=(MOE_NTILES,),
            in_specs=[pl.BlockSpec(memory_space=pl.ANY)],
            out_specs=pl.BlockSpec((MOE_TILE, D_MODEL), lambda m, *_: (m, 0)),
            scratch_shapes=[pltpu.VMEM((MOE_TILE, D_MODEL), F32), pltpu.SemaphoreType.DMA(())],
        ),
        compiler_params=_params(("arbitrary",)),
        name="moe_dispatch",
    )(src, tile_tab[0], h)


def _gmm_up_kernel(used_ref, fetch_ref, exp_ref, x_ref, wg_ref, wu_ref, o_ref, wgb_ref, wub_ref):
    j = pl.program_id(1)
    t0 = j * MOE_PAIR
    used = used_ref[t0] == 1
    pair = used_ref[t0 + 1] == 1
    fresh = jnp.logical_or(j == 0, exp_ref[t0] != exp_ref[jnp.maximum(t0 - MOE_PAIR, 0)])

    @pl.when(jnp.logical_and(used, fresh))
    def _():
        wgb_ref[...] = wg_ref[...].astype(BF16)
        wub_ref[...] = wu_ref[...].astype(BF16)

    def swiglu_up(x):
        a = _dot(x, wgb_ref[...])
        b = _dot(x, wub_ref[...])
        return (a * _sigmoid(a) * b).astype(BF16)

    @pl.when(jnp.logical_and(used, pair))
    def _():
        o_ref[...] = swiglu_up(x_ref[...])

    @pl.when(jnp.logical_and(used, jnp.logical_not(pair)))
    def _():
        o_ref[0:MOE_TILE, :] = swiglu_up(x_ref[0:MOE_TILE, :])
        o_ref[MOE_TILE:MOE_SPAN, :] = jnp.zeros((MOE_SPAN - MOE_TILE, o_ref.shape[1]), BF16)

    @pl.when(jnp.logical_not(used))
    def _():
        o_ref[...] = jnp.zeros_like(o_ref)


def _gmm_up(xs, w_gate, w_up, tile_tab):
    tf = 1024
    blk = lambda j, fetch: fetch[j * MOE_PAIR] // MOE_PAIR
    wsp = pl.BlockSpec((None, D_MODEL, tf), lambda f, j, used, fetch, exp: (exp[j * MOE_PAIR], 0, f))
    return pl.pallas_call(
        _gmm_up_kernel,
        out_shape=jax.ShapeDtypeStruct((MOE_ROWS, EXPERT_DIM), BF16),
        grid_spec=pltpu.PrefetchScalarGridSpec(
            num_scalar_prefetch=3,
            grid=(EXPERT_DIM // tf, MOE_NTILES // MOE_PAIR),
            in_specs=[pl.BlockSpec((MOE_SPAN, D_MODEL), lambda f, j, used, fetch, exp: (blk(j, fetch), 0)),
                      wsp, wsp],
            out_specs=pl.BlockSpec((MOE_SPAN, tf), lambda f, j, used, fetch, exp: (j, f)),
            scratch_shapes=[pltpu.VMEM((D_MODEL, tf), BF16), pltpu.VMEM((D_MODEL, tf), BF16)],
        ),
        compiler_params=_params(("parallel", "arbitrary")),
        name="moe_gate_up",
    )(*tile_tab, xs, w_gate, w_up)


def _gmm_down_kernel(used_ref, fetch_ref, exp_ref, x_ref, w_ref, o_ref, wbf_ref):
    m = pl.program_id(1)
    used = used_ref[m] == 1
    fresh = jnp.logical_or(m == 0, exp_ref[m] != exp_ref[jnp.maximum(m - 1, 0)])

    @pl.when(jnp.logical_and(used, fresh))
    def _():
        wbf_ref[...] = w_ref[...].astype(BF16)

    @pl.when(used)
    def _():
        o_ref[...] = _dot(x_ref[...], wbf_ref[...])

    @pl.when(jnp.logical_not(used))
    def _():
        o_ref[...] = jnp.zeros_like(o_ref)


def _gmm_down(g, w_down, tile_tab):
    tn = 512
    return pl.pallas_call(
        _gmm_down_kernel,
        out_shape=jax.ShapeDtypeStruct((MOE_ROWS, D_MODEL), F32),
        grid_spec=pltpu.PrefetchScalarGridSpec(
            num_scalar_prefetch=3,
            grid=(D_MODEL // tn, MOE_NTILES),
            in_specs=[pl.BlockSpec((MOE_TILE, EXPERT_DIM), lambda n, m, used, fetch, exp: (fetch[m], 0)),
                      pl.BlockSpec((None, EXPERT_DIM, tn), lambda n, m, used, fetch, exp: (exp[m], 0, n))],
            out_specs=pl.BlockSpec((MOE_TILE, tn), lambda n, m, used, fetch, exp: (m, n)),
            scratch_shapes=[pltpu.VMEM((EXPERT_DIM, tn), BF16)],
        ),
        compiler_params=_params(("parallel", "arbitrary")),
        name="moe_down",
    )(*tile_tab, g, w_down)


def _combine_kernel(p1_ref, p2_ref, y_hbm, x_ref, info_ref, gate_ref, g_ref, beta_ref, o_ref, b1, b2, sem):
    i = pl.program_id(0)
    tm = x_ref.shape[0]

    def issue(r, c):
        _row_copy(y_hbm, p1_ref[i * tm + r], b1, r, sem).start()
        _row_copy(y_hbm, p2_ref[i * tm + r], b2, r, sem).start()
        return c

    lax.fori_loop(0, tm, issue, 0, unroll=8)

    def drain(r, c):
        _row_copy(y_hbm, 0, b1, r, sem).wait()
        _row_copy(y_hbm, 0, b2, r, sem).wait()
        return c

    lax.fori_loop(0, tm, drain, 0, unroll=8)
    info = info_ref[...]
    y = info[:, 2:3] * b1[...] + info[:, 3:4] * b2[...]
    o_ref[...] = _post_norm(x_ref[...], y, gate_ref[...], g_ref[...], beta_ref[...])


def _combine(y_sorted, pos1, pos2, x, info, mod, ln_g, ln_b, layer):
    tm = 256
    return pl.pallas_call(
        _combine_kernel,
        out_shape=jax.ShapeDtypeStruct((N_TOK, D_MODEL), F32),
        grid_spec=pltpu.PrefetchScalarGridSpec(
            num_scalar_prefetch=2,
            grid=(N_TOK // tm,),
            in_specs=[pl.BlockSpec(memory_space=pl.ANY),
                      pl.BlockSpec((tm, D_MODEL), lambda i, *_: (i, 0)),
                      pl.BlockSpec((tm, ROUTER_LANES), lambda i, *_: (i, 0)),
                      _mod_spec(layer, 5, tm), _row_spec(layer, 1), _row_spec(layer, 1)],
            out_specs=pl.BlockSpec((tm, D_MODEL), lambda i, *_: (i, 0)),
            scratch_shapes=[pltpu.VMEM((tm, D_MODEL), F32), pltpu.VMEM((tm, D_MODEL), F32),
                            pltpu.SemaphoreType.DMA(())],
        ),
        compiler_params=_params(("arbitrary",)),
        name="moe_combine_norm",
    )(pos1, pos2, y_sorted, x, info, mod, ln_g, ln_b)


def _moe(x, mod, ln_g, ln_b, router_w, router_b, w_gate, w_up, w_down, layer):
    h, info = _router(x, mod, router_w, router_b, layer)
    src, tile_tab, pos1, pos2 = _routing_tables(info)
    xs = _dispatch(h, src, tile_tab)
    g = _gmm_up(xs, w_gate, w_up, tile_tab)
    y = _gmm_down(g, w_down, tile_tab)
    return _combine(y, pos1, pos2, x, info, mod, ln_g, ln_b, layer)


def kernel(x_prompt, x_sample, state_s5_re, state_s5_im, cache_k, cache_v, c, c_ctx, ada_w, ada_b, ln_g, ln_b, ab_w_in, ab_w_out, s5_lambda_re, s5_lambda_im, s5_log_dt, s5_b_re, s5_b_im, s5_c_re, s5_c_im, s5_d, s5_glu_w, s5_glu_b, na_rpb, ffn_w_gate, ffn_w_up, ffn_w_down, pool_w, pool_scale, moe_router_w, moe_router_b, moe_w_gate, moe_w_up, moe_w_down):
    x = jnp.concatenate([x_prompt.reshape(N_PROMPT, D_MODEL), x_sample.reshape(-1, D_MODEL)], axis=0)

    cond8 = jnp.zeros((8, D_MODEL), F32).at[0].set(c_ctx).at[1:N_COND].set(c)
    mod = _ada(cond8, ada_w, ada_b)
    mod = mod[:, :N_COND].reshape(DEPTH, N_COND, N_MOD, 1, D_MODEL).transpose(0, 2, 1, 3, 4)
    ln_g4 = ln_g.reshape(DEPTH, 2, 1, D_MODEL)
    ln_b4 = ln_b.reshape(DEPTH, 2, 1, D_MODEL)

    proj = _inproj(x, mod, _cast_bf16(ab_w_in[0]), 0)
    prep = _s5_prep(s5_lambda_re[0], s5_lambda_im[0], s5_log_dt[0], s5_b_re[0], s5_b_im[0],
                    s5_c_re[0], s5_c_im[0])
    y_s5, fin_re, fin_im = _s5_mixer(proj, state_s5_re[:, 0], state_s5_im[:, 0], prep, s5_d[0],
                                     _cast_bf16(s5_glu_w[0]), s5_glu_b[0])
    y_ctx, new_k, new_v = _ctx_attention(proj)
    y_na = _na_attention(proj, cache_k, cache_v, _na_bias(na_rpb[0]))
    y_att = jnp.concatenate([y_ctx, y_na], axis=0)
    x = _out_proj(y_s5, y_att, _cast_bf16(ab_w_out[0]), x, mod, ln_g4, ln_b4, 0)
    x = _ffn(x, mod, ln_g4, ln_b4, _cast_bf16(ffn_w_gate[0]), _cast_bf16(ffn_w_up[0]),
             _cast_bf16(ffn_w_down[0]), 0)

    x = _pool(x, mod, ln_g4, ln_b4, _cast_bf16(pool_w[0]), pool_scale[0], 1)
    x = _moe(x, mod, ln_g4, ln_b4, moe_router_w[0], moe_router_b[0], moe_w_gate[0], moe_w_up[0],
             moe_w_down[0], 1)

    y_prompt = x[:N_PROMPT].reshape(BATCH, SEQ, D_MODEL)
    y_sample = x[N_PROMPT:].reshape(DEC_BATCH, DEC_SEQ, D_MODEL)
    return (y_prompt, y_sample, fin_re[:, None], fin_im[:, None], new_k, new_v)
```

```python
import functools
import math

import jax
import jax.numpy as jnp
from jax import lax
from jax.experimental import pallas as pl
from jax.experimental.pallas import tpu as pltpu

F32 = jnp.float32
BF16 = jnp.bfloat16

D_MODEL = 2048
BATCH = 16
SEQ = 256
DEPTH = 2
DEC_BATCH = 2
DEC_SEQ = 2048
N_MOD = 6
N_PROMPT = BATCH * SEQ
N_TOK = N_PROMPT + DEC_BATCH * DEC_SEQ
N_COND = 1 + DEC_BATCH

S5_WIDTH = 1024
S5_GROUP = 16
S5_GROUPS = 64
S5_STATE = 64
S5_CHUNK = 16
S5_SEQ = 256
S5_NSEQ = N_TOK // S5_SEQ
S5_NCHUNK = S5_SEQ // S5_CHUNK
S5_ROWS = S5_NSEQ * S5_NCHUNK
S5_SEGS = DEC_SEQ // S5_SEQ
S5_CW = S5_CHUNK * S5_GROUP
S5_SW = 2 * S5_STATE

NA_WIDTH = 1024
NA_HEADS = 8
NA_HEAD_DIM = 128
NA_WIN_R = 8
NA_WIN_C = 16
GRID_W = 64
GRID_ROWS = DEC_SEQ // GRID_W
NA_QROWS = 4
NA_KROWS = 12
NA_NDR = 2 * NA_WIN_R - 1
NA_NDC = 2 * NA_WIN_C - 1
PAST_LEN = 256

POOL_SIZES = (2, 4, 8, 16)
POOL_GROUP_DIM = 512
POOL_HALO = 16

FFN_DIM = 5632
N_EXPERTS = 8
EXPERT_DIM = 7168
ROW_SLABS = D_MODEL // 128
MOE_TILE = 256
MOE_PAIR = 2
MOE_SPAN = MOE_PAIR * MOE_TILE
MOE_ROWS = 2 * N_TOK + N_EXPERTS * MOE_SPAN
MOE_NTILES = MOE_ROWS // MOE_TILE

LN_EPS = 1e-5
DEEPNORM_ALPHA = (2.0 * DEPTH) ** 0.25
NEG_INF = -1e30
QK_SCALE = NA_HEAD_DIM ** -0.5

VMEM_LIMIT = 52 * 1024 * 1024
VMEM_LIMIT_MAX = 60 * 1024 * 1024


def _params(sem, vmem=VMEM_LIMIT):
    return pltpu.CompilerParams(dimension_semantics=sem, vmem_limit_bytes=vmem)


def _sigmoid(x):
    return 1.0 / (1.0 + jnp.exp(-x))


def _dot(a, b):
    return jnp.dot(a, b, preferred_element_type=F32)


def _dot_nt(a, b):
    return lax.dot_general(a, b, (((1,), (1,)), ((), ())), preferred_element_type=F32)


def _dot_exact(a, b):
    return jnp.dot(a, b, preferred_element_type=F32, precision=lax.Precision.HIGHEST)


def _cond_of_row(row0):
    return jnp.where(row0 < N_PROMPT, 0, 1 + (row0 - N_PROMPT) // DEC_SEQ)


def _mod_spec(layer, k, tm):
    return pl.BlockSpec((None, None, None, 1, D_MODEL),
                        lambda i, *_: (layer, k, _cond_of_row(i * tm), 0, 0))


def _row_spec(layer_idx, k=None):
    if k is None:
        return pl.BlockSpec((None, 1, D_MODEL), lambda *_: (layer_idx, 0, 0))
    return pl.BlockSpec((None, None, 1, D_MODEL), lambda *_: (layer_idx, k, 0, 0))


def _post_norm(x, y, gate, g, b):
    v = DEEPNORM_ALPHA * x + gate * y
    mu = jnp.mean(v, axis=-1, keepdims=True)
    c = v - mu
    var = jnp.mean(c * c, axis=-1, keepdims=True)
    return c * lax.rsqrt(var + LN_EPS) * g + b


def _ada_kernel(c_ref, w_ref, b_ref, o_ref):
    c = c_ref[...]
    s = (c * _sigmoid(c)).astype(BF16)
    o_ref[...] = _dot(s, w_ref[...].astype(BF16)) + b_ref[...]


def _ada(cond8, ada_w, ada_b):
    tn = 1024
    n_out = N_MOD * D_MODEL
    return pl.pallas_call(
        _ada_kernel,
        out_shape=jax.ShapeDtypeStruct((DEPTH, 8, n_out), F32),
        grid=(DEPTH, n_out // tn),
        in_specs=[pl.BlockSpec((8, D_MODEL), lambda l, n: (0, 0)),
                  pl.BlockSpec((None, D_MODEL, tn), lambda l, n: (l, 0, n)),
                  pl.BlockSpec((None, 1, tn), lambda l, n: (l, 0, n))],
        out_specs=pl.BlockSpec((None, 8, tn), lambda l, n: (l, 0, n)),
        compiler_params=_params(("parallel", "parallel")),
        name="ada_mod",
    )(cond8, ada_w, ada_b.reshape(DEPTH, 1, n_out))


def _cast_kernel(w_ref, o_ref):
    o_ref[...] = w_ref[...].astype(BF16)


def _cast_bf16(w):
    shape = w.shape
    cols = shape[-1]
    rows = w.size // cols
    rb = 8
    while rb * 2 * cols * 4 <= 4 * 1024 * 1024 and rows % (rb * 2) == 0:
        rb *= 2
    out = pl.pallas_call(
        _cast_kernel,
        out_shape=jax.ShapeDtypeStruct((rows, cols), BF16),
        grid=(rows // rb,),
        in_specs=[pl.BlockSpec((rb, cols), lambda i: (i, 0))],
        out_specs=pl.BlockSpec((rb, cols), lambda i: (i, 0)),
        compiler_params=_params(("parallel",)),
        name="cast_bf16",
    )(w.reshape(rows, cols))
    return out.reshape(shape)


def _inproj_kernel(x_ref, sh_ref, sc_ref, w_ref, o_ref, h_ref):
    @pl.when(pl.program_id(1) == 0)
    def _():
        h_ref[...] = (x_ref[...] * (1.0 + sc_ref[...]) + sh_ref[...]).astype(BF16)

    o_ref[...] = _dot(h_ref[...], w_ref[...])


def _inproj(x, mod, w_bf, layer):
    tm, tn = 1024, 1024
    n_out = w_bf.shape[1]
    return pl.pallas_call(
        _inproj_kernel,
        out_shape=jax.ShapeDtypeStruct((N_TOK, n_out), F32),
        grid=(N_TOK // tm, n_out // tn),
        in_specs=[pl.BlockSpec((tm, D_MODEL), lambda i, n: (i, 0)),
                  _mod_spec(layer, 0, tm), _mod_spec(layer, 1, tm),
                  pl.BlockSpec((D_MODEL, tn), lambda i, n: (0, n))],
        out_specs=pl.BlockSpec((tm, tn), lambda i, n: (i, n)),
        scratch_shapes=[pltpu.VMEM((tm, D_MODEL), BF16)],
        compiler_params=_params(("parallel", "arbitrary")),
        name="in_proj",
    )(x, mod, mod, w_bf)


def _cpow_table(re, im, n):
    out = [(jnp.ones_like(re), jnp.zeros_like(im))]
    for _ in range(n):
        pr, pi = out[-1]
        out.append((pr * re - pi * im, pr * im + pi * re))
    return out


def _lam_bar(lam):
    re = jnp.minimum(lam[0], -1e-4)
    im = lam[1]
    dt = jnp.exp(lam[2])
    mag = jnp.exp(re * dt)
    return re, im, mag * jnp.cos(im * dt), mag * jnp.sin(im * dt)


def _s5prep_kernel(lamc_ref, lamr_ref, ct_ref, bt_ref, b12_ref, toep_ref, s_ref, in_ref, co_ref):
    fwd = pl.program_id(0) == 0
    P, G, T = S5_STATE, S5_GROUP, S5_CHUNK

    lc = lamc_ref[...]
    _, _, lbr_c, lbi_c = _lam_bar((lc[:, 0:1], lc[:, 1:2], lc[:, 2:3]))
    pw_c = _cpow_table(lbr_c, lbi_c, T)

    def pick_cols(e_idx, shape):
        pr = jnp.zeros(shape, F32)
        pi = jnp.zeros(shape, F32)
        for e in range(T + 1):
            hit = e_idx == e
            pr = jnp.where(hit, pw_c[e][0], pr)
            pi = jnp.where(hit, pw_c[e][1], pi)
        return pr, pi

    wide = 2 * T * G
    lane = lax.broadcasted_iota(jnp.int32, (G, wide), 1)
    row = lax.broadcasted_iota(jnp.int32, (G, wide), 0)
    expand = (lane % G == row).astype(F32)
    ct_re = _dot_exact(ct_ref[0], expand)
    ct_im = _dot_exact(ct_ref[1], expand)

    blk = lax.broadcasted_iota(jnp.int32, (P, wide), 1) // G
    e_idx = jnp.where(fwd, blk - T, T - blk)
    pr, pi = pick_cols(e_idx, (P, wide))
    r_re = pr * ct_re - pi * ct_im
    r_im = pr * ct_im + pi * ct_re

    lr = lamr_ref[...]
    re_r, im_r, lbr_r, lbi_r = _lam_bar((lr[0:1, :], lr[1:2, :], lr[2:3, :]))
    den = re_r * re_r + im_r * im_r
    cf_re = ((lbr_r - 1.0) * re_r + lbi_r * im_r) / den
    cf_im = (lbi_r * re_r - (lbr_r - 1.0) * im_r) / den
    bb_re = cf_re[:, :P] * bt_ref[0] - cf_im[:, :P] * bt_ref[1]
    bb_im = cf_re[:, :P] * bt_ref[1] + cf_im[:, :P] * bt_ref[0]
    kext = _dot_exact(bb_re, r_re) - _dot_exact(bb_im, r_im)
    for i in range(T):
        off = (T - i) * G
        toep_ref[i * G:(i + 1) * G, :] = kext[:, off:off + T * G].astype(BF16)

    blk2 = lax.broadcasted_iota(jnp.int32, (P, T * G), 1) // G
    e_in = jnp.where(fwd, blk2 + 1, T - blk2)
    qr, qi = pick_cols(e_in, (P, T * G))
    c_re = ct_re[:, :T * G]
    c_im = ct_im[:, :T * G]
    in_ref[0:P, :] = (qr * c_re - qi * c_im).astype(BF16)
    in_ref[P:2 * P, :] = (-(qr * c_im + qi * c_re)).astype(BF16)

    pw_r = _cpow_table(lbr_r, lbi_r, T)
    b1 = b12_ref[0]
    b2 = b12_ref[1]
    for i in range(T):
        wr = jnp.where(fwd, pw_r[T - 1 - i][0], pw_r[i][0])
        wi = jnp.where(fwd, pw_r[T - 1 - i][1], pw_r[i][1])
        sr = wr * cf_re - wi * cf_im
        si = wr * cf_im + wi * cf_re
        s_ref[i * G:(i + 1) * G, :] = (sr * b1 + si * b2).astype(BF16)

    sgn = jnp.where(lax.broadcasted_iota(jnp.int32, (1, S5_SW), 1) < P, -1.0, 1.0)
    r16, i16 = pw_r[T]
    r256, i256 = r16, i16
    for _ in range(4):
        r256, i256 = r256 * r256 - i256 * i256, 2.0 * r256 * i256
    for k, rowv in enumerate((r16, sgn * i16, r256, sgn * i256)):
        co_ref[k:k + 1, :] = rowv
    co_ref[4:8, :] = jnp.zeros((4, S5_SW), F32)


def _s5_prep(lam_re, lam_im, log_dt, b_re, b_im, c_re, c_im):
    P, G = S5_STATE, S5_GROUP
    ldt = jnp.broadcast_to(log_dt[..., None], lam_re.shape)
    lamc = jnp.stack([lam_re, lam_im, ldt], axis=-1)
    dup = lambda a: jnp.concatenate([a, a], axis=-1)
    lamr = jnp.stack([dup(lam_re), dup(lam_im), dup(ldt)], axis=-2)
    ct = jnp.stack([jnp.swapaxes(c_re, -1, -2), jnp.swapaxes(c_im, -1, -2)], axis=2)
    bt_re = jnp.swapaxes(b_re, -1, -2)
    bt_im = jnp.swapaxes(b_im, -1, -2)
    bt = jnp.stack([bt_re, bt_im], axis=2)
    b12 = jnp.stack([jnp.concatenate([bt_re, bt_im], -1),
                     jnp.concatenate([-bt_im, bt_re], -1)], axis=2)
    blk = lambda *s: pl.BlockSpec((None, None) + s, lambda d, g: (d, g) + (0,) * len(s))
    return pl.pallas_call(
        _s5prep_kernel,
        out_shape=(jax.ShapeDtypeStruct((2, S5_GROUPS, S5_CW, S5_CW), BF16),
                   jax.ShapeDtypeStruct((2, S5_GROUPS, S5_CW, S5_SW), BF16),
                   jax.ShapeDtypeStruct((2, S5_GROUPS, S5_SW, S5_CW), BF16),
                   jax.ShapeDtypeStruct((2, S5_GROUPS, 8, S5_SW), F32)),
        grid=(2, S5_GROUPS),
        in_specs=[blk(P, 3), blk(3, 2 * P), blk(2, P, G), blk(2, G, P), blk(2, G, 2 * P)],
        out_specs=(blk(S5_CW, S5_CW), blk(S5_CW, S5_SW), blk(S5_SW, S5_CW), blk(8, S5_SW)),
        compiler_params=_params(("parallel", "parallel")),
        name="s5_prep",
    )(lamc, lamr, ct, bt, b12)


S5_GPS = 4
S5_LT = 1024


def _s5z_kernel(u_ref, s_ref, z_ref):
    for gg in range(S5_GPS):
        ug = u_ref[:, gg * S5_CW:(gg + 1) * S5_CW]
        for d in range(2):
            z_ref[d, :, gg * S5_SW:(gg + 1) * S5_SW] = _dot(ug, s_ref[d, gg])


def _s5_z(u_chunks, s_mat):
    return pl.pallas_call(
        _s5z_kernel,
        out_shape=jax.ShapeDtypeStruct((2, S5_ROWS, S5_GROUPS * S5_SW), F32),
        grid=(S5_GROUPS // S5_GPS,),
        in_specs=[pl.BlockSpec((S5_ROWS, S5_GPS * S5_CW), lambda g: (0, g)),
                  pl.BlockSpec((2, S5_GPS, S5_CW, S5_SW), lambda g: (0, g, 0, 0))],
        out_specs=pl.BlockSpec((2, S5_ROWS, S5_GPS * S5_SW), lambda g: (0, 0, g)),
        compiler_params=_params(("parallel",)),
        name="s5_chunk_state",
    )(u_chunks, s_mat)


def _s5scan_kernel(z_ref, co_ref, h0_ref, hin_ref, e_ref, g_ref):
    ns = S5_LT // S5_SW
    nq = S5_NSEQ
    lanes = [slice(j * S5_SW, (j + 1) * S5_SW) for j in range(ns)]

    def cmul(a1, a2, h):
        return a1 * h + a2 * pltpu.roll(h, S5_STATE, 1)

    for d in range(2):
        a16 = [(co_ref[d, 0:1, l], co_ref[d, 1:2, l]) for l in lanes]
        a256 = [(co_ref[d, 2:3, l], co_ref[d, 3:4, l]) for l in lanes]
        order = list(range(S5_NCHUNK)) if d == 0 else list(range(S5_NCHUNK - 1, -1, -1))

        def run(h, write):
            for n in order:
                rows = slice(n * nq, (n + 1) * nq)
                if write:
                    for j, l in enumerate(lanes):
                        hin_ref[d, rows, l] = h[j]
                h = [cmul(a16[j][0], a16[j][1], h[j]) + z_ref[d, rows, l] for j, l in enumerate(lanes)]
            return h

        end = run([jnp.zeros((nq, S5_SW), F32)] * ns, False)
        for j, l in enumerate(lanes):
            e_ref[d, :, l] = end[j]

        g_ref[0:BATCH, :] = jnp.zeros((BATCH, S5_LT), F32)
        segs = list(range(S5_SEGS)) if d == 0 else list(range(S5_SEGS - 1, -1, -1))
        for j, l in enumerate(lanes):
            for b in range(DEC_BATCH):
                g = jnp.broadcast_to(h0_ref[d, b:b + 1, l], (8, S5_SW))
                for k, s in enumerate(segs):
                    r = BATCH + b * S5_SEGS + s
                    g_ref[r:r + 1, l] = g[0:1, :]
                    if k + 1 < S5_SEGS:
                        g = cmul(a256[j][0], a256[j][1], g) + jnp.broadcast_to(e_ref[d, r:r + 1, l], (8, S5_SW))

        run([g_ref[:, l] for l in lanes], True)


def _s5_scan(z, coef, h0):
    nl = S5_GROUPS * S5_SW
    return pl.pallas_call(
        _s5scan_kernel,
        out_shape=(jax.ShapeDtypeStruct((2, S5_ROWS, nl), F32),
                   jax.ShapeDtypeStruct((2, S5_NSEQ, nl), F32)),
        grid=(nl // S5_LT,),
        in_specs=[pl.BlockSpec((2, S5_ROWS, S5_LT), lambda t: (0, 0, t)),
                  pl.BlockSpec((2, 8, S5_LT), lambda t: (0, 0, t)),
                  pl.BlockSpec((2, DEC_BATCH, S5_LT), lambda t: (0, 0, t))],
        out_specs=(pl.BlockSpec((2, S5_ROWS, S5_LT), lambda t: (0, 0, t)),
                   pl.BlockSpec((2, S5_NSEQ, S5_LT), lambda t: (0, 0, t))),
        scratch_shapes=[pltpu.VMEM((S5_NSEQ, S5_LT), F32)],
        compiler_params=_params(("parallel",)),
        name="s5_scan",
    )(z, coef, h0)


S5_YG = 8


def _s5y_kernel(u_ref, t_ref, hin_ref, in_ref, o_ref, y_ref, sel_ref):
    kw = S5_YG * S5_CW
    pad = (S5_CHUNK - 1) * S5_GROUP

    @pl.when(pl.program_id(0) == 0)
    def _():
        k = lax.broadcasted_iota(jnp.int32, sel_ref.shape, 0) - pad
        lane = lax.broadcasted_iota(jnp.int32, sel_ref.shape, 1)
        hit = (k >= 0) & ((k % S5_CW) // S5_GROUP == 0) & (lane == (k // S5_CW) * S5_GROUP + k % S5_GROUP)
        sel_ref[...] = jnp.where(hit, 1.0, 0.0).astype(BF16)

    for gg in range(S5_YG):
        ug = u_ref[:, gg * S5_CW:(gg + 1) * S5_CW]
        acc = None
        for d in range(2):
            h = hin_ref[d, :, gg * S5_SW:(gg + 1) * S5_SW].astype(BF16)
            t = _dot(ug, t_ref[d, gg]) + _dot(h, in_ref[d, gg])
            acc = t if acc is None else acc + t
        y_ref[:, gg * S5_CW:(gg + 1) * S5_CW] = acc

    y = y_ref[...]
    hi, lo = _split_bf16(y)
    for j in range(S5_CHUNK):
        off = pad - j * S5_GROUP
        pick = sel_ref[off:off + kw, :]
        yj = _dot(hi, pick) + _dot(lo, pick)
        for n in range(S5_NCHUNK):
            o_ref[pl.ds(n * S5_CHUNK + j, S5_NSEQ, stride=S5_SEQ), :] = yj[n * S5_NSEQ:(n + 1) * S5_NSEQ, :]


def _s5_y(u_chunks, toep, hin, in_mat):
    kw = S5_YG * S5_CW
    lanes = S5_YG * S5_GROUP
    return pl.pallas_call(
        _s5y_kernel,
        out_shape=jax.ShapeDtypeStruct((N_TOK, S5_WIDTH), F32),
        grid=(S5_GROUPS // S5_YG,),
        in_specs=[pl.BlockSpec((S5_ROWS, kw), lambda g: (0, g)),
                  pl.BlockSpec((2, S5_YG, S5_CW, S5_CW), lambda g: (0, g, 0, 0)),
                  pl.BlockSpec((2, S5_ROWS, S5_YG * S5_SW), lambda g: (0, 0, g)),
                  pl.BlockSpec((2, S5_YG, S5_SW, S5_CW), lambda g: (0, g, 0, 0))],
        out_specs=pl.BlockSpec((N_TOK, lanes), lambda g: (0, g)),
        scratch_shapes=[pltpu.VMEM((S5_ROWS, kw), F32), pltpu.VMEM((kw + S5_CW, lanes), BF16)],
        compiler_params=_params(("arbitrary",)),
        name="s5_chunk_out",
    )(u_chunks, toep, hin, in_mat)


def _s5out_kernel(y_ref, u_ref, d_ref, w_ref, b_ref, o_ref):
    y = y_ref[...] + u_ref[...] * d_ref[...]
    y = y * (0.5 * (1.0 + jnp.tanh(math.sqrt(2.0 / math.pi) * (y + 0.044715 * (y * y * y)))))
    z = _dot(y.astype(BF16), w_ref[...]) + b_ref[...]
    o_ref[...] = (y * _sigmoid(z)).astype(BF16)


def _s5_out(y_scan, proj, s5_d, glu_w_bf, glu_b):
    tm = 512
    return pl.pallas_call(
        _s5out_kernel,
        out_shape=jax.ShapeDtypeStruct((N_TOK, S5_WIDTH), BF16),
        grid=(N_TOK // tm,),
        in_specs=[pl.BlockSpec((tm, S5_WIDTH), lambda i: (i, 0)),
                  pl.BlockSpec((tm, S5_WIDTH), lambda i: (i, 0)),
                  pl.BlockSpec((1, S5_WIDTH), lambda i: (0, 0)),
                  pl.BlockSpec((S5_WIDTH, S5_WIDTH), lambda i: (0, 0)),
                  pl.BlockSpec((1, S5_WIDTH), lambda i: (0, 0))],
        out_specs=pl.BlockSpec((tm, S5_WIDTH), lambda i: (i, 0)),
        compiler_params=_params(("parallel",)),
        name="s5_gelu_glu",
    )(y_scan, proj, s5_d.reshape(1, S5_WIDTH), glu_w_bf, glu_b.reshape(1, S5_WIDTH))


S5_LB = 128
S5_LBG = S5_LB // S5_GROUP


def _s5chunks_kernel(p_ref, o_ref, x_ref, sel_ref):
    kw = S5_CHUNK * S5_LB
    pad = (S5_LBG - 1) * S5_GROUP

    @pl.when(pl.program_id(0) == 0)
    def _():
        k = lax.broadcasted_iota(jnp.int32, sel_ref.shape, 0) - pad
        lane = lax.broadcasted_iota(jnp.int32, sel_ref.shape, 1)
        hit = (k >= 0) & ((k % S5_LB) // S5_GROUP == 0) & (lane == (k // S5_LB) * S5_GROUP + k % S5_GROUP)
        sel_ref[...] = jnp.where(hit, 1.0, 0.0).astype(BF16)

    for n in range(S5_NCHUNK):
        for i in range(S5_CHUNK):
            rows = p_ref[pl.ds(n * S5_CHUNK + i, S5_NSEQ, stride=S5_SEQ), :]
            x_ref[n * S5_NSEQ:(n + 1) * S5_NSEQ, i * S5_LB:(i + 1) * S5_LB] = rows.astype(BF16)
    x = x_ref[...]
    for gl in range(S5_LBG):
        off = pad - gl * S5_GROUP
        o_ref[:, gl * S5_CW:(gl + 1) * S5_CW] = _dot(x, sel_ref[off:off + kw, :]).astype(BF16)


def _s5_chunks(proj):
    kw = S5_CHUNK * S5_LB
    return pl.pallas_call(
        _s5chunks_kernel,
        out_shape=jax.ShapeDtypeStruct((S5_ROWS, S5_GROUPS * S5_CW), BF16),
        grid=(S5_WIDTH // S5_LB,),
        in_specs=[pl.BlockSpec((N_TOK, S5_LB), lambda b: (0, b))],
        out_specs=pl.BlockSpec((S5_ROWS, kw), lambda b: (0, b)),
        scratch_shapes=[pltpu.VMEM((S5_ROWS, kw), BF16), pltpu.VMEM((kw + S5_LB, S5_CW), BF16)],
        compiler_params=_params(("arbitrary",)),
        name="s5_chunk_layout",
    )(proj)


def _s5_mixer(proj, state_re, state_im, prep, s5_d, glu_w_bf, glu_b):
    toep, s_mat, in_mat, coef = prep
    u_chunks = _s5_chunks(proj)
    z = _s5_z(u_chunks, s_mat)
    coef2 = coef.transpose(0, 2, 1, 3).reshape(2, 8, S5_GROUPS * S5_SW)
    h0 = jnp.concatenate([state_re, state_im], axis=-1)
    h0 = h0.transpose(1, 0, 2, 3).reshape(2, DEC_BATCH, S5_GROUPS * S5_SW)
    hin, ends = _s5_scan(z, coef2, h0)
    y_tok = _s5_y(u_chunks, toep, hin, in_mat)
    out = _s5_out(y_tok, proj, s5_d, glu_w_bf, glu_b)
    fin = ends[:, :BATCH].reshape(2, BATCH, S5_GROUPS, S5_SW).transpose(1, 0, 2, 3)
    return out, fin[..., :S5_STATE], fin[..., S5_STATE:]


def _softmax_rows(parts):
    m = parts[0].max(axis=-1, keepdims=True)
    for s in parts[1:]:
        m = jnp.maximum(m, s.max(axis=-1, keepdims=True))
    ps = [jnp.exp(s - m) for s in parts]
    tot = ps[0].sum(axis=-1, keepdims=True)
    for p in ps[1:]:
        tot = tot + p.sum(axis=-1, keepdims=True)
    inv = 1.0 / tot
    return [(p * inv).astype(BF16) for p in ps]


def _ctxattn_kernel(q_ref, k_ref, v_ref, o_ref, nk_ref, nv_ref):
    for h in range(NA_HEADS):
        sl = slice(h * NA_HEAD_DIM, (h + 1) * NA_HEAD_DIM)
        k = k_ref[:, sl]
        v = v_ref[:, sl]
        nk_ref[h] = k
        nv_ref[h] = v
        q = (q_ref[:, sl] * QK_SCALE).astype(BF16)
        (p,) = _softmax_rows([_dot_nt(q, k.astype(BF16))])
        o_ref[:, sl] = _dot(p, v.astype(BF16)).astype(BF16)


def _ctx_attention(proj):
    col = lambda c: pl.BlockSpec((SEQ, NA_WIDTH), lambda b: (b, c))
    cache = jax.ShapeDtypeStruct((BATCH, 1, NA_HEADS, SEQ, NA_HEAD_DIM), F32)
    cache_spec = pl.BlockSpec((None, None, NA_HEADS, SEQ, NA_HEAD_DIM), lambda b: (b, 0, 0, 0, 0))
    return pl.pallas_call(
        _ctxattn_kernel,
        out_shape=(jax.ShapeDtypeStruct((N_PROMPT, NA_WIDTH), BF16), cache, cache),
        grid=(BATCH,),
        in_specs=[col(1), col(2), col(3)],
        out_specs=(pl.BlockSpec((SEQ, NA_WIDTH), lambda b: (b, 0)), cache_spec, cache_spec),
        compiler_params=_params(("parallel",)),
        name="ctx_attention",
    )(proj, proj, proj)


def _na_geometry(r0):
    ks = min(max(r0 - NA_WIN_R // 2, 0), GRID_ROWS - NA_KROWS)
    tiles = []
    for a in range(NA_QROWS):
        rq = r0 + a
        rs = min(max(rq - NA_WIN_R // 2, 0), GRID_ROWS - NA_WIN_R)
        row = []
        for rl in range(NA_KROWS):
            rk = ks + rl
            row.append(rk - rq + NA_WIN_R - 1 if rs <= rk < rs + NA_WIN_R else None)
        tiles.append(row)
    return ks, tiles


NA_VARIANT_ROW0 = (0, NA_QROWS, GRID_ROWS - NA_QROWS)


def _nabias_kernel(rpb_ref, o_ref, t_ref):
    h = pl.program_id(0)
    shape = (GRID_W, 2 * GRID_W)
    cq = lax.broadcasted_iota(jnp.int32, shape, 0)
    lane = lax.broadcasted_iota(jnp.int32, shape, 1)
    ck = lane % GRID_W
    dc = jnp.clip(ck - cq + (NA_WIN_C - 1), 0, NA_NDC - 1)
    c0 = jnp.clip(cq - NA_WIN_C // 2, 0, GRID_W - NA_WIN_C)
    in_cols = (ck >= c0) & (ck < c0 + NA_WIN_C)
    neg = jnp.full(shape, NEG_INF, F32)
    for dr in range(NA_NDR):
        t = neg
        for v in range(NA_NDC):
            t = jnp.where(dc == v, rpb_ref[h, dr * NA_NDC + v], t)
        t_ref[dr] = jnp.where(in_cols, t, neg)
    left = lane < GRID_W
    for var, r0 in enumerate(NA_VARIANT_ROW0):
        _, tiles = _na_geometry(r0)
        for a in range(NA_QROWS):
            for m in range(NA_KROWS // 2):
                dl, dr_ = tiles[a][2 * m], tiles[a][2 * m + 1]
                tl = neg if dl is None else t_ref[dl]
                tr = neg if dr_ is None else t_ref[dr_]
                o_ref[var, a * GRID_W:(a + 1) * GRID_W, m * 2 * GRID_W:(m + 1) * 2 * GRID_W] = (
                    jnp.where(left, tl, tr))


def _na_bias(rpb):
    nq, nk = NA_QROWS * GRID_W, NA_KROWS * GRID_W
    return pl.pallas_call(
        _nabias_kernel,
        out_shape=jax.ShapeDtypeStruct((NA_HEADS, 3, nq, nk), F32),
        grid=(NA_HEADS,),
        in_specs=[pl.BlockSpec(memory_space=pltpu.SMEM)],
        out_specs=pl.BlockSpec((None, 3, nq, nk), lambda h: (h, 0, 0, 0)),
        scratch_shapes=[pltpu.VMEM((NA_NDR, GRID_W, 2 * GRID_W), F32)],
        compiler_params=_params(("parallel",)),
        name="na_bias",
    )(rpb.reshape(NA_HEADS, NA_NDR * NA_NDC))


def _naattn_kernel(q_ref, k_ref, v_ref, kc_ref, vc_ref, bias_ref, o_ref):
    kb = k_ref[...].astype(BF16)
    vb = v_ref[...].astype(BF16)
    kc = kc_ref[...].astype(BF16)
    vc = vc_ref[...].astype(BF16)
    nq = NA_QROWS * GRID_W
    for blk in range(GRID_ROWS // NA_QROWS):
        r0 = blk * NA_QROWS
        var = 0 if blk == 0 else (2 if r0 == NA_VARIANT_ROW0[2] else 1)
        ks, _ = _na_geometry(r0)
        keys = slice(ks * GRID_W, (ks + NA_KROWS) * GRID_W)
        q = (q_ref[r0 * GRID_W:r0 * GRID_W + nq, :] * QK_SCALE).astype(BF16)
        s_loc = _dot_nt(q, kb[keys]) + bias_ref[var]
        s_ctx = _dot_nt(q, kc)
        p_loc, p_ctx = _softmax_rows([s_loc, s_ctx])
        o = _dot(p_loc, vb[keys]) + _dot(p_ctx, vc)
        o_ref[r0 * GRID_W:r0 * GRID_W + nq, :] = o.astype(BF16)


def _na_attention(proj, cache_k, cache_v, bias):
    first = N_PROMPT // DEC_SEQ
    col = lambda c: pl.BlockSpec((DEC_SEQ, NA_HEAD_DIM), lambda b, h: (first + b, c * NA_HEADS + h))
    cache_spec = pl.BlockSpec((None, None, None, PAST_LEN, NA_HEAD_DIM), lambda b, h: (b, 0, h, 0, 0))
    return pl.pallas_call(
        _naattn_kernel,
        out_shape=jax.ShapeDtypeStruct((DEC_BATCH * DEC_SEQ, NA_WIDTH), BF16),
        grid=(DEC_BATCH, NA_HEADS),
        in_specs=[col(1), col(2), col(3), cache_spec, cache_spec,
                  pl.BlockSpec((None, 3, NA_QROWS * GRID_W, NA_KROWS * GRID_W), lambda b, h: (h, 0, 0, 0))],
        out_specs=pl.BlockSpec((DEC_SEQ, NA_HEAD_DIM), lambda b, h: (b, h)),
        compiler_params=_params(("parallel", "parallel")),
        name="na_attention",
    )(proj, proj, proj, cache_k, cache_v, bias)


def _outproj_kernel(a_ref, b_ref, w_ref, x_ref, gate_ref, g_ref, beta_ref, o_ref):
    half = a_ref.shape[1]
    y = _dot(a_ref[...], w_ref[0:half, :]) + _dot(b_ref[...], w_ref[half:2 * half, :])
    o_ref[...] = _post_norm(x_ref[...], y, gate_ref[...], g_ref[...], beta_ref[...])


def _out_proj(y_s5, y_att, w_bf, x, mod, ln_g, ln_b, layer):
    tm = 512
    half = y_s5.shape[1]
    return pl.pallas_call(
        _outproj_kernel,
        out_shape=jax.ShapeDtypeStruct((N_TOK, D_MODEL), F32),
        grid=(N_TOK // tm,),
        in_specs=[pl.BlockSpec((tm, half), lambda i: (i, 0)),
                  pl.BlockSpec((tm, half), lambda i: (i, 0)),
                  pl.BlockSpec((D_MODEL, D_MODEL), lambda i: (0, 0)),
                  pl.BlockSpec((tm, D_MODEL), lambda i: (i, 0)),
                  _mod_spec(layer, 2, tm), _row_spec(layer, 0), _row_spec(layer, 0)],
        out_specs=pl.BlockSpec((tm, D_MODEL), lambda i: (i, 0)),
        compiler_params=_params(("parallel",)),
        name="out_proj_norm",
    )(y_s5, y_att, w_bf, x, mod, ln_g, ln_b)


def _ffn_kernel(x_ref, sh_ref, sc_ref, gate_ref, g_ref, beta_ref, wg_ref, wu_ref, wd_ref, o_ref, h_ref, acc_ref):
    f = pl.program_id(1)

    @pl.when(f == 0)
    def _():
        h_ref[...] = (x_ref[...] * (1.0 + sc_ref[...]) + sh_ref[...]).astype(BF16)
        acc_ref[...] = jnp.zeros_like(acc_ref)

    h = h_ref[...]
    a = _dot(h, wg_ref[...])
    b = _dot(h, wu_ref[...])
    acc_ref[...] += _dot((a * _sigmoid(a) * b).astype(BF16), wd_ref[...])

    @pl.when(f == pl.num_programs(1) - 1)
    def _():
        o_ref[...] = _post_norm(x_ref[...], acc_ref[...], gate_ref[...], g_ref[...], beta_ref[...])


def _ffn(x, mod, ln_g, ln_b, wg_bf, wu_bf, wd_bf, layer):
    tm, tf = 512, 512
    return pl.pallas_call(
        _ffn_kernel,
        out_shape=jax.ShapeDtypeStruct((N_TOK, D_MODEL), F32),
        grid=(N_TOK // tm, FFN_DIM // tf),
        in_specs=[pl.BlockSpec((tm, D_MODEL), lambda i, f: (i, 0)),
                  _mod_spec(layer, 3, tm), _mod_spec(layer, 4, tm), _mod_spec(layer, 5, tm),
                  _row_spec(layer, 1), _row_spec(layer, 1),
                  pl.BlockSpec((D_MODEL, tf), lambda i, f: (0, f)),
                  pl.BlockSpec((D_MODEL, tf), lambda i, f: (0, f)),
                  pl.BlockSpec((tf, D_MODEL), lambda i, f: (f, 0))],
        out_specs=pl.BlockSpec((tm, D_MODEL), lambda i, f: (i, 0)),
        scratch_shapes=[pltpu.VMEM((tm, D_MODEL), BF16), pltpu.VMEM((tm, D_MODEL), F32)],
        compiler_params=_params(("parallel", "arbitrary")),
        name="ffn_norm",
    )(x, mod, mod, mod, ln_g, ln_b, wg_bf, wu_bf, wd_bf)


def _pool_kernel(x_ref, prev_ref, next_ref, sh_ref, sc_ref, gate_ref, g_ref, beta_ref, w_ref, ps_ref,
                 o_ref, ext_ref, y_ref):
    q = pl.program_id(0)
    latent = q >= BATCH
    seg = (q - BATCH) % S5_SEGS
    has_prev = latent & (seg > 0)
    has_next = latent & (seg < S5_SEGS - 1)
    seq_len = jnp.where(latent, DEC_SEQ, SEQ)
    t = jnp.where(latent, seg * SEQ, 0) + lax.broadcasted_iota(jnp.int32, (SEQ, 1), 0)

    scale = 1.0 + sc_ref[...]
    shift = sh_ref[...]
    x = x_ref[...]
    halo = POOL_HALO
    ext_ref[0:halo, :] = jnp.where(has_prev, prev_ref[...] * scale + shift, 0.0)
    ext_ref[halo:halo + SEQ, :] = x * scale + shift
    ext_ref[halo + SEQ:2 * halo + SEQ, :] = jnp.where(has_next, next_ref[...] * scale + shift, 0.0)

    for g, w in enumerate(POOL_SIZES):
        cols = slice(g * POOL_GROUP_DIM, (g + 1) * POOL_GROUP_DIM)
        total = None
        for k in range(-(w // 2), w - w // 2):
            part = ext_ref[halo + k:halo + k + SEQ, cols]
            total = part if total is None else total + part
        count = jnp.minimum(t + (w - w // 2), seq_len) - jnp.maximum(t - w // 2, 0)
        pooled = total / count.astype(F32) - ext_ref[halo:halo + SEQ, cols]
        y_ref[:, cols] = _dot(pooled.astype(BF16), w_ref[g])
    y = y_ref[...] * ps_ref[...]
    o_ref[...] = _post_norm(x, y, gate_ref[...], g_ref[...], beta_ref[...])


def _pool(x, mod, ln_g, ln_b, w_bf, pool_scale, layer):
    tm = SEQ
    nhb = N_TOK // POOL_HALO
    per = tm // POOL_HALO
    return pl.pallas_call(
        _pool_kernel,
        out_shape=jax.ShapeDtypeStruct((N_TOK, D_MODEL), F32),
        grid=(N_TOK // tm,),
        in_specs=[pl.BlockSpec((tm, D_MODEL), lambda i: (i, 0)),
                  pl.BlockSpec((POOL_HALO, D_MODEL), lambda i: (jnp.maximum(i * per - 1, 0), 0)),
                  pl.BlockSpec((POOL_HALO, D_MODEL), lambda i: (jnp.minimum((i + 1) * per, nhb - 1), 0)),
                  _mod_spec(layer, 0, tm), _mod_spec(layer, 1, tm), _mod_spec(layer, 2, tm),
                  _row_spec(layer, 0), _row_spec(layer, 0),
                  pl.BlockSpec((len(POOL_SIZES), POOL_GROUP_DIM, POOL_GROUP_DIM), lambda i: (0, 0, 0)),
                  pl.BlockSpec((1, D_MODEL), lambda i: (0, 0))],
        out_specs=pl.BlockSpec((tm, D_MODEL), lambda i: (i, 0)),
        scratch_shapes=[pltpu.VMEM((tm + 2 * POOL_HALO, D_MODEL), F32), pltpu.VMEM((tm, D_MODEL), F32)],
        compiler_params=_params(("parallel",)),
        name="pool_norm",
    )(x, x, x, mod, mod, mod, ln_g, ln_b, w_bf, pool_scale.reshape(1, D_MODEL))


ROUTER_LANES = 128


def _split_bf16(a):
    hi = a.astype(BF16)
    return hi, (a - hi.astype(F32)).astype(BF16)


def _router_kernel(x_ref, sh_ref, sc_ref, w_ref, b_ref, h_ref, info_ref):
    h = x_ref[...] * (1.0 + sc_ref[...]) + sh_ref[...]
    for c in range(ROW_SLABS):
        h_ref[pl.ds(c, h.shape[0], stride=ROW_SLABS), :] = h[:, c * 128:(c + 1) * 128]
    hh, hl = _split_bf16(h)
    wh, wl = _split_bf16(w_ref[...])
    logits = _dot(hh, wh) + _dot(hl, wh) + _dot(hh, wl) + b_ref[...]
    lane = lax.broadcasted_iota(jnp.int32, logits.shape, 1)
    logits = jnp.where(lane < N_EXPERTS, logits, -jnp.inf)
    m1 = logits.max(axis=-1, keepdims=True)
    i1 = jnp.where(logits == m1, lane, ROUTER_LANES).min(axis=-1, keepdims=True)
    rest = jnp.where(lane == i1, -jnp.inf, logits)
    m2 = rest.max(axis=-1, keepdims=True)
    i2 = jnp.where(rest == m2, lane, ROUTER_LANES).min(axis=-1, keepdims=True)
    e = jnp.exp(m2 - m1)
    g1 = 1.0 / (1.0 + e)
    g2 = e / (1.0 + e)
    info = jnp.where(lane == 0, i1.astype(F32), jnp.where(lane == 1, i2.astype(F32),
                     jnp.where(lane == 2, g1, jnp.where(lane == 3, g2, 0.0))))
    info_ref[...] = info


def _router(x, mod, router_w, router_b, layer):
    tm = 512
    w = jnp.zeros((D_MODEL, ROUTER_LANES), F32).at[:, :N_EXPERTS].set(router_w)
    b = jnp.zeros((1, ROUTER_LANES), F32).at[0, :N_EXPERTS].set(router_b)
    return pl.pallas_call(
        _router_kernel,
        out_shape=(jax.ShapeDtypeStruct((N_TOK * ROW_SLABS, 128), F32),
                   jax.ShapeDtypeStruct((N_TOK, ROUTER_LANES), F32)),
        grid=(N_TOK // tm,),
        in_specs=[pl.BlockSpec((tm, D_MODEL), lambda i: (i, 0)),
                  _mod_spec(layer, 3, tm), _mod_spec(layer, 4, tm),
                  pl.BlockSpec((D_MODEL, ROUTER_LANES), lambda i: (0, 0)),
                  pl.BlockSpec((1, ROUTER_LANES), lambda i: (0, 0))],
        out_specs=(pl.BlockSpec((tm * ROW_SLABS, 128), lambda i: (i, 0)),
                   pl.BlockSpec((tm, ROUTER_LANES), lambda i: (i, 0))),
        compiler_params=_params(("parallel",)),
        name="moe_router",
    )(x, mod, mod, w, b)


def _routing_tables(info):
    experts = info[:, :2].astype(jnp.int32).reshape(-1)
    onehot = (experts[:, None] == jnp.arange(N_EXPERTS)[None, :]).astype(jnp.int32)
    counts = onehot.sum(axis=0)
    rank = ((jnp.cumsum(onehot, axis=0) - onehot) * onehot).sum(axis=1)
    rows = (counts + MOE_TILE - 1) // MOE_TILE * MOE_TILE
    span = (rows + MOE_SPAN - 1) // MOE_SPAN * MOE_SPAN
    ends = jnp.cumsum(span)
    starts = ends - span
    pos = (starts[experts] + rank).astype(jnp.int32)
    src = jnp.zeros((MOE_ROWS,), jnp.int32).at[pos].set(jnp.arange(2 * N_TOK, dtype=jnp.int32) // 2)
    tile_row = jnp.arange(MOE_NTILES, dtype=jnp.int32) * MOE_TILE
    owner = jnp.minimum(jnp.sum(tile_row[:, None] >= ends[None, :], axis=1), N_EXPERTS - 1)
    used = (tile_row < (starts + rows)[owner]) & (tile_row < ends[-1])
    fetch = lax.cummax(jnp.where(used, jnp.arange(MOE_NTILES, dtype=jnp.int32), 0))
    tile_tab = (used.astype(jnp.int32), fetch.astype(jnp.int32), owner[fetch].astype(jnp.int32))
    pos = pos.reshape(N_TOK, 2)
    return src, tile_tab, pos[:, 0], pos[:, 1]


def _row_copy(src_hbm, row, dst, r, sem):
    return pltpu.make_async_copy(src_hbm.at[pl.ds(row, 1), :], dst.at[pl.ds(r, 1), :], sem)


def _dispatch_kernel(src_ref, used_ref, h_hbm, o_ref, buf, sem):
    m = pl.program_id(0)
    used = used_ref[m] == 1

    def token_copy(tok, r):
        return pltpu.make_async_copy(h_hbm.at[pl.ds(tok * ROW_SLABS, ROW_SLABS), :],
                                     buf.at[pl.ds(r * ROW_SLABS, ROW_SLABS), :], sem)

    @pl.when(used)
    def _():
        def issue(r, c):
            token_copy(src_ref[m * MOE_TILE + r], r).start()
            return c

        lax.fori_loop(0, MOE_TILE, issue, 0, unroll=8)

        def drain(r, c):
            token_copy(0, r).wait()
            return c

        lax.fori_loop(0, MOE_TILE, drain, 0, unroll=8)
        for c in range(ROW_SLABS):
            o_ref[:, c * 128:(c + 1) * 128] = buf[pl.ds(c, MOE_TILE, stride=ROW_SLABS), :].astype(BF16)

    @pl.when(jnp.logical_not(used))
    def _():
        o_ref[...] = jnp.zeros_like(o_ref)


def _dispatch(h, src, tile_tab):
    return pl.pallas_call(
        _dispatch_kernel,
        out_shape=jax.ShapeDtypeStruct((MOE_ROWS, D_MODEL), BF16),
        grid_spec=pltpu.PrefetchScalarGridSpec(
            num_scalar_prefetch=2,
            grid=(MOE_NTILES,),
            in_specs=[pl.BlockSpec(memory_space=pl.ANY)],
            out_specs=pl.BlockSpec((MOE_TILE, D_MODEL), lambda m, *_: (m, 0)),
            scratch_shapes=[pltpu.VMEM((MOE_TILE * ROW_SLABS, 128), F32), pltpu.SemaphoreType.DMA(())],
        ),
        compiler_params=_params(("arbitrary",)),
        name="moe_dispatch",
    )(src, tile_tab[0], h)


def _gmm_up_kernel(used_ref, fetch_ref, exp_ref, x_ref, wg_ref, wu_ref, o_ref, wgb_ref, wub_ref):
    j = pl.program_id(1)
    t0 = j * MOE_PAIR
    used = used_ref[t0] == 1
    pair = used_ref[t0 + 1] == 1
    fresh = jnp.logical_or(j == 0, exp_ref[t0] != exp_ref[jnp.maximum(t0 - MOE_PAIR, 0)])

    @pl.when(jnp.logical_and(used, fresh))
    def _():
        wgb_ref[...] = wg_ref[...].astype(BF16)
        wub_ref[...] = wu_ref[...].astype(BF16)

    def swiglu_up(x):
        a = _dot(x, wgb_ref[...])
        b = _dot(x, wub_ref[...])
        return (a * _sigmoid(a) * b).astype(BF16)

    @pl.when(jnp.logical_and(used, pair))
    def _():
        o_ref[...] = swiglu_up(x_ref[...])

    @pl.when(jnp.logical_and(used, jnp.logical_not(pair)))
    def _():
        o_ref[0:MOE_TILE, :] = swiglu_up(x_ref[0:MOE_TILE, :])
        o_ref[MOE_TILE:MOE_SPAN, :] = jnp.zeros((MOE_SPAN - MOE_TILE, o_ref.shape[1]), BF16)

    @pl.when(jnp.logical_not(used))
    def _():
        o_ref[...] = jnp.zeros_like(o_ref)


def _gmm_up(xs, w_gate, w_up, tile_tab):
    tf = 1024
    blk = lambda j, fetch: fetch[j * MOE_PAIR] // MOE_PAIR
    wsp = pl.BlockSpec((None, D_MODEL, tf), lambda f, j, used, fetch, exp: (exp[j * MOE_PAIR], 0, f))
    return pl.pallas_call(
        _gmm_up_kernel,
        out_shape=jax.ShapeDtypeStruct((MOE_ROWS, EXPERT_DIM), BF16),
        grid_spec=pltpu.PrefetchScalarGridSpec(
            num_scalar_prefetch=3,
            grid=(EXPERT_DIM // tf, MOE_NTILES // MOE_PAIR),
            in_specs=[pl.BlockSpec((MOE_SPAN, D_MODEL), lambda f, j, used, fetch, exp: (blk(j, fetch), 0)),
                      wsp, wsp],
            out_specs=pl.BlockSpec((MOE_SPAN, tf), lambda f, j, used, fetch, exp: (j, f)),
            scratch_shapes=[pltpu.VMEM((D_MODEL, tf), BF16), pltpu.VMEM((D_MODEL, tf), BF16)],
        ),
        compiler_params=_params(("parallel", "arbitrary")),
        name="moe_gate_up",
    )(*tile_tab, xs, w_gate, w_up)


def _gmm_down_kernel(used_ref, fetch_ref, exp_ref, x_ref, w_ref, o_ref, wbf_ref):
    j = pl.program_id(1)
    t0 = j * MOE_PAIR
    used = used_ref[t0] == 1
    pair = used_ref[t0 + 1] == 1
    fresh = jnp.logical_or(j == 0, exp_ref[t0] != exp_ref[jnp.maximum(t0 - MOE_PAIR, 0)])

    @pl.when(jnp.logical_and(used, fresh))
    def _():
        wbf_ref[...] = w_ref[...].astype(BF16)

    @pl.when(jnp.logical_and(used, pair))
    def _():
        o_ref[...] = _dot(x_ref[...], wbf_ref[...])

    @pl.when(jnp.logical_and(used, jnp.logical_not(pair)))
    def _():
        o_ref[0:MOE_TILE, :] = _dot(x_ref[0:MOE_TILE, :], wbf_ref[...])
        o_ref[MOE_TILE:MOE_SPAN, :] = jnp.zeros((MOE_SPAN - MOE_TILE, o_ref.shape[1]), F32)

    @pl.when(jnp.logical_not(used))
    def _():
        o_ref[...] = jnp.zeros_like(o_ref)


def _gmm_down(g, w_down, tile_tab):
    tn = 512
    blk = lambda j, fetch: fetch[j * MOE_PAIR] // MOE_PAIR
    return pl.pallas_call(
        _gmm_down_kernel,
        out_shape=jax.ShapeDtypeStruct((MOE_ROWS, D_MODEL), F32),
        grid_spec=pltpu.PrefetchScalarGridSpec(
            num_scalar_prefetch=3,
            grid=(D_MODEL // tn, MOE_NTILES // MOE_PAIR),
            in_specs=[pl.BlockSpec((MOE_SPAN, EXPERT_DIM), lambda n, j, used, fetch, exp: (blk(j, fetch), 0)),
                      pl.BlockSpec((None, EXPERT_DIM, tn), lambda n, j, used, fetch, exp: (exp[j * MOE_PAIR], 0, n))],
            out_specs=pl.BlockSpec((MOE_SPAN, tn), lambda n, j, used, fetch, exp: (j, n)),
            scratch_shapes=[pltpu.VMEM((EXPERT_DIM, tn), BF16)],
        ),
        compiler_params=_params(("parallel", "arbitrary"), vmem=VMEM_LIMIT_MAX),
        name="moe_down",
    )(*tile_tab, g, w_down)


def _combine_kernel(p1_ref, p2_ref, y_hbm, x_ref, info_ref, gate_ref, g_ref, beta_ref, o_ref, b1, b2, sem):
    i = pl.program_id(0)
    tm = x_ref.shape[0]

    def issue(r, c):
        _row_copy(y_hbm, p1_ref[i * tm + r], b1, r, sem).start()
        _row_copy(y_hbm, p2_ref[i * tm + r], b2, r, sem).start()
        return c

    lax.fori_loop(0, tm, issue, 0, unroll=8)

    def drain(r, c):
        _row_copy(y_hbm, 0, b1, r, sem).wait()
        _row_copy(y_hbm, 0, b2, r, sem).wait()
        return c

    lax.fori_loop(0, tm, drain, 0, unroll=8)
    info = info_ref[...]
    y = info[:, 2:3] * b1[...] + info[:, 3:4] * b2[...]
    o_ref[...] = _post_norm(x_ref[...], y, gate_ref[...], g_ref[...], beta_ref[...])


def _combine(y_sorted, pos1, pos2, x, info, mod, ln_g, ln_b, layer):
    tm = 256
    return pl.pallas_call(
        _combine_kernel,
        out_shape=jax.ShapeDtypeStruct((N_TOK, D_MODEL), F32),
        grid_spec=pltpu.PrefetchScalarGridSpec(
            num_scalar_prefetch=2,
            grid=(N_TOK // tm,),
            in_specs=[pl.BlockSpec(memory_space=pl.ANY),
                      pl.BlockSpec((tm, D_MODEL), lambda i, *_: (i, 0)),
                      pl.BlockSpec((tm, ROUTER_LANES), lambda i, *_: (i, 0)),
                      _mod_spec(layer, 5, tm), _row_spec(layer, 1), _row_spec(layer, 1)],
            out_specs=pl.BlockSpec((tm, D_MODEL), lambda i, *_: (i, 0)),
            scratch_shapes=[pltpu.VMEM((tm, D_MODEL), F32), pltpu.VMEM((tm, D_MODEL), F32),
                            pltpu.SemaphoreType.DMA(())],
        ),
        compiler_params=_params(("arbitrary",)),
        name="moe_combine_norm",
    )(pos1, pos2, y_sorted, x, info, mod, ln_g, ln_b)


def _moe(x, mod, ln_g, ln_b, router_w, router_b, w_gate, w_up, w_down, layer):
    h, info = _router(x, mod, router_w, router_b, layer)
    src, tile_tab, pos1, pos2 = _routing_tables(info)
    xs = _dispatch(h, src, tile_tab)
    g = _gmm_up(xs, w_gate, w_up, tile_tab)
    y = _gmm_down(g, w_down, tile_tab)
    return _combine(y, pos1, pos2, x, info, mod, ln_g, ln_b, layer)


def kernel(x_prompt, x_sample, state_s5_re, state_s5_im, cache_k, cache_v, c, c_ctx, ada_w, ada_b, ln_g, ln_b, ab_w_in, ab_w_out, s5_lambda_re, s5_lambda_im, s5_log_dt, s5_b_re, s5_b_im, s5_c_re, s5_c_im, s5_d, s5_glu_w, s5_glu_b, na_rpb, ffn_w_gate, ffn_w_up, ffn_w_down, pool_w, pool_scale, moe_router_w, moe_router_b, moe_w_gate, moe_w_up, moe_w_down):
    x = jnp.concatenate([x_prompt.reshape(N_PROMPT, D_MODEL), x_sample.reshape(-1, D_MODEL)], axis=0)

    cond8 = jnp.zeros((8, D_MODEL), F32).at[0].set(c_ctx).at[1:N_COND].set(c)
    mod = _ada(cond8, ada_w, ada_b)
    mod = mod[:, :N_COND].reshape(DEPTH, N_COND, N_MOD, 1, D_MODEL).transpose(0, 2, 1, 3, 4)
    ln_g4 = ln_g.reshape(DEPTH, 2, 1, D_MODEL)
    ln_b4 = ln_b.reshape(DEPTH, 2, 1, D_MODEL)

    proj = _inproj(x, mod, _cast_bf16(ab_w_in[0]), 0)
    prep = _s5_prep(s5_lambda_re[0], s5_lambda_im[0], s5_log_dt[0], s5_b_re[0], s5_b_im[0],
                    s5_c_re[0], s5_c_im[0])
    y_s5, fin_re, fin_im = _s5_mixer(proj, state_s5_re[:, 0], state_s5_im[:, 0], prep, s5_d[0],
                                     _cast_bf16(s5_glu_w[0]), s5_glu_b[0])
    y_ctx, new_k, new_v = _ctx_attention(proj)
    y_na = _na_attention(proj, cache_k, cache_v, _na_bias(na_rpb[0]))
    y_att = jnp.concatenate([y_ctx, y_na], axis=0)
    x = _out_proj(y_s5, y_att, _cast_bf16(ab_w_out[0]), x, mod, ln_g4, ln_b4, 0)
    x = _ffn(x, mod, ln_g4, ln_b4, _cast_bf16(ffn_w_gate[0]), _cast_bf16(ffn_w_up[0]),
             _cast_bf16(ffn_w_down[0]), 0)

    x = _pool(x, mod, ln_g4, ln_b4, _cast_bf16(pool_w[0]), pool_scale[0], 1)
    x = _moe(x, mod, ln_g4, ln_b4, moe_router_w[0], moe_router_b[0], moe_w_gate[0], moe_w_up[0],
             moe_w_down[0], 1)

    y_prompt = x[:N_PROMPT].reshape(BATCH, SEQ, D_MODEL)
    y_sample = x[N_PROMPT:].reshape(DEC_BATCH, DEC_SEQ, D_MODEL)
    return (y_prompt, y_sample, fin_re[:, None], fin_im[:, None], new_k, new_v)
```

```python
import functools
import math

import jax
import jax.numpy as jnp
from jax import lax
from jax.experimental import pallas as pl
from jax.experimental.pallas import tpu as pltpu

F32 = jnp.float32
BF16 = jnp.bfloat16

D_MODEL = 2048
BATCH = 16
SEQ = 256
DEPTH = 2
DEC_BATCH = 2
DEC_SEQ = 2048
N_MOD = 6
N_PROMPT = BATCH * SEQ
N_TOK = N_PROMPT + DEC_BATCH * DEC_SEQ
N_COND = 1 + DEC_BATCH

S5_WIDTH = 1024
S5_GROUP = 16
S5_GROUPS = 64
S5_STATE = 64
S5_CHUNK = 16
S5_SEQ = 256
S5_NSEQ = N_TOK // S5_SEQ
S5_NCHUNK = S5_SEQ // S5_CHUNK
S5_ROWS = S5_NSEQ * S5_NCHUNK
S5_SEGS = DEC_SEQ // S5_SEQ
S5_CW = S5_CHUNK * S5_GROUP
S5_SW = 2 * S5_STATE

NA_WIDTH = 1024
NA_HEADS = 8
NA_HEAD_DIM = 128
NA_WIN_R = 8
NA_WIN_C = 16
GRID_W = 64
GRID_ROWS = DEC_SEQ // GRID_W
NA_QROWS = 4
NA_KROWS = 12
NA_NDR = 2 * NA_WIN_R - 1
NA_NDC = 2 * NA_WIN_C - 1
PAST_LEN = 256

POOL_SIZES = (2, 4, 8, 16)
POOL_GROUP_DIM = 512
POOL_HALO = 16

FFN_DIM = 5632
N_EXPERTS = 8
EXPERT_DIM = 7168
DMA_UNROLL = 8
MOE_TILE = 256
MOE_PAIR = 2
MOE_SPAN = MOE_PAIR * MOE_TILE
MOE_ROWS = 2 * N_TOK + N_EXPERTS * MOE_SPAN
MOE_NTILES = MOE_ROWS // MOE_TILE

LN_EPS = 1e-5
DEEPNORM_ALPHA = (2.0 * DEPTH) ** 0.25
NEG_INF = -1e30
QK_SCALE = NA_HEAD_DIM ** -0.5

VMEM_LIMIT = 52 * 1024 * 1024
VMEM_LIMIT_MAX = 60 * 1024 * 1024


def _params(sem, vmem=VMEM_LIMIT):
    return pltpu.CompilerParams(dimension_semantics=sem, vmem_limit_bytes=vmem)


def _sigmoid(x):
    return 1.0 / (1.0 + jnp.exp(-x))


def _dot(a, b):
    return jnp.dot(a, b, preferred_element_type=F32)


def _dot_nt(a, b):
    return lax.dot_general(a, b, (((1,), (1,)), ((), ())), preferred_element_type=F32)


def _dot_exact(a, b):
    return jnp.dot(a, b, preferred_element_type=F32, precision=lax.Precision.HIGHEST)


def _cond_of_row(row0):
    return jnp.where(row0 < N_PROMPT, 0, 1 + (row0 - N_PROMPT) // DEC_SEQ)


def _mod_spec(layer, k, tm):
    return pl.BlockSpec((None, None, None, 1, D_MODEL),
                        lambda i, *_: (layer, k, _cond_of_row(i * tm), 0, 0))


def _row_spec(layer_idx, k=None):
    if k is None:
        return pl.BlockSpec((None, 1, D_MODEL), lambda *_: (layer_idx, 0, 0))
    return pl.BlockSpec((None, None, 1, D_MODEL), lambda *_: (layer_idx, k, 0, 0))


def _post_norm(x, y, gate, g, b):
    v = DEEPNORM_ALPHA * x + gate * y
    mu = jnp.mean(v, axis=-1, keepdims=True)
    c = v - mu
    var = jnp.mean(c * c, axis=-1, keepdims=True)
    return c * lax.rsqrt(var + LN_EPS) * g + b


def _ada_kernel(c_ref, w_ref, b_ref, o_ref):
    c = c_ref[...]
    s = (c * _sigmoid(c)).astype(BF16)
    o_ref[...] = _dot(s, w_ref[...].astype(BF16)) + b_ref[...]


def _ada(cond8, ada_w, ada_b):
    tn = 1024
    n_out = N_MOD * D_MODEL
    return pl.pallas_call(
        _ada_kernel,
        out_shape=jax.ShapeDtypeStruct((DEPTH, 8, n_out), F32),
        grid=(DEPTH, n_out // tn),
        in_specs=[pl.BlockSpec((8, D_MODEL), lambda l, n: (0, 0)),
                  pl.BlockSpec((None, D_MODEL, tn), lambda l, n: (l, 0, n)),
                  pl.BlockSpec((None, 1, tn), lambda l, n: (l, 0, n))],
        out_specs=pl.BlockSpec((None, 8, tn), lambda l, n: (l, 0, n)),
        compiler_params=_params(("parallel", "parallel")),
        name="ada_mod",
    )(cond8, ada_w, ada_b.reshape(DEPTH, 1, n_out))


def _cast_kernel(w_ref, o_ref):
    o_ref[...] = w_ref[...].astype(BF16)


def _cast_bf16(w):
    shape = w.shape
    cols = shape[-1]
    rows = w.size // cols
    rb = 8
    while rb * 2 * cols * 4 <= 4 * 1024 * 1024 and rows % (rb * 2) == 0:
        rb *= 2
    out = pl.pallas_call(
        _cast_kernel,
        out_shape=jax.ShapeDtypeStruct((rows, cols), BF16),
        grid=(rows // rb,),
        in_specs=[pl.BlockSpec((rb, cols), lambda i: (i, 0))],
        out_specs=pl.BlockSpec((rb, cols), lambda i: (i, 0)),
        compiler_params=_params(("parallel",)),
        name="cast_bf16",
    )(w.reshape(rows, cols))
    return out.reshape(shape)


def _inproj_kernel(x_ref, sh_ref, sc_ref, w_ref, o_ref, h_ref):
    @pl.when(pl.program_id(1) == 0)
    def _():
        h_ref[...] = (x_ref[...] * (1.0 + sc_ref[...]) + sh_ref[...]).astype(BF16)

    o_ref[...] = _dot(h_ref[...], w_ref[...])


def _inproj(x, mod, w_bf, layer):
    tm, tn = 1024, 1024
    n_out = w_bf.shape[1]
    return pl.pallas_call(
        _inproj_kernel,
        out_shape=jax.ShapeDtypeStruct((N_TOK, n_out), F32),
        grid=(N_TOK // tm, n_out // tn),
        in_specs=[pl.BlockSpec((tm, D_MODEL), lambda i, n: (i, 0)),
                  _mod_spec(layer, 0, tm), _mod_spec(layer, 1, tm),
                  pl.BlockSpec((D_MODEL, tn), lambda i, n: (0, n))],
        out_specs=pl.BlockSpec((tm, tn), lambda i, n: (i, n)),
        scratch_shapes=[pltpu.VMEM((tm, D_MODEL), BF16)],
        compiler_params=_params(("parallel", "arbitrary")),
        name="in_proj",
    )(x, mod, mod, w_bf)


def _cpow_table(re, im, n):
    out = [(jnp.ones_like(re), jnp.zeros_like(im))]
    for _ in range(n):
        pr, pi = out[-1]
        out.append((pr * re - pi * im, pr * im + pi * re))
    return out


def _lam_bar(lam):
    re = jnp.minimum(lam[0], -1e-4)
    im = lam[1]
    dt = jnp.exp(lam[2])
    mag = jnp.exp(re * dt)
    return re, im, mag * jnp.cos(im * dt), mag * jnp.sin(im * dt)


def _s5prep_kernel(lamc_ref, lamr_ref, ct_ref, bt_ref, b12_ref, toep_ref, s_ref, in_ref, co_ref):
    fwd = pl.program_id(0) == 0
    P, G, T = S5_STATE, S5_GROUP, S5_CHUNK

    lc = lamc_ref[...]
    _, _, lbr_c, lbi_c = _lam_bar((lc[:, 0:1], lc[:, 1:2], lc[:, 2:3]))
    pw_c = _cpow_table(lbr_c, lbi_c, T)

    def pick_cols(e_idx, shape):
        pr = jnp.zeros(shape, F32)
        pi = jnp.zeros(shape, F32)
        for e in range(T + 1):
            hit = e_idx == e
            pr = jnp.where(hit, pw_c[e][0], pr)
            pi = jnp.where(hit, pw_c[e][1], pi)
        return pr, pi

    wide = 2 * T * G
    lane = lax.broadcasted_iota(jnp.int32, (G, wide), 1)
    row = lax.broadcasted_iota(jnp.int32, (G, wide), 0)
    expand = (lane % G == row).astype(F32)
    ct_re = _dot_exact(ct_ref[0], expand)
    ct_im = _dot_exact(ct_ref[1], expand)

    blk = lax.broadcasted_iota(jnp.int32, (P, wide), 1) // G
    e_idx = jnp.where(fwd, blk - T, T - blk)
    pr, pi = pick_cols(e_idx, (P, wide))
    r_re = pr * ct_re - pi * ct_im
    r_im = pr * ct_im + pi * ct_re

    lr = lamr_ref[...]
    re_r, im_r, lbr_r, lbi_r = _lam_bar((lr[0:1, :], lr[1:2, :], lr[2:3, :]))
    den = re_r * re_r + im_r * im_r
    cf_re = ((lbr_r - 1.0) * re_r + lbi_r * im_r) / den
    cf_im = (lbi_r * re_r - (lbr_r - 1.0) * im_r) / den
    bb_re = cf_re[:, :P] * bt_ref[0] - cf_im[:, :P] * bt_ref[1]
    bb_im = cf_re[:, :P] * bt_ref[1] + cf_im[:, :P] * bt_ref[0]
    kext = _dot_exact(bb_re, r_re) - _dot_exact(bb_im, r_im)
    for i in range(T):
        off = (T - i) * G
        toep_ref[i * G:(i + 1) * G, :] = kext[:, off:off + T * G].astype(BF16)

    blk2 = lax.broadcasted_iota(jnp.int32, (P, T * G), 1) // G
    e_in = jnp.where(fwd, blk2 + 1, T - blk2)
    qr, qi = pick_cols(e_in, (P, T * G))
    c_re = ct_re[:, :T * G]
    c_im = ct_im[:, :T * G]
    in_ref[0:P, :] = (qr * c_re - qi * c_im).astype(BF16)
    in_ref[P:2 * P, :] = (-(qr * c_im + qi * c_re)).astype(BF16)

    pw_r = _cpow_table(lbr_r, lbi_r, T)
    b1 = b12_ref[0]
    b2 = b12_ref[1]
    for i in range(T):
        wr = jnp.where(fwd, pw_r[T - 1 - i][0], pw_r[i][0])
        wi = jnp.where(fwd, pw_r[T - 1 - i][1], pw_r[i][1])
        sr = wr * cf_re - wi * cf_im
        si = wr * cf_im + wi * cf_re
        s_ref[i * G:(i + 1) * G, :] = (sr * b1 + si * b2).astype(BF16)

    sgn = jnp.where(lax.broadcasted_iota(jnp.int32, (1, S5_SW), 1) < P, -1.0, 1.0)
    r16, i16 = pw_r[T]
    r256, i256 = r16, i16
    for _ in range(4):
        r256, i256 = r256 * r256 - i256 * i256, 2.0 * r256 * i256
    for k, rowv in enumerate((r16, sgn * i16, r256, sgn * i256)):
        co_ref[k:k + 1, :] = rowv
    co_ref[4:8, :] = jnp.zeros((4, S5_SW), F32)


def _s5_prep(lam_re, lam_im, log_dt, b_re, b_im, c_re, c_im):
    P, G = S5_STATE, S5_GROUP
    ldt = jnp.broadcast_to(log_dt[..., None], lam_re.shape)
    lamc = jnp.stack([lam_re, lam_im, ldt], axis=-1)
    dup = lambda a: jnp.concatenate([a, a], axis=-1)
    lamr = jnp.stack([dup(lam_re), dup(lam_im), dup(ldt)], axis=-2)
    ct = jnp.stack([jnp.swapaxes(c_re, -1, -2), jnp.swapaxes(c_im, -1, -2)], axis=2)
    bt_re = jnp.swapaxes(b_re, -1, -2)
    bt_im = jnp.swapaxes(b_im, -1, -2)
    bt = jnp.stack([bt_re, bt_im], axis=2)
    b12 = jnp.stack([jnp.concatenate([bt_re, bt_im], -1),
                     jnp.concatenate([-bt_im, bt_re], -1)], axis=2)
    blk = lambda *s: pl.BlockSpec((None, None) + s, lambda d, g: (d, g) + (0,) * len(s))
    return pl.pallas_call(
        _s5prep_kernel,
        out_shape=(jax.ShapeDtypeStruct((2, S5_GROUPS, S5_CW, S5_CW), BF16),
                   jax.ShapeDtypeStruct((2, S5_GROUPS, S5_CW, S5_SW), BF16),
                   jax.ShapeDtypeStruct((2, S5_GROUPS, S5_SW, S5_CW), BF16),
                   jax.ShapeDtypeStruct((2, S5_GROUPS, 8, S5_SW), F32)),
        grid=(2, S5_GROUPS),
        in_specs=[blk(P, 3), blk(3, 2 * P), blk(2, P, G), blk(2, G, P), blk(2, G, 2 * P)],
        out_specs=(blk(S5_CW, S5_CW), blk(S5_CW, S5_SW), blk(S5_SW, S5_CW), blk(8, S5_SW)),
        compiler_params=_params(("parallel", "parallel")),
        name="s5_prep",
    )(lamc, lamr, ct, bt, b12)


S5_GPS = 4
S5_LT = 1024


def _s5z_kernel(u_ref, s_ref, z_ref):
    for gg in range(S5_GPS):
        ug = u_ref[:, gg * S5_CW:(gg + 1) * S5_CW]
        for d in range(2):
            z_ref[d, :, gg * S5_SW:(gg + 1) * S5_SW] = _dot(ug, s_ref[d, gg])


def _s5_z(u_chunks, s_mat):
    return pl.pallas_call(
        _s5z_kernel,
        out_shape=jax.ShapeDtypeStruct((2, S5_ROWS, S5_GROUPS * S5_SW), F32),
        grid=(S5_GROUPS // S5_GPS,),
        in_specs=[pl.BlockSpec((S5_ROWS, S5_GPS * S5_CW), lambda g: (0, g)),
                  pl.BlockSpec((2, S5_GPS, S5_CW, S5_SW), lambda g: (0, g, 0, 0))],
        out_specs=pl.BlockSpec((2, S5_ROWS, S5_GPS * S5_SW), lambda g: (0, 0, g)),
        compiler_params=_params(("parallel",)),
        name="s5_chunk_state",
    )(u_chunks, s_mat)


def _s5scan_kernel(z_ref, co_ref, h0_ref, hin_ref, e_ref, g_ref):
    ns = S5_LT // S5_SW
    nq = S5_NSEQ
    lanes = [slice(j * S5_SW, (j + 1) * S5_SW) for j in range(ns)]

    def cmul(a1, a2, h):
        return a1 * h + a2 * pltpu.roll(h, S5_STATE, 1)

    for d in range(2):
        a16 = [(co_ref[d, 0:1, l], co_ref[d, 1:2, l]) for l in lanes]
        a256 = [(co_ref[d, 2:3, l], co_ref[d, 3:4, l]) for l in lanes]
        order = list(range(S5_NCHUNK)) if d == 0 else list(range(S5_NCHUNK - 1, -1, -1))

        def run(h, write):
            for n in order:
                rows = slice(n * nq, (n + 1) * nq)
                if write:
                    for j, l in enumerate(lanes):
                        hin_ref[d, rows, l] = h[j]
                h = [cmul(a16[j][0], a16[j][1], h[j]) + z_ref[d, rows, l] for j, l in enumerate(lanes)]
            return h

        end = run([jnp.zeros((nq, S5_SW), F32)] * ns, False)
        for j, l in enumerate(lanes):
            e_ref[d, :, l] = end[j]

        g_ref[0:BATCH, :] = jnp.zeros((BATCH, S5_LT), F32)
        segs = list(range(S5_SEGS)) if d == 0 else list(range(S5_SEGS - 1, -1, -1))
        for j, l in enumerate(lanes):
            for b in range(DEC_BATCH):
                g = jnp.broadcast_to(h0_ref[d, b:b + 1, l], (8, S5_SW))
                for k, s in enumerate(segs):
                    r = BATCH + b * S5_SEGS + s
                    g_ref[r:r + 1, l] = g[0:1, :]
                    if k + 1 < S5_SEGS:
                        g = cmul(a256[j][0], a256[j][1], g) + jnp.broadcast_to(e_ref[d, r:r + 1, l], (8, S5_SW))

        run([g_ref[:, l] for l in lanes], True)


def _s5_scan(z, coef, h0):
    nl = S5_GROUPS * S5_SW
    return pl.pallas_call(
        _s5scan_kernel,
        out_shape=(jax.ShapeDtypeStruct((2, S5_ROWS, nl), F32),
                   jax.ShapeDtypeStruct((2, S5_NSEQ, nl), F32)),
        grid=(nl // S5_LT,),
        in_specs=[pl.BlockSpec((2, S5_ROWS, S5_LT), lambda t: (0, 0, t)),
                  pl.BlockSpec((2, 8, S5_LT), lambda t: (0, 0, t)),
                  pl.BlockSpec((2, DEC_BATCH, S5_LT), lambda t: (0, 0, t))],
        out_specs=(pl.BlockSpec((2, S5_ROWS, S5_LT), lambda t: (0, 0, t)),
                   pl.BlockSpec((2, S5_NSEQ, S5_LT), lambda t: (0, 0, t))),
        scratch_shapes=[pltpu.VMEM((S5_NSEQ, S5_LT), F32)],
        compiler_params=_params(("parallel",)),
        name="s5_scan",
    )(z, coef, h0)


S5_YG = 8


def _s5y_kernel(u_ref, t_ref, hin_ref, in_ref, o_ref, y_ref, sel_ref):
    kw = S5_YG * S5_CW
    lanes = S5_YG * S5_GROUP
    pad = (S5_CHUNK - 1) * S5_GROUP

    @pl.when(pl.program_id(0) == 0)
    def _():
        lane2 = lax.broadcasted_iota(jnp.int32, sel_ref.shape, 1)
        lane = lane2 % lanes
        k = lax.broadcasted_iota(jnp.int32, sel_ref.shape, 0) - pad - (lane2 // lanes) * S5_GROUP
        hit = (k >= 0) & ((k % S5_CW) // S5_GROUP == 0) & (lane == (k // S5_CW) * S5_GROUP + k % S5_GROUP)
        sel_ref[...] = jnp.where(hit, 1.0, 0.0).astype(BF16)

    for gg in range(S5_YG):
        ug = u_ref[:, gg * S5_CW:(gg + 1) * S5_CW]
        acc = None
        for d in range(2):
            h = hin_ref[d, :, gg * S5_SW:(gg + 1) * S5_SW].astype(BF16)
            t = _dot(ug, t_ref[d, gg]) + _dot(h, in_ref[d, gg])
            acc = t if acc is None else acc + t
        y_ref[:, gg * S5_CW:(gg + 1) * S5_CW] = acc

    y = y_ref[...]
    hi, lo = _split_bf16(y)
    for j in range(0, S5_CHUNK, 2):
        off = pad - j * S5_GROUP
        pick = sel_ref[off:off + kw, :]
        yj = _dot(hi, pick) + _dot(lo, pick)
        for n in range(S5_NCHUNK):
            for e in range(2):
                o_ref[pl.ds(n * S5_CHUNK + j + e, S5_NSEQ, stride=S5_SEQ), :] = (
                    yj[n * S5_NSEQ:(n + 1) * S5_NSEQ, e * lanes:(e + 1) * lanes])


def _s5_y(u_chunks, toep, hin, in_mat):
    kw = S5_YG * S5_CW
    lanes = S5_YG * S5_GROUP
    return pl.pallas_call(
        _s5y_kernel,
        out_shape=jax.ShapeDtypeStruct((N_TOK, S5_WIDTH), F32),
        grid=(S5_GROUPS // S5_YG,),
        in_specs=[pl.BlockSpec((S5_ROWS, kw), lambda g: (0, g)),
                  pl.BlockSpec((2, S5_YG, S5_CW, S5_CW), lambda g: (0, g, 0, 0)),
                  pl.BlockSpec((2, S5_ROWS, S5_YG * S5_SW), lambda g: (0, 0, g)),
                  pl.BlockSpec((2, S5_YG, S5_SW, S5_CW), lambda g: (0, g, 0, 0))],
        out_specs=pl.BlockSpec((N_TOK, lanes), lambda g: (0, g)),
        scratch_shapes=[pltpu.VMEM((S5_ROWS, kw), F32), pltpu.VMEM((kw + S5_CW, 2 * lanes), BF16)],
        compiler_params=_params(("arbitrary",)),
        name="s5_chunk_out",
    )(u_chunks, toep, hin, in_mat)


def _s5out_kernel(y_ref, u_ref, d_ref, w_ref, b_ref, o_ref):
    y = y_ref[...] + u_ref[...] * d_ref[...]
    y = y * (0.5 * (1.0 + jnp.tanh(math.sqrt(2.0 / math.pi) * (y + 0.044715 * (y * y * y)))))
    z = _dot(y.astype(BF16), w_ref[...]) + b_ref[...]
    o_ref[...] = (y * _sigmoid(z)).astype(BF16)


def _s5_out(y_scan, proj, s5_d, glu_w_bf, glu_b):
    tm = 512
    return pl.pallas_call(
        _s5out_kernel,
        out_shape=jax.ShapeDtypeStruct((N_TOK, S5_WIDTH), BF16),
        grid=(N_TOK // tm,),
        in_specs=[pl.BlockSpec((tm, S5_WIDTH), lambda i: (i, 0)),
                  pl.BlockSpec((tm, S5_WIDTH), lambda i: (i, 0)),
                  pl.BlockSpec((1, S5_WIDTH), lambda i: (0, 0)),
                  pl.BlockSpec((S5_WIDTH, S5_WIDTH), lambda i: (0, 0)),
                  pl.BlockSpec((1, S5_WIDTH), lambda i: (0, 0))],
        out_specs=pl.BlockSpec((tm, S5_WIDTH), lambda i: (i, 0)),
        compiler_params=_params(("parallel",)),
        name="s5_gelu_glu",
    )(y_scan, proj, s5_d.reshape(1, S5_WIDTH), glu_w_bf, glu_b.reshape(1, S5_WIDTH))


S5_LB = 128
S5_LBG = S5_LB // S5_GROUP


def _s5chunks_kernel(p_ref, o_ref, x_ref, sel_ref):
    kw = S5_CHUNK * S5_LB
    pad = (S5_LBG - 1) * S5_GROUP

    @pl.when(pl.program_id(0) == 0)
    def _():
        k = lax.broadcasted_iota(jnp.int32, sel_ref.shape, 0) - pad
        lane = lax.broadcasted_iota(jnp.int32, sel_ref.shape, 1)
        hit = (k >= 0) & ((k % S5_LB) // S5_GROUP == 0) & (lane == (k // S5_LB) * S5_GROUP + k % S5_GROUP)
        sel_ref[...] = jnp.where(hit, 1.0, 0.0).astype(BF16)

    for n in range(S5_NCHUNK):
        for i in range(S5_CHUNK):
            rows = p_ref[pl.ds(n * S5_CHUNK + i, S5_NSEQ, stride=S5_SEQ), :]
            x_ref[n * S5_NSEQ:(n + 1) * S5_NSEQ, i * S5_LB:(i + 1) * S5_LB] = rows.astype(BF16)
    x = x_ref[...]
    for gl in range(S5_LBG):
        off = pad - gl * S5_GROUP
        o_ref[:, gl * S5_CW:(gl + 1) * S5_CW] = _dot(x, sel_ref[off:off + kw, :]).astype(BF16)


def _s5_chunks(proj):
    kw = S5_CHUNK * S5_LB
    return pl.pallas_call(
        _s5chunks_kernel,
        out_shape=jax.ShapeDtypeStruct((S5_ROWS, S5_GROUPS * S5_CW), BF16),
        grid=(S5_WIDTH // S5_LB,),
        in_specs=[pl.BlockSpec((N_TOK, S5_LB), lambda b: (0, b))],
        out_specs=pl.BlockSpec((S5_ROWS, kw), lambda b: (0, b)),
        scratch_shapes=[pltpu.VMEM((S5_ROWS, kw), BF16), pltpu.VMEM((kw + S5_LB, S5_CW), BF16)],
        compiler_params=_params(("arbitrary",)),
        name="s5_chunk_layout",
    )(proj)


def _s5_mixer(proj, state_re, state_im, prep, s5_d, glu_w_bf, glu_b):
    toep, s_mat, in_mat, coef = prep
    u_chunks = _s5_chunks(proj)
    z = _s5_z(u_chunks, s_mat)
    coef2 = coef.transpose(0, 2, 1, 3).reshape(2, 8, S5_GROUPS * S5_SW)
    h0 = jnp.concatenate([state_re, state_im], axis=-1)
    h0 = h0.transpose(1, 0, 2, 3).reshape(2, DEC_BATCH, S5_GROUPS * S5_SW)
    hin, ends = _s5_scan(z, coef2, h0)
    y_tok = _s5_y(u_chunks, toep, hin, in_mat)
    out = _s5_out(y_tok, proj, s5_d, glu_w_bf, glu_b)
    fin = ends[:, :BATCH].reshape(2, BATCH, S5_GROUPS, S5_SW).transpose(1, 0, 2, 3)
    return out, fin[..., :S5_STATE], fin[..., S5_STATE:]


def _softmax_rows(parts):
    m = parts[0].max(axis=-1, keepdims=True)
    for s in parts[1:]:
        m = jnp.maximum(m, s.max(axis=-1, keepdims=True))
    ps = [jnp.exp(s - m) for s in parts]
    tot = ps[0].sum(axis=-1, keepdims=True)
    for p in ps[1:]:
        tot = tot + p.sum(axis=-1, keepdims=True)
    inv = 1.0 / tot
    return [(p * inv).astype(BF16) for p in ps]


def _ctxattn_kernel(q_ref, k_ref, v_ref, o_ref, nk_ref, nv_ref):
    for h in range(NA_HEADS):
        sl = slice(h * NA_HEAD_DIM, (h + 1) * NA_HEAD_DIM)
        k = k_ref[:, sl]
        v = v_ref[:, sl]
        nk_ref[h] = k
        nv_ref[h] = v
        q = (q_ref[:, sl] * QK_SCALE).astype(BF16)
        (p,) = _softmax_rows([_dot_nt(q, k.astype(BF16))])
        o_ref[:, sl] = _dot(p, v.astype(BF16)).astype(BF16)


def _ctx_attention(proj):
    col = lambda c: pl.BlockSpec((SEQ, NA_WIDTH), lambda b: (b, c))
    cache = jax.ShapeDtypeStruct((BATCH, 1, NA_HEADS, SEQ, NA_HEAD_DIM), F32)
    cache_spec = pl.BlockSpec((None, None, NA_HEADS, SEQ, NA_HEAD_DIM), lambda b: (b, 0, 0, 0, 0))
    return pl.pallas_call(
        _ctxattn_kernel,
        out_shape=(jax.ShapeDtypeStruct((N_PROMPT, NA_WIDTH), BF16), cache, cache),
        grid=(BATCH,),
        in_specs=[col(1), col(2), col(3)],
        out_specs=(pl.BlockSpec((SEQ, NA_WIDTH), lambda b: (b, 0)), cache_spec, cache_spec),
        compiler_params=_params(("parallel",)),
        name="ctx_attention",
    )(proj, proj, proj)


def _na_geometry(r0):
    ks = min(max(r0 - NA_WIN_R // 2, 0), GRID_ROWS - NA_KROWS)
    tiles = []
    for a in range(NA_QROWS):
        rq = r0 + a
        rs = min(max(rq - NA_WIN_R // 2, 0), GRID_ROWS - NA_WIN_R)
        row = []
        for rl in range(NA_KROWS):
            rk = ks + rl
            row.append(rk - rq + NA_WIN_R - 1 if rs <= rk < rs + NA_WIN_R else None)
        tiles.append(row)
    return ks, tiles


NA_VARIANT_ROW0 = (0, NA_QROWS, GRID_ROWS - NA_QROWS)


def _nabias_kernel(rpb_ref, o_ref, t_ref):
    h = pl.program_id(0)
    shape = (GRID_W, 2 * GRID_W)
    cq = lax.broadcasted_iota(jnp.int32, shape, 0)
    lane = lax.broadcasted_iota(jnp.int32, shape, 1)
    ck = lane % GRID_W
    dc = jnp.clip(ck - cq + (NA_WIN_C - 1), 0, NA_NDC - 1)
    c0 = jnp.clip(cq - NA_WIN_C // 2, 0, GRID_W - NA_WIN_C)
    in_cols = (ck >= c0) & (ck < c0 + NA_WIN_C)
    neg = jnp.full(shape, NEG_INF, F32)
    for dr in range(NA_NDR):
        t = neg
        for v in range(NA_NDC):
            t = jnp.where(dc == v, rpb_ref[h, dr * NA_NDC + v], t)
        t_ref[dr] = jnp.where(in_cols, t, neg)
    left = lane < GRID_W
    for var, r0 in enumerate(NA_VARIANT_ROW0):
        _, tiles = _na_geometry(r0)
        for a in range(NA_QROWS):
            for m in range(NA_KROWS // 2):
                dl, dr_ = tiles[a][2 * m], tiles[a][2 * m + 1]
                tl = neg if dl is None else t_ref[dl]
                tr = neg if dr_ is None else t_ref[dr_]
                o_ref[var, a * GRID_W:(a + 1) * GRID_W, m * 2 * GRID_W:(m + 1) * 2 * GRID_W] = (
                    jnp.where(left, tl, tr))


def _na_bias(rpb):
    nq, nk = NA_QROWS * GRID_W, NA_KROWS * GRID_W
    return pl.pallas_call(
        _nabias_kernel,
        out_shape=jax.ShapeDtypeStruct((NA_HEADS, 3, nq, nk), F32),
        grid=(NA_HEADS,),
        in_specs=[pl.BlockSpec(memory_space=pltpu.SMEM)],
        out_specs=pl.BlockSpec((None, 3, nq, nk), lambda h: (h, 0, 0, 0)),
        scratch_shapes=[pltpu.VMEM((NA_NDR, GRID_W, 2 * GRID_W), F32)],
        compiler_params=_params(("parallel",)),
        name="na_bias",
    )(rpb.reshape(NA_HEADS, NA_NDR * NA_NDC))


def _naattn_kernel(q_ref, k_ref, v_ref, kc_ref, vc_ref, bias_ref, o_ref):
    kb = k_ref[...].astype(BF16)
    vb = v_ref[...].astype(BF16)
    kc = kc_ref[...].astype(BF16)
    vc = vc_ref[...].astype(BF16)
    nq = NA_QROWS * GRID_W
    for blk in range(GRID_ROWS // NA_QROWS):
        r0 = blk * NA_QROWS
        var = 0 if blk == 0 else (2 if r0 == NA_VARIANT_ROW0[2] else 1)
        ks, _ = _na_geometry(r0)
        keys = slice(ks * GRID_W, (ks + NA_KROWS) * GRID_W)
        q = (q_ref[r0 * GRID_W:r0 * GRID_W + nq, :] * QK_SCALE).astype(BF16)
        s_loc = _dot_nt(q, kb[keys]) + bias_ref[var]
        s_ctx = _dot_nt(q, kc)
        p_loc, p_ctx = _softmax_rows([s_loc, s_ctx])
        o = _dot(p_loc, vb[keys]) + _dot(p_ctx, vc)
        o_ref[r0 * GRID_W:r0 * GRID_W + nq, :] = o.astype(BF16)


def _na_attention(proj, cache_k, cache_v, bias):
    first = N_PROMPT // DEC_SEQ
    col = lambda c: pl.BlockSpec((DEC_SEQ, NA_HEAD_DIM), lambda b, h: (first + b, c * NA_HEADS + h))
    cache_spec = pl.BlockSpec((None, None, None, PAST_LEN, NA_HEAD_DIM), lambda b, h: (b, 0, h, 0, 0))
    return pl.pallas_call(
        _naattn_kernel,
        out_shape=jax.ShapeDtypeStruct((DEC_BATCH * DEC_SEQ, NA_WIDTH), BF16),
        grid=(DEC_BATCH, NA_HEADS),
        in_specs=[col(1), col(2), col(3), cache_spec, cache_spec,
                  pl.BlockSpec((None, 3, NA_QROWS * GRID_W, NA_KROWS * GRID_W), lambda b, h: (h, 0, 0, 0))],
        out_specs=pl.BlockSpec((DEC_SEQ, NA_HEAD_DIM), lambda b, h: (b, h)),
        compiler_params=_params(("parallel", "parallel")),
        name="na_attention",
    )(proj, proj, proj, cache_k, cache_v, bias)


def _outproj_kernel(a_ref, b_ref, w_ref, x_ref, gate_ref, g_ref, beta_ref, o_ref):
    half = a_ref.shape[1]
    y = _dot(a_ref[...], w_ref[0:half, :]) + _dot(b_ref[...], w_ref[half:2 * half, :])
    o_ref[...] = _post_norm(x_ref[...], y, gate_ref[...], g_ref[...], beta_ref[...])


def _out_proj(y_s5, y_att, w_bf, x, mod, ln_g, ln_b, layer):
    tm = 512
    half = y_s5.shape[1]
    return pl.pallas_call(
        _outproj_kernel,
        out_shape=jax.ShapeDtypeStruct((N_TOK, D_MODEL), F32),
        grid=(N_TOK // tm,),
        in_specs=[pl.BlockSpec((tm, half), lambda i: (i, 0)),
                  pl.BlockSpec((tm, half), lambda i: (i, 0)),
                  pl.BlockSpec((D_MODEL, D_MODEL), lambda i: (0, 0)),
                  pl.BlockSpec((tm, D_MODEL), lambda i: (i, 0)),
                  _mod_spec(layer, 2, tm), _row_spec(layer, 0), _row_spec(layer, 0)],
        out_specs=pl.BlockSpec((tm, D_MODEL), lambda i: (i, 0)),
        compiler_params=_params(("parallel",)),
        name="out_proj_norm",
    )(y_s5, y_att, w_bf, x, mod, ln_g, ln_b)


def _ffn_kernel(x_ref, sh_ref, sc_ref, gate_ref, g_ref, beta_ref, wg_ref, wu_ref, wd_ref, o_ref, h_ref, acc_ref):
    f = pl.program_id(1)

    @pl.when(f == 0)
    def _():
        h_ref[...] = (x_ref[...] * (1.0 + sc_ref[...]) + sh_ref[...]).astype(BF16)
        acc_ref[...] = jnp.zeros_like(acc_ref)

    h = h_ref[...]
    a = _dot(h, wg_ref[...])
    b = _dot(h, wu_ref[...])
    acc_ref[...] += _dot((a * _sigmoid(a) * b).astype(BF16), wd_ref[...])

    @pl.when(f == pl.num_programs(1) - 1)
    def _():
        o_ref[...] = _post_norm(x_ref[...], acc_ref[...], gate_ref[...], g_ref[...], beta_ref[...])


def _ffn(x, mod, ln_g, ln_b, wg_bf, wu_bf, wd_bf, layer):
    tm, tf = 512, 512
    return pl.pallas_call(
        _ffn_kernel,
        out_shape=jax.ShapeDtypeStruct((N_TOK, D_MODEL), F32),
        grid=(N_TOK // tm, FFN_DIM // tf),
        in_specs=[pl.BlockSpec((tm, D_MODEL), lambda i, f: (i, 0)),
                  _mod_spec(layer, 3, tm), _mod_spec(layer, 4, tm), _mod_spec(layer, 5, tm),
                  _row_spec(layer, 1), _row_spec(layer, 1),
                  pl.BlockSpec((D_MODEL, tf), lambda i, f: (0, f)),
                  pl.BlockSpec((D_MODEL, tf), lambda i, f: (0, f)),
                  pl.BlockSpec((tf, D_MODEL), lambda i, f: (f, 0))],
        out_specs=pl.BlockSpec((tm, D_MODEL), lambda i, f: (i, 0)),
        scratch_shapes=[pltpu.VMEM((tm, D_MODEL), BF16), pltpu.VMEM((tm, D_MODEL), F32)],
        compiler_params=_params(("parallel", "arbitrary")),
        name="ffn_norm",
    )(x, mod, mod, mod, ln_g, ln_b, wg_bf, wu_bf, wd_bf)


def _pool_kernel(x_ref, prev_ref, next_ref, sh_ref, sc_ref, gate_ref, g_ref, beta_ref, w_ref, ps_ref,
                 o_ref, ext_ref, y_ref):
    q = pl.program_id(0)
    latent = q >= BATCH
    seg = (q - BATCH) % S5_SEGS
    has_prev = latent & (seg > 0)
    has_next = latent & (seg < S5_SEGS - 1)
    seq_len = jnp.where(latent, DEC_SEQ, SEQ)
    t = jnp.where(latent, seg * SEQ, 0) + lax.broadcasted_iota(jnp.int32, (SEQ, 1), 0)

    scale = 1.0 + sc_ref[...]
    shift = sh_ref[...]
    x = x_ref[...]
    halo = POOL_HALO
    ext_ref[0:halo, :] = jnp.where(has_prev, prev_ref[...] * scale + shift, 0.0)
    ext_ref[halo:halo + SEQ, :] = x * scale + shift
    ext_ref[halo + SEQ:2 * halo + SEQ, :] = jnp.where(has_next, next_ref[...] * scale + shift, 0.0)

    for g, w in enumerate(POOL_SIZES):
        cols = slice(g * POOL_GROUP_DIM, (g + 1) * POOL_GROUP_DIM)
        total = None
        for k in range(-(w // 2), w - w // 2):
            part = ext_ref[halo + k:halo + k + SEQ, cols]
            total = part if total is None else total + part
        count = jnp.minimum(t + (w - w // 2), seq_len) - jnp.maximum(t - w // 2, 0)
        pooled = total / count.astype(F32) - ext_ref[halo:halo + SEQ, cols]
        y_ref[:, cols] = _dot(pooled.astype(BF16), w_ref[g])
    y = y_ref[...] * ps_ref[...]
    o_ref[...] = _post_norm(x, y, gate_ref[...], g_ref[...], beta_ref[...])


def _pool(x, mod, ln_g, ln_b, w_bf, pool_scale, layer):
    tm = SEQ
    nhb = N_TOK // POOL_HALO
    per = tm // POOL_HALO
    return pl.pallas_call(
        _pool_kernel,
        out_shape=jax.ShapeDtypeStruct((N_TOK, D_MODEL), F32),
        grid=(N_TOK // tm,),
        in_specs=[pl.BlockSpec((tm, D_MODEL), lambda i: (i, 0)),
                  pl.BlockSpec((POOL_HALO, D_MODEL), lambda i: (jnp.maximum(i * per - 1, 0), 0)),
                  pl.BlockSpec((POOL_HALO, D_MODEL), lambda i: (jnp.minimum((i + 1) * per, nhb - 1), 0)),
                  _mod_spec(layer, 0, tm), _mod_spec(layer, 1, tm), _mod_spec(layer, 2, tm),
                  _row_spec(layer, 0), _row_spec(layer, 0),
                  pl.BlockSpec((len(POOL_SIZES), POOL_GROUP_DIM, POOL_GROUP_DIM), lambda i: (0, 0, 0)),
                  pl.BlockSpec((1, D_MODEL), lambda i: (0, 0))],
        out_specs=pl.BlockSpec((tm, D_MODEL), lambda i: (i, 0)),
        scratch_shapes=[pltpu.VMEM((tm + 2 * POOL_HALO, D_MODEL), F32), pltpu.VMEM((tm, D_MODEL), F32)],
        compiler_params=_params(("parallel",)),
        name="pool_norm",
    )(x, x, x, mod, mod, mod, ln_g, ln_b, w_bf, pool_scale.reshape(1, D_MODEL))


ROUTER_LANES = 128


def _split_bf16(a):
    hi = a.astype(BF16)
    return hi, (a - hi.astype(F32)).astype(BF16)


def _router_kernel(x_ref, sh_ref, sc_ref, w_ref, b_ref, h_ref, info_ref):
    h = x_ref[...] * (1.0 + sc_ref[...]) + sh_ref[...]
    h_ref[...] = h
    hh, hl = _split_bf16(h)
    wh, wl = _split_bf16(w_ref[...])
    logits = _dot(hh, wh) + _dot(hl, wh) + _dot(hh, wl) + b_ref[...]
    lane = lax.broadcasted_iota(jnp.int32, logits.shape, 1)
    logits = jnp.where(lane < N_EXPERTS, logits, -jnp.inf)
    m1 = logits.max(axis=-1, keepdims=True)
    i1 = jnp.where(logits == m1, lane, ROUTER_LANES).min(axis=-1, keepdims=True)
    rest = jnp.where(lane == i1, -jnp.inf, logits)
    m2 = rest.max(axis=-1, keepdims=True)
    i2 = jnp.where(rest == m2, lane, ROUTER_LANES).min(axis=-1, keepdims=True)
    e = jnp.exp(m2 - m1)
    g1 = 1.0 / (1.0 + e)
    g2 = e / (1.0 + e)
    info = jnp.where(lane == 0, i1.astype(F32), jnp.where(lane == 1, i2.astype(F32),
                     jnp.where(lane == 2, g1, jnp.where(lane == 3, g2, 0.0))))
    info_ref[...] = info


def _router(x, mod, router_w, router_b, layer):
    tm = 512
    w = jnp.zeros((D_MODEL, ROUTER_LANES), F32).at[:, :N_EXPERTS].set(router_w)
    b = jnp.zeros((1, ROUTER_LANES), F32).at[0, :N_EXPERTS].set(router_b)
    return pl.pallas_call(
        _router_kernel,
        out_shape=(jax.ShapeDtypeStruct((N_TOK, D_MODEL), F32),
                   jax.ShapeDtypeStruct((N_TOK, ROUTER_LANES), F32)),
        grid=(N_TOK // tm,),
        in_specs=[pl.BlockSpec((tm, D_MODEL), lambda i: (i, 0)),
                  _mod_spec(layer, 3, tm), _mod_spec(layer, 4, tm),
                  pl.BlockSpec((D_MODEL, ROUTER_LANES), lambda i: (0, 0)),
                  pl.BlockSpec((1, ROUTER_LANES), lambda i: (0, 0))],
        out_specs=(pl.BlockSpec((tm, D_MODEL), lambda i: (i, 0)),
                   pl.BlockSpec((tm, ROUTER_LANES), lambda i: (i, 0))),
        compiler_params=_params(("parallel",)),
        name="moe_router",
    )(x, mod, mod, w, b)


def _routing_tables(info):
    experts = info[:, :2].astype(jnp.int32).reshape(-1)
    onehot = (experts[:, None] == jnp.arange(N_EXPERTS)[None, :]).astype(jnp.int32)
    counts = onehot.sum(axis=0)
    rank = ((jnp.cumsum(onehot, axis=0) - onehot) * onehot).sum(axis=1)
    rows = (counts + MOE_TILE - 1) // MOE_TILE * MOE_TILE
    span = (rows + MOE_SPAN - 1) // MOE_SPAN * MOE_SPAN
    ends = jnp.cumsum(span)
    starts = ends - span
    pos = (starts[experts] + rank).astype(jnp.int32)
    src = jnp.zeros((MOE_ROWS,), jnp.int32).at[pos].set(jnp.arange(2 * N_TOK, dtype=jnp.int32) // 2)
    tile_row = jnp.arange(MOE_NTILES, dtype=jnp.int32) * MOE_TILE
    owner = jnp.minimum(jnp.sum(tile_row[:, None] >= ends[None, :], axis=1), N_EXPERTS - 1)
    used = (tile_row < (starts + rows)[owner]) & (tile_row < ends[-1])
    fetch = lax.cummax(jnp.where(used, jnp.arange(MOE_NTILES, dtype=jnp.int32), 0))
    tile_tab = (used.astype(jnp.int32), fetch.astype(jnp.int32), owner[fetch].astype(jnp.int32))
    pos = pos.reshape(N_TOK, 2)
    return src, tile_tab, pos[:, 0], pos[:, 1]


def _row_copy(src_hbm, row, dst, r, sem):
    return pltpu.make_async_copy(src_hbm.at[pl.ds(row, 1), :], dst.at[pl.ds(r, 1), :], sem)


def _dispatch_kernel(src_ref, used_ref, h_hbm, o_ref, buf, sem):
    m = pl.program_id(0)
    used = used_ref[m] == 1

    @pl.when(used)
    def _():
        def issue(g, c):
            for u in range(DMA_UNROLL):
                r = g * DMA_UNROLL + u
                _row_copy(h_hbm, src_ref[m * MOE_TILE + r], buf, r, sem).start(priority=u % 2)
            return c

        lax.fori_loop(0, MOE_TILE // DMA_UNROLL, issue, 0)

        def drain(r, c):
            _row_copy(h_hbm, 0, buf, r, sem).wait()
            return c

        lax.fori_loop(0, MOE_TILE, drain, 0, unroll=DMA_UNROLL)
        o_ref[...] = buf[...].astype(BF16)

    @pl.when(jnp.logical_not(used))
    def _():
        o_ref[...] = jnp.zeros_like(o_ref)


def _dispatch(h, src, tile_tab):
    return pl.pallas_call(
        _dispatch_kernel,
        out_shape=jax.ShapeDtypeStruct((MOE_ROWS, D_MODEL), BF16),
        grid_spec=pltpu.PrefetchScalarGridSpec(
            num_scalar_prefetch=2,
            grid=(MOE_NTILES,),
            in_specs=[pl.BlockSpec(memory_space=pl.ANY)],
            out_specs=pl.BlockSpec((MOE_TILE, D_MODEL), lambda m, *_: (m, 0)),
            scratch_shapes=[pltpu.VMEM((MOE_TILE, D_MODEL), F32), pltpu.SemaphoreType.DMA(())],
        ),
        compiler_params=_params(("arbitrary",)),
        name="moe_dispatch",
    )(src, tile_tab[0], h)


def _gmm_up_kernel(used_ref, fetch_ref, exp_ref, x_ref, wg_ref, wu_ref, o_ref, wgb_ref, wub_ref):
    j = pl.program_id(1)
    t0 = j * MOE_PAIR
    used = used_ref[t0] == 1
    pair = used_ref[t0 + 1] == 1
    fresh = jnp.logical_or(j == 0, exp_ref[t0] != exp_ref[jnp.maximum(t0 - MOE_PAIR, 0)])

    @pl.when(jnp.logical_and(used, fresh))
    def _():
        wgb_ref[...] = wg_ref[...].astype(BF16)
        wub_ref[...] = wu_ref[...].astype(BF16)

    def swiglu_up(x):
        a = _dot(x, wgb_ref[...])
        b = _dot(x, wub_ref[...])
        return (a * _sigmoid(a) * b).astype(BF16)

    @pl.when(jnp.logical_and(used, pair))
    def _():
        o_ref[...] = swiglu_up(x_ref[...])

    @pl.when(jnp.logical_and(used, jnp.logical_not(pair)))
    def _():
        o_ref[0:MOE_TILE, :] = swiglu_up(x_ref[0:MOE_TILE, :])
        o_ref[MOE_TILE:MOE_SPAN, :] = jnp.zeros((MOE_SPAN - MOE_TILE, o_ref.shape[1]), BF16)

    @pl.when(jnp.logical_not(used))
    def _():
        o_ref[...] = jnp.zeros_like(o_ref)


def _gmm_up(xs, w_gate, w_up, tile_tab):
    tf = 1024
    blk = lambda j, fetch: fetch[j * MOE_PAIR] // MOE_PAIR
    wsp = pl.BlockSpec((None, D_MODEL, tf), lambda f, j, used, fetch, exp: (exp[j * MOE_PAIR], 0, f))
    return pl.pallas_call(
        _gmm_up_kernel,
        out_shape=jax.ShapeDtypeStruct((MOE_ROWS, EXPERT_DIM), BF16),
        grid_spec=pltpu.PrefetchScalarGridSpec(
            num_scalar_prefetch=3,
            grid=(EXPERT_DIM // tf, MOE_NTILES // MOE_PAIR),
            in_specs=[pl.BlockSpec((MOE_SPAN, D_MODEL), lambda f, j, used, fetch, exp: (blk(j, fetch), 0)),
                      wsp, wsp],
            out_specs=pl.BlockSpec((MOE_SPAN, tf), lambda f, j, used, fetch, exp: (j, f)),
            scratch_shapes=[pltpu.VMEM((D_MODEL, tf), BF16), pltpu.VMEM((D_MODEL, tf), BF16)],
        ),
        compiler_params=_params(("parallel", "arbitrary")),
        name="moe_gate_up",
    )(*tile_tab, xs, w_gate, w_up)


def _gmm_down_kernel(used_ref, fetch_ref, exp_ref, x_ref, w_ref, o_ref, wbf_ref):
    j = pl.program_id(1)
    t0 = j * MOE_PAIR
    used = used_ref[t0] == 1
    pair = used_ref[t0 + 1] == 1
    fresh = jnp.logical_or(j == 0, exp_ref[t0] != exp_ref[jnp.maximum(t0 - MOE_PAIR, 0)])

    @pl.when(jnp.logical_and(used, fresh))
    def _():
        wbf_ref[...] = w_ref[...].astype(BF16)

    @pl.when(jnp.logical_and(used, pair))
    def _():
        o_ref[...] = _dot(x_ref[...], wbf_ref[...])

    @pl.when(jnp.logical_and(used, jnp.logical_not(pair)))
    def _():
        o_ref[0:MOE_TILE, :] = _dot(x_ref[0:MOE_TILE, :], wbf_ref[...])
        o_ref[MOE_TILE:MOE_SPAN, :] = jnp.zeros((MOE_SPAN - MOE_TILE, o_ref.shape[1]), F32)

    @pl.when(jnp.logical_not(used))
    def _():
        o_ref[...] = jnp.zeros_like(o_ref)


def _gmm_down(g, w_down, tile_tab):
    tn = 512
    blk = lambda j, fetch: fetch[j * MOE_PAIR] // MOE_PAIR
    return pl.pallas_call(
        _gmm_down_kernel,
        out_shape=jax.ShapeDtypeStruct((MOE_ROWS, D_MODEL), F32),
        grid_spec=pltpu.PrefetchScalarGridSpec(
            num_scalar_prefetch=3,
            grid=(D_MODEL // tn, MOE_NTILES // MOE_PAIR),
            in_specs=[pl.BlockSpec((MOE_SPAN, EXPERT_DIM), lambda n, j, used, fetch, exp: (blk(j, fetch), 0)),
                      pl.BlockSpec((None, EXPERT_DIM, tn), lambda n, j, used, fetch, exp: (exp[j * MOE_PAIR], 0, n))],
            out_specs=pl.BlockSpec((MOE_SPAN, tn), lambda n, j, used, fetch, exp: (j, n)),
            scratch_shapes=[pltpu.VMEM((EXPERT_DIM, tn), BF16)],
        ),
        compiler_params=_params(("parallel", "arbitrary"), vmem=VMEM_LIMIT_MAX),
        name="moe_down",
    )(*tile_tab, g, w_down)


def _combine_kernel(p1_ref, p2_ref, y_hbm, x_ref, info_ref, gate_ref, g_ref, beta_ref, o_ref, b1, b2, sem):
    i = pl.program_id(0)
    tm = x_ref.shape[0]

    def issue(g, c):
        for u in range(DMA_UNROLL):
            r = g * DMA_UNROLL + u
            _row_copy(y_hbm, p1_ref[i * tm + r], b1, r, sem).start(priority=0)
            _row_copy(y_hbm, p2_ref[i * tm + r], b2, r, sem).start(priority=1)
        return c

    lax.fori_loop(0, tm // DMA_UNROLL, issue, 0)

    def drain(r, c):
        _row_copy(y_hbm, 0, b1, r, sem).wait()
        _row_copy(y_hbm, 0, b2, r, sem).wait()
        return c

    lax.fori_loop(0, tm, drain, 0, unroll=8)
    info = info_ref[...]
    y = info[:, 2:3] * b1[...] + info[:, 3:4] * b2[...]
    o_ref[...] = _post_norm(x_ref[...], y, gate_ref[...], g_ref[...], beta_ref[...])


def _combine(y_sorted, pos1, pos2, x, info, mod, ln_g, ln_b, layer):
    tm = 256
    return pl.pallas_call(
        _combine_kernel,
        out_shape=jax.ShapeDtypeStruct((N_TOK, D_MODEL), F32),
        grid_spec=pltpu.PrefetchScalarGridSpec(
            num_scalar_prefetch=2,
            grid=(N_TOK // tm,),
            in_specs=[pl.BlockSpec(memory_space=pl.ANY),
                      pl.BlockSpec((tm, D_MODEL), lambda i, *_: (i, 0)),
                      pl.BlockSpec((tm, ROUTER_LANES), lambda i, *_: (i, 0)),
                      _mod_spec(layer, 5, tm), _row_spec(layer, 1), _row_spec(layer, 1)],
            out_specs=pl.BlockSpec((tm, D_MODEL), lambda i, *_: (i, 0)),
            scratch_shapes=[pltpu.VMEM((tm, D_MODEL), F32), pltpu.VMEM((tm, D_MODEL), F32),
                            pltpu.SemaphoreType.DMA(())],
        ),
        compiler_params=_params(("arbitrary",)),
        name="moe_combine_norm",
    )(pos1, pos2, y_sorted, x, info, mod, ln_g, ln_b)


def _moe(x, mod, ln_g, ln_b, router_w, router_b, w_gate, w_up, w_down, layer):
    h, info = _router(x, mod, router_w, router_b, layer)
    src, tile_tab, pos1, pos2 = _routing_tables(info)
    xs = _dispatch(h, src, tile_tab)
    g = _gmm_up(xs, w_gate, w_up, tile_tab)
    y = _gmm_down(g, w_down, tile_tab)
    return _combine(y, pos1, pos2, x, info, mod, ln_g, ln_b, layer)


def kernel(x_prompt, x_sample, state_s5_re, state_s5_im, cache_k, cache_v, c, c_ctx, ada_w, ada_b, ln_g, ln_b, ab_w_in, ab_w_out, s5_lambda_re, s5_lambda_im, s5_log_dt, s5_b_re, s5_b_im, s5_c_re, s5_c_im, s5_d, s5_glu_w, s5_glu_b, na_rpb, ffn_w_gate, ffn_w_up, ffn_w_down, pool_w, pool_scale, moe_router_w, moe_router_b, moe_w_gate, moe_w_up, moe_w_down):
    x = jnp.concatenate([x_prompt.reshape(N_PROMPT, D_MODEL), x_sample.reshape(-1, D_MODEL)], axis=0)

    cond8 = jnp.zeros((8, D_MODEL), F32).at[0].set(c_ctx).at[1:N_COND].set(c)
    mod = _ada(cond8, ada_w, ada_b)
    mod = mod[:, :N_COND].reshape(DEPTH, N_COND, N_MOD, 1, D_MODEL).transpose(0, 2, 1, 3, 4)
    ln_g4 = ln_g.reshape(DEPTH, 2, 1, D_MODEL)
    ln_b4 = ln_b.reshape(DEPTH, 2, 1, D_MODEL)

    proj = _inproj(x, mod, _cast_bf16(ab_w_in[0]), 0)
    prep = _s5_prep(s5_lambda_re[0], s5_lambda_im[0], s5_log_dt[0], s5_b_re[0], s5_b_im[0],
                    s5_c_re[0], s5_c_im[0])
    y_s5, fin_re, fin_im = _s5_mixer(proj, state_s5_re[:, 0], state_s5_im[:, 0], prep, s5_d[0],
                                     _cast_bf16(s5_glu_w[0]), s5_glu_b[0])
    y_ctx, new_k, new_v = _ctx_attention(proj)
    y_na = _na_attention(proj, cache_k, cache_v, _na_bias(na_rpb[0]))
    y_att = jnp.concatenate([y_ctx, y_na], axis=0)
    x = _out_proj(y_s5, y_att, _cast_bf16(ab_w_out[0]), x, mod, ln_g4, ln_b4, 0)
    x = _ffn(x, mod, ln_g4, ln_b4, _cast_bf16(ffn_w_gate[0]), _cast_bf16(ffn_w_up[0]),
             _cast_bf16(ffn_w_down[0]), 0)

    x = _pool(x, mod, ln_g4, ln_b4, _cast_bf16(pool_w[0]), pool_scale[0], 1)
    x = _moe(x, mod, ln_g4, ln_b4, moe_router_w[0], moe_router_b[0], moe_w_gate[0], moe_w_up[0],
             moe_w_down[0], 1)

    y_prompt = x[:N_PROMPT].reshape(BATCH, SEQ, D_MODEL)
    y_sample = x[N_PROMPT:].reshape(DEC_BATCH, DEC_SEQ, D_MODEL)
    return (y_prompt, y_sample, fin_re[:, None], fin_im[:, None], new_k, new_v)
```

```python
import functools
import math

import jax
import jax.numpy as jnp
from jax import lax
from jax.experimental import pallas as pl
from jax.experimental.pallas import tpu as pltpu

F32 = jnp.float32
BF16 = jnp.bfloat16

D_MODEL = 2048
BATCH = 16
SEQ = 256
DEPTH = 2
DEC_BATCH = 2
DEC_SEQ = 2048
N_MOD = 6
N_PROMPT = BATCH * SEQ
N_TOK = N_PROMPT + DEC_BATCH * DEC_SEQ
N_COND = 1 + DEC_BATCH

S5_WIDTH = 1024
S5_GROUP = 16
S5_GROUPS = 64
S5_STATE = 64
S5_CHUNK = 16
S5_SEQ = 256
S5_NSEQ = N_TOK // S5_SEQ
S5_NCHUNK = S5_SEQ // S5_CHUNK
S5_ROWS = S5_NSEQ * S5_NCHUNK
S5_SEGS = DEC_SEQ // S5_SEQ
S5_CW = S5_CHUNK * S5_GROUP
S5_SW = 2 * S5_STATE

NA_WIDTH = 1024
NA_HEADS = 8
NA_HEAD_DIM = 128
NA_WIN_R = 8
NA_WIN_C = 16
GRID_W = 64
GRID_ROWS = DEC_SEQ // GRID_W
NA_QROWS = 4
NA_KROWS = 12
NA_NDR = 2 * NA_WIN_R - 1
NA_NDC = 2 * NA_WIN_C - 1
PAST_LEN = 256

POOL_SIZES = (2, 4, 8, 16)
POOL_GROUP_DIM = 512
POOL_HALO = 16

FFN_DIM = 5632
N_EXPERTS = 8
EXPERT_DIM = 7168
DMA_UNROLL = 8
MOE_TILE = 256
MOE_PAIR = 2
MOE_SPAN = MOE_PAIR * MOE_TILE
MOE_ROWS = 2 * N_TOK + N_EXPERTS * MOE_SPAN
MOE_NTILES = MOE_ROWS // MOE_TILE

LN_EPS = 1e-5
DEEPNORM_ALPHA = (2.0 * DEPTH) ** 0.25
NEG_INF = -1e30
QK_SCALE = NA_HEAD_DIM ** -0.5

VMEM_LIMIT = 52 * 1024 * 1024
VMEM_LIMIT_MAX = 60 * 1024 * 1024


def _params(sem, vmem=VMEM_LIMIT):
    return pltpu.CompilerParams(dimension_semantics=sem, vmem_limit_bytes=vmem)


def _sigmoid(x):
    return 1.0 / (1.0 + jnp.exp(-x))


def _dot(a, b):
    return jnp.dot(a, b, preferred_element_type=F32)


def _dot_nt(a, b):
    return lax.dot_general(a, b, (((1,), (1,)), ((), ())), preferred_element_type=F32)


def _dot_exact(a, b):
    return jnp.dot(a, b, preferred_element_type=F32, precision=lax.Precision.HIGHEST)


def _cond_of_row(row0):
    return jnp.where(row0 < N_PROMPT, 0, 1 + (row0 - N_PROMPT) // DEC_SEQ)


def _mod_spec(layer, k, tm):
    return pl.BlockSpec((None, None, None, 1, D_MODEL),
                        lambda i, *_: (layer, k, _cond_of_row(i * tm), 0, 0))


def _row_spec(layer_idx, k=None):
    if k is None:
        return pl.BlockSpec((None, 1, D_MODEL), lambda *_: (layer_idx, 0, 0))
    return pl.BlockSpec((None, None, 1, D_MODEL), lambda *_: (layer_idx, k, 0, 0))


def _post_norm(x, y, gate, g, b):
    v = DEEPNORM_ALPHA * x + gate * y
    mu = jnp.mean(v, axis=-1, keepdims=True)
    c = v - mu
    var = jnp.mean(c * c, axis=-1, keepdims=True)
    return c * lax.rsqrt(var + LN_EPS) * g + b


def _ada_kernel(c_ref, w_ref, b_ref, o_ref):
    c = c_ref[...]
    s = (c * _sigmoid(c)).astype(BF16)
    o_ref[...] = _dot(s, w_ref[...].astype(BF16)) + b_ref[...]


def _ada(cond8, ada_w, ada_b):
    tn = 1024
    n_out = N_MOD * D_MODEL
    return pl.pallas_call(
        _ada_kernel,
        out_shape=jax.ShapeDtypeStruct((DEPTH, 8, n_out), F32),
        grid=(DEPTH, n_out // tn),
        in_specs=[pl.BlockSpec((8, D_MODEL), lambda l, n: (0, 0)),
                  pl.BlockSpec((None, D_MODEL, tn), lambda l, n: (l, 0, n)),
                  pl.BlockSpec((None, 1, tn), lambda l, n: (l, 0, n))],
        out_specs=pl.BlockSpec((None, 8, tn), lambda l, n: (l, 0, n)),
        compiler_params=_params(("parallel", "parallel")),
        name="ada_mod",
    )(cond8, ada_w, ada_b.reshape(DEPTH, 1, n_out))


def _cast_kernel(w_ref, o_ref):
    o_ref[...] = w_ref[...].astype(BF16)


def _cast_bf16(w):
    shape = w.shape
    cols = shape[-1]
    rows = w.size // cols
    rb = 8
    while rb * 2 * cols * 4 <= 4 * 1024 * 1024 and rows % (rb * 2) == 0:
        rb *= 2
    out = pl.pallas_call(
        _cast_kernel,
        out_shape=jax.ShapeDtypeStruct((rows, cols), BF16),
        grid=(rows // rb,),
        in_specs=[pl.BlockSpec((rb, cols), lambda i: (i, 0))],
        out_specs=pl.BlockSpec((rb, cols), lambda i: (i, 0)),
        compiler_params=_params(("parallel",)),
        name="cast_bf16",
    )(w.reshape(rows, cols))
    return out.reshape(shape)


def _inproj_kernel(x_ref, sh_ref, sc_ref, w_ref, o_ref, h_ref):
    @pl.when(pl.program_id(1) == 0)
    def _():
        h_ref[...] = (x_ref[...] * (1.0 + sc_ref[...]) + sh_ref[...]).astype(BF16)

    o_ref[...] = _dot(h_ref[...], w_ref[...])


def _inproj(x, mod, w_bf, layer):
    tm, tn = 1024, 1024
    n_out = w_bf.shape[1]
    return pl.pallas_call(
        _inproj_kernel,
        out_shape=jax.ShapeDtypeStruct((N_TOK, n_out), F32),
        grid=(N_TOK // tm, n_out // tn),
        in_specs=[pl.BlockSpec((tm, D_MODEL), lambda i, n: (i, 0)),
                  _mod_spec(layer, 0, tm), _mod_spec(layer, 1, tm),
                  pl.BlockSpec((D_MODEL, tn), lambda i, n: (0, n))],
        out_specs=pl.BlockSpec((tm, tn), lambda i, n: (i, n)),
        scratch_shapes=[pltpu.VMEM((tm, D_MODEL), BF16)],
        compiler_params=_params(("parallel", "arbitrary")),
        name="in_proj",
    )(x, mod, mod, w_bf)


def _cpow_table(re, im, n):
    out = [(jnp.ones_like(re), jnp.zeros_like(im))]
    for _ in range(n):
        pr, pi = out[-1]
        out.append((pr * re - pi * im, pr * im + pi * re))
    return out


def _lam_bar(lam):
    re = jnp.minimum(lam[0], -1e-4)
    im = lam[1]
    dt = jnp.exp(lam[2])
    mag = jnp.exp(re * dt)
    return re, im, mag * jnp.cos(im * dt), mag * jnp.sin(im * dt)


def _s5prep_kernel(lamc_ref, lamr_ref, ct_ref, bt_ref, b12_ref, toep_ref, s_ref, in_ref, co_ref):
    fwd = pl.program_id(0) == 0
    P, G, T = S5_STATE, S5_GROUP, S5_CHUNK

    lc = lamc_ref[...]
    _, _, lbr_c, lbi_c = _lam_bar((lc[:, 0:1], lc[:, 1:2], lc[:, 2:3]))
    pw_c = _cpow_table(lbr_c, lbi_c, T)

    half = T // 2
    kblk = lax.broadcasted_iota(jnp.int32, (P, half * G), 1) // G
    zero = jnp.zeros((P, half * G), F32)
    up = [zero, zero]
    down = [zero, zero]
    for k in range(half):
        hit = kblk == k
        up = [jnp.where(hit, pw_c[k][c], up[c]) for c in range(2)]
        down = [jnp.where(hit, pw_c[half - k][c], down[c]) for c in range(2)]

    def cmul(a, b):
        return a[0] * b[0] - a[1] * b[1], a[0] * b[1] + a[1] * b[0]

    def either(a, b):
        return tuple(jnp.where(fwd, a[c], b[c]) for c in range(2))

    def lanes_cat(cols):
        return tuple(jnp.concatenate([col[c] for col in cols], axis=1) for c in range(2))

    none = (zero, zero)
    one0 = (jnp.where(kblk == 0, 1.0, 0.0), zero)
    up8 = cmul(pw_c[half], up)
    down8 = cmul(pw_c[half], down)

    wide = 2 * T * G
    lane = lax.broadcasted_iota(jnp.int32, (G, wide), 1)
    row = lax.broadcasted_iota(jnp.int32, (G, wide), 0)
    expand = (lane % G == row).astype(F32)
    ct_re = _dot_exact(ct_ref[0], expand)
    ct_im = _dot_exact(ct_ref[1], expand)

    pr, pi = lanes_cat([either(none, down8), either(none, down), either(up, one0), either(up8, none)])
    r_re = pr * ct_re - pi * ct_im
    r_im = pr * ct_im + pi * ct_re

    lr = lamr_ref[...]
    re_r, im_r, lbr_r, lbi_r = _lam_bar((lr[0:1, :], lr[1:2, :], lr[2:3, :]))
    den = re_r * re_r + im_r * im_r
    cf_re = ((lbr_r - 1.0) * re_r + lbi_r * im_r) / den
    cf_im = (lbi_r * re_r - (lbr_r - 1.0) * im_r) / den
    bb_re = cf_re[:, :P] * bt_ref[0] - cf_im[:, :P] * bt_ref[1]
    bb_im = cf_re[:, :P] * bt_ref[1] + cf_im[:, :P] * bt_ref[0]
    kext = _dot_exact(bb_re, r_re) - _dot_exact(bb_im, r_im)
    for i in range(T):
        off = (T - i) * G
        toep_ref[i * G:(i + 1) * G, :] = kext[:, off:off + T * G].astype(BF16)

    qr, qi = lanes_cat([either(cmul(pw_c[1], up), down8), either(cmul(pw_c[half + 1], up), down)])
    c_re = ct_re[:, :T * G]
    c_im = ct_im[:, :T * G]
    in_ref[0:P, :] = (qr * c_re - qi * c_im).astype(BF16)
    in_ref[P:2 * P, :] = (-(qr * c_im + qi * c_re)).astype(BF16)

    pw_r = _cpow_table(lbr_r, lbi_r, T)
    b1 = b12_ref[0]
    b2 = b12_ref[1]
    for i in range(T):
        wr = jnp.where(fwd, pw_r[T - 1 - i][0], pw_r[i][0])
        wi = jnp.where(fwd, pw_r[T - 1 - i][1], pw_r[i][1])
        sr = wr * cf_re - wi * cf_im
        si = wr * cf_im + wi * cf_re
        s_ref[i * G:(i + 1) * G, :] = (sr * b1 + si * b2).astype(BF16)

    sgn = jnp.where(lax.broadcasted_iota(jnp.int32, (1, S5_SW), 1) < P, -1.0, 1.0)
    r16, i16 = pw_r[T]
    r256, i256 = r16, i16
    for _ in range(4):
        r256, i256 = r256 * r256 - i256 * i256, 2.0 * r256 * i256
    for k, rowv in enumerate((r16, sgn * i16, r256, sgn * i256)):
        co_ref[k:k + 1, :] = rowv
    co_ref[4:8, :] = jnp.zeros((4, S5_SW), F32)


def _s5_prep(lam_re, lam_im, log_dt, b_re, b_im, c_re, c_im):
    P, G = S5_STATE, S5_GROUP
    ldt = jnp.broadcast_to(log_dt[..., None], lam_re.shape)
    lamc = jnp.stack([lam_re, lam_im, ldt], axis=-1)
    dup = lambda a: jnp.concatenate([a, a], axis=-1)
    lamr = jnp.stack([dup(lam_re), dup(lam_im), dup(ldt)], axis=-2)
    ct = jnp.stack([jnp.swapaxes(c_re, -1, -2), jnp.swapaxes(c_im, -1, -2)], axis=2)
    bt_re = jnp.swapaxes(b_re, -1, -2)
    bt_im = jnp.swapaxes(b_im, -1, -2)
    bt = jnp.stack([bt_re, bt_im], axis=2)
    b12 = jnp.stack([jnp.concatenate([bt_re, bt_im], -1),
                     jnp.concatenate([-bt_im, bt_re], -1)], axis=2)
    blk = lambda *s: pl.BlockSpec((None, None) + s, lambda d, g: (d, g) + (0,) * len(s))
    return pl.pallas_call(
        _s5prep_kernel,
        out_shape=(jax.ShapeDtypeStruct((2, S5_GROUPS, S5_CW, S5_CW), BF16),
                   jax.ShapeDtypeStruct((2, S5_GROUPS, S5_CW, S5_SW), BF16),
                   jax.ShapeDtypeStruct((2, S5_GROUPS, S5_SW, S5_CW), BF16),
                   jax.ShapeDtypeStruct((2, S5_GROUPS, 8, S5_SW), F32)),
        grid=(2, S5_GROUPS),
        in_specs=[blk(P, 3), blk(3, 2 * P), blk(2, P, G), blk(2, G, P), blk(2, G, 2 * P)],
        out_specs=(blk(S5_CW, S5_CW), blk(S5_CW, S5_SW), blk(S5_SW, S5_CW), blk(8, S5_SW)),
        compiler_params=_params(("parallel", "parallel")),
        name="s5_prep",
    )(lamc, lamr, ct, bt, b12)


S5_GPS = 4
S5_LT = 1024


def _s5z_kernel(u_ref, s_ref, z_ref):
    for gg in range(S5_GPS):
        ug = u_ref[:, gg * S5_CW:(gg + 1) * S5_CW]
        for d in range(2):
            z_ref[d, :, gg * S5_SW:(gg + 1) * S5_SW] = _dot(ug, s_ref[d, gg])


def _s5_z(u_chunks, s_mat):
    return pl.pallas_call(
        _s5z_kernel,
        out_shape=jax.ShapeDtypeStruct((2, S5_ROWS, S5_GROUPS * S5_SW), F32),
        grid=(S5_GROUPS // S5_GPS,),
        in_specs=[pl.BlockSpec((S5_ROWS, S5_GPS * S5_CW), lambda g: (0, g)),
                  pl.BlockSpec((2, S5_GPS, S5_CW, S5_SW), lambda g: (0, g, 0, 0))],
        out_specs=pl.BlockSpec((2, S5_ROWS, S5_GPS * S5_SW), lambda g: (0, 0, g)),
        compiler_params=_params(("parallel",)),
        name="s5_chunk_state",
    )(u_chunks, s_mat)


def _s5scan_kernel(z_ref, co_ref, h0_ref, hin_ref, e_ref, g_ref):
    ns = S5_LT // S5_SW
    nq = S5_NSEQ
    lanes = [slice(j * S5_SW, (j + 1) * S5_SW) for j in range(ns)]

    def cmul(a1, a2, h):
        return a1 * h + a2 * pltpu.roll(h, S5_STATE, 1)

    for d in range(2):
        a16 = [(co_ref[d, 0:1, l], co_ref[d, 1:2, l]) for l in lanes]
        a256 = [(co_ref[d, 2:3, l], co_ref[d, 3:4, l]) for l in lanes]
        order = list(range(S5_NCHUNK)) if d == 0 else list(range(S5_NCHUNK - 1, -1, -1))

        def run(h, write):
            for n in order:
                rows = slice(n * nq, (n + 1) * nq)
                if write:
                    for j, l in enumerate(lanes):
                        hin_ref[d, rows, l] = h[j]
                h = [cmul(a16[j][0], a16[j][1], h[j]) + z_ref[d, rows, l] for j, l in enumerate(lanes)]
            return h

        end = run([jnp.zeros((nq, S5_SW), F32)] * ns, False)
        for j, l in enumerate(lanes):
            e_ref[d, :, l] = end[j]

        g_ref[0:BATCH, :] = jnp.zeros((BATCH, S5_LT), F32)
        segs = list(range(S5_SEGS)) if d == 0 else list(range(S5_SEGS - 1, -1, -1))
        for j, l in enumerate(lanes):
            for b in range(DEC_BATCH):
                g = jnp.broadcast_to(h0_ref[d, b:b + 1, l], (8, S5_SW))
                for k, s in enumerate(segs):
                    r = BATCH + b * S5_SEGS + s
                    g_ref[r:r + 1, l] = g[0:1, :]
                    if k + 1 < S5_SEGS:
                        g = cmul(a256[j][0], a256[j][1], g) + jnp.broadcast_to(e_ref[d, r:r + 1, l], (8, S5_SW))

        run([g_ref[:, l] for l in lanes], True)


def _s5_scan(z, coef, h0):
    nl = S5_GROUPS * S5_SW
    return pl.pallas_call(
        _s5scan_kernel,
        out_shape=(jax.ShapeDtypeStruct((2, S5_ROWS, nl), F32),
                   jax.ShapeDtypeStruct((2, S5_NSEQ, nl), F32)),
        grid=(nl // S5_LT,),
        in_specs=[pl.BlockSpec((2, S5_ROWS, S5_LT), lambda t: (0, 0, t)),
                  pl.BlockSpec((2, 8, S5_LT), lambda t: (0, 0, t)),
                  pl.BlockSpec((2, DEC_BATCH, S5_LT), lambda t: (0, 0, t))],
        out_specs=(pl.BlockSpec((2, S5_ROWS, S5_LT), lambda t: (0, 0, t)),
                   pl.BlockSpec((2, S5_NSEQ, S5_LT), lambda t: (0, 0, t))),
        scratch_shapes=[pltpu.VMEM((S5_NSEQ, S5_LT), F32)],
        compiler_params=_params(("parallel",)),
        name="s5_scan",
    )(z, coef, h0)


S5_YG = 8


def _s5y_kernel(u_ref, t_ref, hin_ref, in_ref, o_ref, y_ref, sel_ref):
    kw = S5_YG * S5_CW
    lanes = S5_YG * S5_GROUP
    pad = (S5_CHUNK - 1) * S5_GROUP

    @pl.when(pl.program_id(0) == 0)
    def _():
        lane2 = lax.broadcasted_iota(jnp.int32, sel_ref.shape, 1)
        lane = lane2 % lanes
        k = lax.broadcasted_iota(jnp.int32, sel_ref.shape, 0) - pad - (lane2 // lanes) * S5_GROUP
        hit = (k >= 0) & ((k % S5_CW) // S5_GROUP == 0) & (lane == (k // S5_CW) * S5_GROUP + k % S5_GROUP)
        sel_ref[...] = jnp.where(hit, 1.0, 0.0).astype(BF16)

    for gg in range(S5_YG):
        ug = u_ref[:, gg * S5_CW:(gg + 1) * S5_CW]
        acc = None
        for d in range(2):
            h = hin_ref[d, :, gg * S5_SW:(gg + 1) * S5_SW].astype(BF16)
            t = _dot(ug, t_ref[d, gg]) + _dot(h, in_ref[d, gg])
            acc = t if acc is None else acc + t
        y_ref[:, gg * S5_CW:(gg + 1) * S5_CW] = acc

    y = y_ref[...]
    hi, lo = _split_bf16(y)
    for j in range(0, S5_CHUNK, 2):
        off = pad - j * S5_GROUP
        pick = sel_ref[off:off + kw, :]
        yj = _dot(hi, pick) + _dot(lo, pick)
        for n in range(S5_NCHUNK):
            for e in range(2):
                o_ref[pl.ds(n * S5_CHUNK + j + e, S5_NSEQ, stride=S5_SEQ), :] = (
                    yj[n * S5_NSEQ:(n + 1) * S5_NSEQ, e * lanes:(e + 1) * lanes])


def _s5_y(u_chunks, toep, hin, in_mat):
    kw = S5_YG * S5_CW
    lanes = S5_YG * S5_GROUP
    return pl.pallas_call(
        _s5y_kernel,
        out_shape=jax.ShapeDtypeStruct((N_TOK, S5_WIDTH), F32),
        grid=(S5_GROUPS // S5_YG,),
        in_specs=[pl.BlockSpec((S5_ROWS, kw), lambda g: (0, g)),
                  pl.BlockSpec((2, S5_YG, S5_CW, S5_CW), lambda g: (0, g, 0, 0)),
                  pl.BlockSpec((2, S5_ROWS, S5_YG * S5_SW), lambda g: (0, 0, g)),
                  pl.BlockSpec((2, S5_YG, S5_SW, S5_CW), lambda g: (0, g, 0, 0))],
        out_specs=pl.BlockSpec((N_TOK, lanes), lambda g: (0, g)),
        scratch_shapes=[pltpu.VMEM((S5_ROWS, kw), F32), pltpu.VMEM((kw + S5_CW, 2 * lanes), BF16)],
        compiler_params=_params(("arbitrary",)),
        name="s5_chunk_out",
    )(u_chunks, toep, hin, in_mat)


def _s5out_kernel(y_ref, u_ref, d_ref, w_ref, b_ref, o_ref):
    y = y_ref[...] + u_ref[...] * d_ref[...]
    y = y * (0.5 * (1.0 + jnp.tanh(math.sqrt(2.0 / math.pi) * (y + 0.044715 * (y * y * y)))))
    z = _dot(y.astype(BF16), w_ref[...]) + b_ref[...]
    o_ref[...] = (y * _sigmoid(z)).astype(BF16)


def _s5_out(y_scan, proj, s5_d, glu_w_bf, glu_b):
    tm = 512
    return pl.pallas_call(
        _s5out_kernel,
        out_shape=jax.ShapeDtypeStruct((N_TOK, S5_WIDTH), BF16),
        grid=(N_TOK // tm,),
        in_specs=[pl.BlockSpec((tm, S5_WIDTH), lambda i: (i, 0)),
                  pl.BlockSpec((tm, S5_WIDTH), lambda i: (i, 0)),
                  pl.BlockSpec((1, S5_WIDTH), lambda i: (0, 0)),
                  pl.BlockSpec((S5_WIDTH, S5_WIDTH), lambda i: (0, 0)),
                  pl.BlockSpec((1, S5_WIDTH), lambda i: (0, 0))],
        out_specs=pl.BlockSpec((tm, S5_WIDTH), lambda i: (i, 0)),
        compiler_params=_params(("parallel",)),
        name="s5_gelu_glu",
    )(y_scan, proj, s5_d.reshape(1, S5_WIDTH), glu_w_bf, glu_b.reshape(1, S5_WIDTH))


S5_LB = 128
S5_LBG = S5_LB // S5_GROUP


def _s5chunks_kernel(p_ref, o_ref, x_ref, sel_ref):
    kw = S5_CHUNK * S5_LB
    pad = (S5_LBG - 1) * S5_GROUP

    @pl.when(pl.program_id(0) == 0)
    def _():
        k = lax.broadcasted_iota(jnp.int32, sel_ref.shape, 0) - pad
        lane = lax.broadcasted_iota(jnp.int32, sel_ref.shape, 1)
        hit = (k >= 0) & ((k % S5_LB) // S5_GROUP == 0) & (lane == (k // S5_LB) * S5_GROUP + k % S5_GROUP)
        sel_ref[...] = jnp.where(hit, 1.0, 0.0).astype(BF16)

    for n in range(S5_NCHUNK):
        for i in range(S5_CHUNK):
            rows = p_ref[pl.ds(n * S5_CHUNK + i, S5_NSEQ, stride=S5_SEQ), :]
            x_ref[n * S5_NSEQ:(n + 1) * S5_NSEQ, i * S5_LB:(i + 1) * S5_LB] = rows.astype(BF16)
    x = x_ref[...]
    for gl in range(S5_LBG):
        off = pad - gl * S5_GROUP
        o_ref[:, gl * S5_CW:(gl + 1) * S5_CW] = _dot(x, sel_ref[off:off + kw, :]).astype(BF16)


def _s5_chunks(proj):
    kw = S5_CHUNK * S5_LB
    return pl.pallas_call(
        _s5chunks_kernel,
        out_shape=jax.ShapeDtypeStruct((S5_ROWS, S5_GROUPS * S5_CW), BF16),
        grid=(S5_WIDTH // S5_LB,),
        in_specs=[pl.BlockSpec((N_TOK, S5_LB), lambda b: (0, b))],
        out_specs=pl.BlockSpec((S5_ROWS, kw), lambda b: (0, b)),
        scratch_shapes=[pltpu.VMEM((S5_ROWS, kw), BF16), pltpu.VMEM((kw + S5_LB, S5_CW), BF16)],
        compiler_params=_params(("arbitrary",)),
        name="s5_chunk_layout",
    )(proj)


def _s5_mixer(proj, state_re, state_im, prep, s5_d, glu_w_bf, glu_b):
    toep, s_mat, in_mat, coef = prep
    u_chunks = _s5_chunks(proj)
    z = _s5_z(u_chunks, s_mat)
    coef2 = coef.transpose(0, 2, 1, 3).reshape(2, 8, S5_GROUPS * S5_SW)
    h0 = jnp.concatenate([state_re, state_im], axis=-1)
    h0 = h0.transpose(1, 0, 2, 3).reshape(2, DEC_BATCH, S5_GROUPS * S5_SW)
    hin, ends = _s5_scan(z, coef2, h0)
    y_tok = _s5_y(u_chunks, toep, hin, in_mat)
    out = _s5_out(y_tok, proj, s5_d, glu_w_bf, glu_b)
    fin = ends[:, :BATCH].reshape(2, BATCH, S5_GROUPS, S5_SW).transpose(1, 0, 2, 3)
    return out, fin[..., :S5_STATE], fin[..., S5_STATE:]


def _softmax_rows(parts):
    m = parts[0].max(axis=-1, keepdims=True)
    for s in parts[1:]:
        m = jnp.maximum(m, s.max(axis=-1, keepdims=True))
    ps = [jnp.exp(s - m) for s in parts]
    tot = ps[0].sum(axis=-1, keepdims=True)
    for p in ps[1:]:
        tot = tot + p.sum(axis=-1, keepdims=True)
    inv = 1.0 / tot
    return [(p * inv).astype(BF16) for p in ps]


def _ctxattn_kernel(q_ref, k_ref, v_ref, o_ref, nk_ref, nv_ref):
    for h in range(NA_HEADS):
        sl = slice(h * NA_HEAD_DIM, (h + 1) * NA_HEAD_DIM)
        k = k_ref[:, sl]
        v = v_ref[:, sl]
        nk_ref[h] = k
        nv_ref[h] = v
        q = (q_ref[:, sl] * QK_SCALE).astype(BF16)
        (p,) = _softmax_rows([_dot_nt(q, k.astype(BF16))])
        o_ref[:, sl] = _dot(p, v.astype(BF16)).astype(BF16)


def _ctx_attention(proj):
    col = lambda c: pl.BlockSpec((SEQ, NA_WIDTH), lambda b: (b, c))
    cache = jax.ShapeDtypeStruct((BATCH, 1, NA_HEADS, SEQ, NA_HEAD_DIM), F32)
    cache_spec = pl.BlockSpec((None, None, NA_HEADS, SEQ, NA_HEAD_DIM), lambda b: (b, 0, 0, 0, 0))
    return pl.pallas_call(
        _ctxattn_kernel,
        out_shape=(jax.ShapeDtypeStruct((N_PROMPT, NA_WIDTH), BF16), cache, cache),
        grid=(BATCH,),
        in_specs=[col(1), col(2), col(3)],
        out_specs=(pl.BlockSpec((SEQ, NA_WIDTH), lambda b: (b, 0)), cache_spec, cache_spec),
        compiler_params=_params(("parallel",)),
        name="ctx_attention",
    )(proj, proj, proj)


def _na_geometry(r0):
    ks = min(max(r0 - NA_WIN_R // 2, 0), GRID_ROWS - NA_KROWS)
    tiles = []
    for a in range(NA_QROWS):
        rq = r0 + a
        rs = min(max(rq - NA_WIN_R // 2, 0), GRID_ROWS - NA_WIN_R)
        row = []
        for rl in range(NA_KROWS):
            rk = ks + rl
            row.append(rk - rq + NA_WIN_R - 1 if rs <= rk < rs + NA_WIN_R else None)
        tiles.append(row)
    return ks, tiles


NA_VARIANT_ROW0 = (0, NA_QROWS, GRID_ROWS - NA_QROWS)


def _nabias_kernel(rpb_ref, o_ref, t_ref):
    h = pl.program_id(0)
    shape = (GRID_W, 2 * GRID_W)
    cq = lax.broadcasted_iota(jnp.int32, shape, 0)
    lane = lax.broadcasted_iota(jnp.int32, shape, 1)
    ck = lane % GRID_W
    dc = jnp.clip(ck - cq + (NA_WIN_C - 1), 0, NA_NDC - 1)
    c0 = jnp.clip(cq - NA_WIN_C // 2, 0, GRID_W - NA_WIN_C)
    in_cols = (ck >= c0) & (ck < c0 + NA_WIN_C)
    neg = jnp.full(shape, NEG_INF, F32)
    for dr in range(NA_NDR):
        t = neg
        for v in range(NA_NDC):
            t = jnp.where(dc == v, rpb_ref[h, dr * NA_NDC + v], t)
        t_ref[dr] = jnp.where(in_cols, t, neg)
    left = lane < GRID_W
    for var, r0 in enumerate(NA_VARIANT_ROW0):
        _, tiles = _na_geometry(r0)
        for a in range(NA_QROWS):
            for m in range(NA_KROWS // 2):
                dl, dr_ = tiles[a][2 * m], tiles[a][2 * m + 1]
                tl = neg if dl is None else t_ref[dl]
                tr = neg if dr_ is None else t_ref[dr_]
                o_ref[var, a * GRID_W:(a + 1) * GRID_W, m * 2 * GRID_W:(m + 1) * 2 * GRID_W] = (
                    jnp.where(left, tl, tr))


def _na_bias(rpb):
    nq, nk = NA_QROWS * GRID_W, NA_KROWS * GRID_W
    return pl.pallas_call(
        _nabias_kernel,
        out_shape=jax.ShapeDtypeStruct((NA_HEADS, 3, nq, nk), F32),
        grid=(NA_HEADS,),
        in_specs=[pl.BlockSpec(memory_space=pltpu.SMEM)],
        out_specs=pl.BlockSpec((None, 3, nq, nk), lambda h: (h, 0, 0, 0)),
        scratch_shapes=[pltpu.VMEM((NA_NDR, GRID_W, 2 * GRID_W), F32)],
        compiler_params=_params(("parallel",)),
        name="na_bias",
    )(rpb.reshape(NA_HEADS, NA_NDR * NA_NDC))


def _naattn_kernel(q_ref, k_ref, v_ref, kc_ref, vc_ref, bias_ref, o_ref):
    kb = k_ref[...].astype(BF16)
    vb = v_ref[...].astype(BF16)
    kc = kc_ref[...].astype(BF16)
    vc = vc_ref[...].astype(BF16)
    nq = NA_QROWS * GRID_W
    for blk in range(GRID_ROWS // NA_QROWS):
        r0 = blk * NA_QROWS
        var = 0 if blk == 0 else (2 if r0 == NA_VARIANT_ROW0[2] else 1)
        ks, _ = _na_geometry(r0)
        keys = slice(ks * GRID_W, (ks + NA_KROWS) * GRID_W)
        q = (q_ref[r0 * GRID_W:r0 * GRID_W + nq, :] * QK_SCALE).astype(BF16)
        s_loc = _dot_nt(q, kb[keys]) + bias_ref[var]
        s_ctx = _dot_nt(q, kc)
        p_loc, p_ctx = _softmax_rows([s_loc, s_ctx])
        o = _dot(p_loc, vb[keys]) + _dot(p_ctx, vc)
        o_ref[r0 * GRID_W:r0 * GRID_W + nq, :] = o.astype(BF16)


def _na_attention(proj, cache_k, cache_v, bias):
    first = N_PROMPT // DEC_SEQ
    col = lambda c: pl.BlockSpec((DEC_SEQ, NA_HEAD_DIM), lambda b, h: (first + b, c * NA_HEADS + h))
    cache_spec = pl.BlockSpec((None, None, None, PAST_LEN, NA_HEAD_DIM), lambda b, h: (b, 0, h, 0, 0))
    return pl.pallas_call(
        _naattn_kernel,
        out_shape=jax.ShapeDtypeStruct((DEC_BATCH * DEC_SEQ, NA_WIDTH), BF16),
        grid=(DEC_BATCH, NA_HEADS),
        in_specs=[col(1), col(2), col(3), cache_spec, cache_spec,
                  pl.BlockSpec((None, 3, NA_QROWS * GRID_W, NA_KROWS * GRID_W), lambda b, h: (h, 0, 0, 0))],
        out_specs=pl.BlockSpec((DEC_SEQ, NA_HEAD_DIM), lambda b, h: (b, h)),
        compiler_params=_params(("parallel", "parallel")),
        name="na_attention",
    )(proj, proj, proj, cache_k, cache_v, bias)


def _outproj_kernel(a_ref, b_ref, w_ref, x_ref, gate_ref, g_ref, beta_ref, o_ref):
    half = a_ref.shape[1]
    y = _dot(a_ref[...], w_ref[0:half, :]) + _dot(b_ref[...], w_ref[half:2 * half, :])
    o_ref[...] = _post_norm(x_ref[...], y, gate_ref[...], g_ref[...], beta_ref[...])


def _out_proj(y_s5, y_att, w_bf, x, mod, ln_g, ln_b, layer):
    tm = 512
    half = y_s5.shape[1]
    return pl.pallas_call(
        _outproj_kernel,
        out_shape=jax.ShapeDtypeStruct((N_TOK, D_MODEL), F32),
        grid=(N_TOK // tm,),
        in_specs=[pl.BlockSpec((tm, half), lambda i: (i, 0)),
                  pl.BlockSpec((tm, half), lambda i: (i, 0)),
                  pl.BlockSpec((D_MODEL, D_MODEL), lambda i: (0, 0)),
                  pl.BlockSpec((tm, D_MODEL), lambda i: (i, 0)),
                  _mod_spec(layer, 2, tm), _row_spec(layer, 0), _row_spec(layer, 0)],
        out_specs=pl.BlockSpec((tm, D_MODEL), lambda i: (i, 0)),
        compiler_params=_params(("parallel",)),
        name="out_proj_norm",
    )(y_s5, y_att, w_bf, x, mod, ln_g, ln_b)


def _ffn_kernel(x_ref, sh_ref, sc_ref, gate_ref, g_ref, beta_ref, wg_ref, wu_ref, wd_ref, o_ref, h_ref, acc_ref):
    f = pl.program_id(1)

    @pl.when(f == 0)
    def _():
        h_ref[...] = (x_ref[...] * (1.0 + sc_ref[...]) + sh_ref[...]).astype(BF16)
        acc_ref[...] = jnp.zeros_like(acc_ref)

    h = h_ref[...]
    a = _dot(h, wg_ref[...])
    b = _dot(h, wu_ref[...])
    acc_ref[...] += _dot((a * _sigmoid(a) * b).astype(BF16), wd_ref[...])

    @pl.when(f == pl.num_programs(1) - 1)
    def _():
        o_ref[...] = _post_norm(x_ref[...], acc_ref[...], gate_ref[...], g_ref[...], beta_ref[...])


def _ffn(x, mod, ln_g, ln_b, wg_bf, wu_bf, wd_bf, layer):
    tm, tf = 512, 512
    return pl.pallas_call(
        _ffn_kernel,
        out_shape=jax.ShapeDtypeStruct((N_TOK, D_MODEL), F32),
        grid=(N_TOK // tm, FFN_DIM // tf),
        in_specs=[pl.BlockSpec((tm, D_MODEL), lambda i, f: (i, 0)),
                  _mod_spec(layer, 3, tm), _mod_spec(layer, 4, tm), _mod_spec(layer, 5, tm),
                  _row_spec(layer, 1), _row_spec(layer, 1),
                  pl.BlockSpec((D_MODEL, tf), lambda i, f: (0, f)),
                  pl.BlockSpec((D_MODEL, tf), lambda i, f: (0, f)),
                  pl.BlockSpec((tf, D_MODEL), lambda i, f: (f, 0))],
        out_specs=pl.BlockSpec((tm, D_MODEL), lambda i, f: (i, 0)),
        scratch_shapes=[pltpu.VMEM((tm, D_MODEL), BF16), pltpu.VMEM((tm, D_MODEL), F32)],
        compiler_params=_params(("parallel", "arbitrary")),
        name="ffn_norm",
    )(x, mod, mod, mod, ln_g, ln_b, wg_bf, wu_bf, wd_bf)


def _pool_kernel(x_ref, prev_ref, next_ref, sh_ref, sc_ref, gate_ref, g_ref, beta_ref, w_ref, ps_ref,
                 o_ref, ext_ref, y_ref):
    q = pl.program_id(0)
    latent = q >= BATCH
    seg = (q - BATCH) % S5_SEGS
    has_prev = latent & (seg > 0)
    has_next = latent & (seg < S5_SEGS - 1)
    seq_len = jnp.where(latent, DEC_SEQ, SEQ)
    t = jnp.where(latent, seg * SEQ, 0) + lax.broadcasted_iota(jnp.int32, (SEQ, 1), 0)

    scale = 1.0 + sc_ref[...]
    shift = sh_ref[...]
    x = x_ref[...]
    halo = POOL_HALO
    ext_ref[0:halo, :] = jnp.where(has_prev, prev_ref[...] * scale + shift, 0.0)
    ext_ref[halo:halo + SEQ, :] = x * scale + shift
    ext_ref[halo + SEQ:2 * halo + SEQ, :] = jnp.where(has_next, next_ref[...] * scale + shift, 0.0)

    for g, w in enumerate(POOL_SIZES):
        cols = slice(g * POOL_GROUP_DIM, (g + 1) * POOL_GROUP_DIM)
        total = None
        for k in range(-(w // 2), w - w // 2):
            part = ext_ref[halo + k:halo + k + SEQ, cols]
            total = part if total is None else total + part
        count = jnp.minimum(t + (w - w // 2), seq_len) - jnp.maximum(t - w // 2, 0)
        pooled = total / count.astype(F32) - ext_ref[halo:halo + SEQ, cols]
        y_ref[:, cols] = _dot(pooled.astype(BF16), w_ref[g])
    y = y_ref[...] * ps_ref[...]
    o_ref[...] = _post_norm(x, y, gate_ref[...], g_ref[...], beta_ref[...])


def _pool(x, mod, ln_g, ln_b, w_bf, pool_scale, layer):
    tm = SEQ
    nhb = N_TOK // POOL_HALO
    per = tm // POOL_HALO
    return pl.pallas_call(
        _pool_kernel,
        out_shape=jax.ShapeDtypeStruct((N_TOK, D_MODEL), F32),
        grid=(N_TOK // tm,),
        in_specs=[pl.BlockSpec((tm, D_MODEL), lambda i: (i, 0)),
                  pl.BlockSpec((POOL_HALO, D_MODEL), lambda i: (jnp.maximum(i * per - 1, 0), 0)),
                  pl.BlockSpec((POOL_HALO, D_MODEL), lambda i: (jnp.minimum((i + 1) * per, nhb - 1), 0)),
                  _mod_spec(layer, 0, tm), _mod_spec(layer, 1, tm), _mod_spec(layer, 2, tm),
                  _row_spec(layer, 0), _row_spec(layer, 0),
                  pl.BlockSpec((len(POOL_SIZES), POOL_GROUP_DIM, POOL_GROUP_DIM), lambda i: (0, 0, 0)),
                  pl.BlockSpec((1, D_MODEL), lambda i: (0, 0))],
        out_specs=pl.BlockSpec((tm, D_MODEL), lambda i: (i, 0)),
        scratch_shapes=[pltpu.VMEM((tm + 2 * POOL_HALO, D_MODEL), F32), pltpu.VMEM((tm, D_MODEL), F32)],
        compiler_params=_params(("parallel",)),
        name="pool_norm",
    )(x, x, x, mod, mod, mod, ln_g, ln_b, w_bf, pool_scale.reshape(1, D_MODEL))


ROUTER_LANES = 128


def _split_bf16(a):
    hi = a.astype(BF16)
    return hi, (a - hi.astype(F32)).astype(BF16)


def _router_kernel(x_ref, sh_ref, sc_ref, w_ref, b_ref, h_ref, info_ref):
    h = x_ref[...] * (1.0 + sc_ref[...]) + sh_ref[...]
    h_ref[...] = h
    hh, hl = _split_bf16(h)
    wh, wl = _split_bf16(w_ref[...])
    logits = _dot(hh, wh) + _dot(hl, wh) + _dot(hh, wl) + b_ref[...]
    lane = lax.broadcasted_iota(jnp.int32, logits.shape, 1)
    logits = jnp.where(lane < N_EXPERTS, logits, -jnp.inf)
    m1 = logits.max(axis=-1, keepdims=True)
    i1 = jnp.where(logits == m1, lane, ROUTER_LANES).min(axis=-1, keepdims=True)
    rest = jnp.where(lane == i1, -jnp.inf, logits)
    m2 = rest.max(axis=-1, keepdims=True)
    i2 = jnp.where(rest == m2, lane, ROUTER_LANES).min(axis=-1, keepdims=True)
    e = jnp.exp(m2 - m1)
    g1 = 1.0 / (1.0 + e)
    g2 = e / (1.0 + e)
    info = jnp.where(lane == 0, i1.astype(F32), jnp.where(lane == 1, i2.astype(F32),
                     jnp.where(lane == 2, g1, jnp.where(lane == 3, g2, 0.0))))
    info_ref[...] = info


def _router(x, mod, router_w, router_b, layer):
    tm = 512
    w = jnp.zeros((D_MODEL, ROUTER_LANES), F32).at[:, :N_EXPERTS].set(router_w)
    b = jnp.zeros((1, ROUTER_LANES), F32).at[0, :N_EXPERTS].set(router_b)
    return pl.pallas_call(
        _router_kernel,
        out_shape=(jax.ShapeDtypeStruct((N_TOK, D_MODEL), F32),
                   jax.ShapeDtypeStruct((N_TOK, ROUTER_LANES), F32)),
        grid=(N_TOK // tm,),
        in_specs=[pl.BlockSpec((tm, D_MODEL), lambda i: (i, 0)),
                  _mod_spec(layer, 3, tm), _mod_spec(layer, 4, tm),
                  pl.BlockSpec((D_MODEL, ROUTER_LANES), lambda i: (0, 0)),
                  pl.BlockSpec((1, ROUTER_LANES), lambda i: (0, 0))],
        out_specs=(pl.BlockSpec((tm, D_MODEL), lambda i: (i, 0)),
                   pl.BlockSpec((tm, ROUTER_LANES), lambda i: (i, 0))),
        compiler_params=_params(("parallel",)),
        name="moe_router",
    )(x, mod, mod, w, b)


def _routing_tables(info):
    experts = info[:, :2].astype(jnp.int32).reshape(-1)
    onehot = (experts[:, None] == jnp.arange(N_EXPERTS)[None, :]).astype(jnp.int32)
    counts = onehot.sum(axis=0)
    rank = ((jnp.cumsum(onehot, axis=0) - onehot) * onehot).sum(axis=1)
    rows = (counts + MOE_TILE - 1) // MOE_TILE * MOE_TILE
    span = (rows + MOE_SPAN - 1) // MOE_SPAN * MOE_SPAN
    ends = jnp.cumsum(span)
    starts = ends - span
    pos = (starts[experts] + rank).astype(jnp.int32)
    src = jnp.zeros((MOE_ROWS,), jnp.int32).at[pos].set(jnp.arange(2 * N_TOK, dtype=jnp.int32) // 2)
    tile_row = jnp.arange(MOE_NTILES, dtype=jnp.int32) * MOE_TILE
    owner = jnp.minimum(jnp.sum(tile_row[:, None] >= ends[None, :], axis=1), N_EXPERTS - 1)
    used = (tile_row < (starts + rows)[owner]) & (tile_row < ends[-1])
    fetch = lax.cummax(jnp.where(used, jnp.arange(MOE_NTILES, dtype=jnp.int32), 0))
    tile_tab = (used.astype(jnp.int32), fetch.astype(jnp.int32), owner[fetch].astype(jnp.int32))
    pos = pos.reshape(N_TOK, 2)
    return src, tile_tab, pos[:, 0], pos[:, 1]


def _row_copy(src_hbm, row, dst, r, sem):
    return pltpu.make_async_copy(src_hbm.at[pl.ds(row, 1), :], dst.at[pl.ds(r, 1), :], sem)


def _dispatch_kernel(src_ref, used_ref, h_hbm, o_ref, buf, sem):
    m = pl.program_id(0)
    used = used_ref[m] == 1

    @pl.when(used)
    def _():
        def issue(g, c):
            for u in range(DMA_UNROLL):
                r = g * DMA_UNROLL + u
                _row_copy(h_hbm, src_ref[m * MOE_TILE + r], buf, r, sem).start(priority=u % 2)
            return c

        lax.fori_loop(0, MOE_TILE // DMA_UNROLL, issue, 0)

        def drain(r, c):
            _row_copy(h_hbm, 0, buf, r, sem).wait()
            return c

        lax.fori_loop(0, MOE_TILE, drain, 0, unroll=DMA_UNROLL)
        o_ref[...] = buf[...].astype(BF16)

    @pl.when(jnp.logical_not(used))
    def _():
        o_ref[...] = jnp.zeros_like(o_ref)


def _dispatch(h, src, tile_tab):
    return pl.pallas_call(
        _dispatch_kernel,
        out_shape=jax.ShapeDtypeStruct((MOE_ROWS, D_MODEL), BF16),
        grid_spec=pltpu.PrefetchScalarGridSpec(
            num_scalar_prefetch=2,
            grid=(MOE_NTILES,),
            in_specs=[pl.BlockSpec(memory_space=pl.ANY)],
            out_specs=pl.BlockSpec((MOE_TILE, D_MODEL), lambda m, *_: (m, 0)),
            scratch_shapes=[pltpu.VMEM((MOE_TILE, D_MODEL), F32), pltpu.SemaphoreType.DMA(())],
        ),
        compiler_params=_params(("arbitrary",)),
        name="moe_dispatch",
    )(src, tile_tab[0], h)


def _gmm_up_kernel(used_ref, fetch_ref, exp_ref, x_ref, wg_ref, wu_ref, o_ref, wgb_ref, wub_ref):
    j = pl.program_id(1)
    t0 = j * MOE_PAIR
    used = used_ref[t0] == 1
    pair = used_ref[t0 + 1] == 1
    fresh = jnp.logical_or(j == 0, exp_ref[t0] != exp_ref[jnp.maximum(t0 - MOE_PAIR, 0)])

    @pl.when(jnp.logical_and(used, fresh))
    def _():
        wgb_ref[...] = wg_ref[...].astype(BF16)
        wub_ref[...] = wu_ref[...].astype(BF16)

    def swiglu_up(x):
        a = _dot(x, wgb_ref[...])
        b = _dot(x, wub_ref[...])
        return (a * _sigmoid(a) * b).astype(BF16)

    @pl.when(jnp.logical_and(used, pair))
    def _():
        o_ref[...] = swiglu_up(x_ref[...])

    @pl.when(jnp.logical_and(used, jnp.logical_not(pair)))
    def _():
        o_ref[0:MOE_TILE, :] = swiglu_up(x_ref[0:MOE_TILE, :])
        o_ref[MOE_TILE:MOE_SPAN, :] = jnp.zeros((MOE_SPAN - MOE_TILE, o_ref.shape[1]), BF16)

    @pl.when(jnp.logical_not(used))
    def _():
        o_ref[...] = jnp.zeros_like(o_ref)


def _gmm_up(xs, w_gate, w_up, tile_tab):
    tf = 1024
    blk = lambda j, fetch: fetch[j * MOE_PAIR] // MOE_PAIR
    wsp = pl.BlockSpec((None, D_MODEL, tf), lambda f, j, used, fetch, exp: (exp[j * MOE_PAIR], 0, f))
    return pl.pallas_call(
        _gmm_up_kernel,
        out_shape=jax.ShapeDtypeStruct((MOE_ROWS, EXPERT_DIM), BF16),
        grid_spec=pltpu.PrefetchScalarGridSpec(
            num_scalar_prefetch=3,
            grid=(EXPERT_DIM // tf, MOE_NTILES // MOE_PAIR),
            in_specs=[pl.BlockSpec((MOE_SPAN, D_MODEL), lambda f, j, used, fetch, exp: (blk(j, fetch), 0)),
                      wsp, wsp],
            out_specs=pl.BlockSpec((MOE_SPAN, tf), lambda f, j, used, fetch, exp: (j, f)),
            scratch_shapes=[pltpu.VMEM((D_MODEL, tf), BF16), pltpu.VMEM((D_MODEL, tf), BF16)],
        ),
        compiler_params=_params(("parallel", "arbitrary")),
        name="moe_gate_up",
    )(*tile_tab, xs, w_gate, w_up)


def _gmm_down_kernel(used_ref, fetch_ref, exp_ref, x_ref, w_ref, o_ref, wbf_ref):
    j = pl.program_id(1)
    t0 = j * MOE_PAIR
    used = used_ref[t0] == 1
    pair = used_ref[t0 + 1] == 1
    fresh = jnp.logical_or(j == 0, exp_ref[t0] != exp_ref[jnp.maximum(t0 - MOE_PAIR, 0)])

    @pl.when(jnp.logical_and(used, fresh))
    def _():
        wbf_ref[...] = w_ref[...].astype(BF16)

    @pl.when(jnp.logical_and(used, pair))
    def _():
        o_ref[...] = _dot(x_ref[...], wbf_ref[...])

    @pl.when(jnp.logical_and(used, jnp.logical_not(pair)))
    def _():
        o_ref[0:MOE_TILE, :] = _dot(x_ref[0:MOE_TILE, :], wbf_ref[...])
        o_ref[MOE_TILE:MOE_SPAN, :] = jnp.zeros((MOE_SPAN - MOE_TILE, o_ref.shape[1]), F32)

    @pl.when(jnp.logical_not(used))
    def _():
        o_ref[...] = jnp.zeros_like(o_ref)


def _gmm_down(g, w_down, tile_tab):
    tn = 512
    blk = lambda j, fetch: fetch[j * MOE_PAIR] // MOE_PAIR
    return pl.pallas_call(
        _gmm_down_kernel,
        out_shape=jax.ShapeDtypeStruct((MOE_ROWS, D_MODEL), F32),
        grid_spec=pltpu.PrefetchScalarGridSpec(
            num_scalar_prefetch=3,
            grid=(D_MODEL // tn, MOE_NTILES // MOE_PAIR),
            in_specs=[pl.BlockSpec((MOE_SPAN, EXPERT_DIM), lambda n, j, used, fetch, exp: (blk(j, fetch), 0)),
                      pl.BlockSpec((None, EXPERT_DIM, tn), lambda n, j, used, fetch, exp: (exp[j * MOE_PAIR], 0, n))],
            out_specs=pl.BlockSpec((MOE_SPAN, tn), lambda n, j, used, fetch, exp: (j, n)),
            scratch_shapes=[pltpu.VMEM((EXPERT_DIM, tn), BF16)],
        ),
        compiler_params=_params(("parallel", "arbitrary"), vmem=VMEM_LIMIT_MAX),
        name="moe_down",
    )(*tile_tab, g, w_down)


def _combine_kernel(p1_ref, p2_ref, y_hbm, x_ref, info_ref, gate_ref, g_ref, beta_ref, o_ref, b1, b2, sem):
    i = pl.program_id(0)
    tm = x_ref.shape[0]

    def issue(g, c):
        for u in range(DMA_UNROLL):
            r = g * DMA_UNROLL + u
            _row_copy(y_hbm, p1_ref[i * tm + r], b1, r, sem).start(priority=0)
            _row_copy(y_hbm, p2_ref[i * tm + r], b2, r, sem).start(priority=1)
        return c

    lax.fori_loop(0, tm // DMA_UNROLL, issue, 0)

    def drain(r, c):
        _row_copy(y_hbm, 0, b1, r, sem).wait()
        _row_copy(y_hbm, 0, b2, r, sem).wait()
        return c

    lax.fori_loop(0, tm, drain, 0, unroll=8)
    info = info_ref[...]
    y = info[:, 2:3] * b1[...] + info[:, 3:4] * b2[...]
    o_ref[...] = _post_norm(x_ref[...], y, gate_ref[...], g_ref[...], beta_ref[...])


def _combine(y_sorted, pos1, pos2, x, info, mod, ln_g, ln_b, layer):
    tm = 256
    return pl.pallas_call(
        _combine_kernel,
        out_shape=jax.ShapeDtypeStruct((N_TOK, D_MODEL), F32),
        grid_spec=pltpu.PrefetchScalarGridSpec(
            num_scalar_prefetch=2,
            grid=(N_TOK // tm,),
            in_specs=[pl.BlockSpec(memory_space=pl.ANY),
                      pl.BlockSpec((tm, D_MODEL), lambda i, *_: (i, 0)),
                      pl.BlockSpec((tm, ROUTER_LANES), lambda i, *_: (i, 0)),
                      _mod_spec(layer, 5, tm), _row_spec(layer, 1), _row_spec(layer, 1)],
            out_specs=pl.BlockSpec((tm, D_MODEL), lambda i, *_: (i, 0)),
            scratch_shapes=[pltpu.VMEM((tm, D_MODEL), F32), pltpu.VMEM((tm, D_MODEL), F32),
                            pltpu.SemaphoreType.DMA(())],
        ),
        compiler_params=_params(("arbitrary",)),
        name="moe_combine_norm",
    )(pos1, pos2, y_sorted, x, info, mod, ln_g, ln_b)


def _moe(x, mod, ln_g, ln_b, router_w, router_b, w_gate, w_up, w_down, layer):
    h, info = _router(x, mod, router_w, router_b, layer)
    src, tile_tab, pos1, pos2 = _routing_tables(info)
    xs = _dispatch(h, src, tile_tab)
    g = _gmm_up(xs, w_gate, w_up, tile_tab)
    y = _gmm_down(g, w_down, tile_tab)
    return _combine(y, pos1, pos2, x, info, mod, ln_g, ln_b, layer)


def kernel(x_prompt, x_sample, state_s5_re, state_s5_im, cache_k, cache_v, c, c_ctx, ada_w, ada_b, ln_g, ln_b, ab_w_in, ab_w_out, s5_lambda_re, s5_lambda_im, s5_log_dt, s5_b_re, s5_b_im, s5_c_re, s5_c_im, s5_d, s5_glu_w, s5_glu_b, na_rpb, ffn_w_gate, ffn_w_up, ffn_w_down, pool_w, pool_scale, moe_router_w, moe_router_b, moe_w_gate, moe_w_up, moe_w_down):
    x = jnp.concatenate([x_prompt.reshape(N_PROMPT, D_MODEL), x_sample.reshape(-1, D_MODEL)], axis=0)

    cond8 = jnp.zeros((8, D_MODEL), F32).at[0].set(c_ctx).at[1:N_COND].set(c)
    mod = _ada(cond8, ada_w, ada_b)
    mod = mod[:, :N_COND].reshape(DEPTH, N_COND, N_MOD, 1, D_MODEL).transpose(0, 2, 1, 3, 4)
    ln_g4 = ln_g.reshape(DEPTH, 2, 1, D_MODEL)
    ln_b4 = ln_b.reshape(DEPTH, 2, 1, D_MODEL)

    proj = _inproj(x, mod, _cast_bf16(ab_w_in[0]), 0)
    prep = _s5_prep(s5_lambda_re[0], s5_lambda_im[0], s5_log_dt[0], s5_b_re[0], s5_b_im[0],
                    s5_c_re[0], s5_c_im[0])
    y_s5, fin_re, fin_im = _s5_mixer(proj, state_s5_re[:, 0], state_s5_im[:, 0], prep, s5_d[0],
                                     _cast_bf16(s5_glu_w[0]), s5_glu_b[0])
    y_ctx, new_k, new_v = _ctx_attention(proj)
    y_na = _na_attention(proj, cache_k, cache_v, _na_bias(na_rpb[0]))
    y_att = jnp.concatenate([y_ctx, y_na], axis=0)
    x = _out_proj(y_s5, y_att, _cast_bf16(ab_w_out[0]), x, mod, ln_g4, ln_b4, 0)
    x = _ffn(x, mod, ln_g4, ln_b4, _cast_bf16(ffn_w_gate[0]), _cast_bf16(ffn_w_up[0]),
             _cast_bf16(ffn_w_down[0]), 0)

    x = _pool(x, mod, ln_g4, ln_b4, _cast_bf16(pool_w[0]), pool_scale[0], 1)
    x = _moe(x, mod, ln_g4, ln_b4, moe_router_w[0], moe_router_b[0], moe_w_gate[0], moe_w_up[0],
             moe_w_down[0], 1)

    y_prompt = x[:N_PROMPT].reshape(BATCH, SEQ, D_MODEL)
    y_sample = x[N_PROMPT:].reshape(DEC_BATCH, DEC_SEQ, D_MODEL)
    return (y_prompt, y_sample, fin_re[:, None], fin_im[:, None], new_k, new_v)
```

```python
import functools
import math

import jax
import jax.numpy as jnp
from jax import lax
from jax.experimental import pallas as pl
from jax.experimental.pallas import tpu as pltpu

F32 = jnp.float32
BF16 = jnp.bfloat16

D_MODEL = 2048
BATCH = 16
SEQ = 256
DEPTH = 2
DEC_BATCH = 2
DEC_SEQ = 2048
N_MOD = 6
N_PROMPT = BATCH * SEQ
N_TOK = N_PROMPT + DEC_BATCH * DEC_SEQ
N_COND = 1 + DEC_BATCH

S5_WIDTH = 1024
S5_GROUP = 16
S5_GROUPS = 64
S5_STATE = 64
S5_CHUNK = 16
S5_SEQ = 256
S5_NSEQ = N_TOK // S5_SEQ
S5_NCHUNK = S5_SEQ // S5_CHUNK
S5_ROWS = S5_NSEQ * S5_NCHUNK
S5_SEGS = DEC_SEQ // S5_SEQ
S5_CW = S5_CHUNK * S5_GROUP
S5_SW = 2 * S5_STATE

NA_WIDTH = 1024
NA_HEADS = 8
NA_HEAD_DIM = 128
NA_WIN_R = 8
NA_WIN_C = 16
GRID_W = 64
GRID_ROWS = DEC_SEQ // GRID_W
NA_QROWS = 4
NA_KROWS = 12
NA_NDR = 2 * NA_WIN_R - 1
NA_NDC = 2 * NA_WIN_C - 1
PAST_LEN = 256

POOL_SIZES = (2, 4, 8, 16)
POOL_GROUP_DIM = 512
POOL_HALO = 16

FFN_DIM = 5632
N_EXPERTS = 8
EXPERT_DIM = 7168
DMA_UNROLL = 8
MOE_TILE = 256
MOE_PAIR = 2
MOE_SPAN = MOE_PAIR * MOE_TILE
MOE_ROWS = 2 * N_TOK + N_EXPERTS * MOE_SPAN
MOE_NTILES = MOE_ROWS // MOE_TILE

LN_EPS = 1e-5
DEEPNORM_ALPHA = (2.0 * DEPTH) ** 0.25
NEG_INF = -1e30
QK_SCALE = NA_HEAD_DIM ** -0.5

VMEM_LIMIT = 52 * 1024 * 1024
VMEM_LIMIT_MAX = 60 * 1024 * 1024


def _params(sem, vmem=VMEM_LIMIT):
    return pltpu.CompilerParams(dimension_semantics=sem, vmem_limit_bytes=vmem)


def _sigmoid(x):
    return 1.0 / (1.0 + jnp.exp(-x))


def _dot(a, b):
    return jnp.dot(a, b, preferred_element_type=F32)


def _dot_nt(a, b):
    return lax.dot_general(a, b, (((1,), (1,)), ((), ())), preferred_element_type=F32)


def _dot_exact(a, b):
    return jnp.dot(a, b, preferred_element_type=F32, precision=lax.Precision.HIGHEST)


def _cond_of_row(row0):
    return jnp.where(row0 < N_PROMPT, 0, 1 + (row0 - N_PROMPT) // DEC_SEQ)


def _mod_spec(layer, k, tm):
    return pl.BlockSpec((None, None, None, 1, D_MODEL),
                        lambda i, *_: (layer, k, _cond_of_row(i * tm), 0, 0))


def _row_spec(layer_idx, k=None):
    if k is None:
        return pl.BlockSpec((None, 1, D_MODEL), lambda *_: (layer_idx, 0, 0))
    return pl.BlockSpec((None, None, 1, D_MODEL), lambda *_: (layer_idx, k, 0, 0))


def _post_norm(x, y, gate, g, b):
    v = DEEPNORM_ALPHA * x + gate * y
    mu = jnp.mean(v, axis=-1, keepdims=True)
    c = v - mu
    var = jnp.mean(c * c, axis=-1, keepdims=True)
    return c * lax.rsqrt(var + LN_EPS) * g + b


def _ada_kernel(c_ref, w_ref, b_ref, o_ref):
    c = c_ref[...]
    s = (c * _sigmoid(c)).astype(BF16)
    o_ref[...] = _dot(s, w_ref[...].astype(BF16)) + b_ref[...]


def _ada(cond8, ada_w, ada_b):
    tn = 1024
    n_out = N_MOD * D_MODEL
    return pl.pallas_call(
        _ada_kernel,
        out_shape=jax.ShapeDtypeStruct((DEPTH, 8, n_out), F32),
        grid=(DEPTH, n_out // tn),
        in_specs=[pl.BlockSpec((8, D_MODEL), lambda l, n: (0, 0)),
                  pl.BlockSpec((None, D_MODEL, tn), lambda l, n: (l, 0, n)),
                  pl.BlockSpec((None, 1, tn), lambda l, n: (l, 0, n))],
        out_specs=pl.BlockSpec((None, 8, tn), lambda l, n: (l, 0, n)),
        compiler_params=_params(("parallel", "parallel")),
        name="ada_mod",
    )(cond8, ada_w, ada_b.reshape(DEPTH, 1, n_out))


def _cast_kernel(w_ref, o_ref):
    o_ref[...] = w_ref[...].astype(BF16)


def _cast_bf16(w):
    shape = w.shape
    cols = shape[-1]
    rows = w.size // cols
    rb = 8
    while rb * 2 * cols * 4 <= 4 * 1024 * 1024 and rows % (rb * 2) == 0:
        rb *= 2
    out = pl.pallas_call(
        _cast_kernel,
        out_shape=jax.ShapeDtypeStruct((rows, cols), BF16),
        grid=(rows // rb,),
        in_specs=[pl.BlockSpec((rb, cols), lambda i: (i, 0))],
        out_specs=pl.BlockSpec((rb, cols), lambda i: (i, 0)),
        compiler_params=_params(("parallel",)),
        name="cast_bf16",
    )(w.reshape(rows, cols))
    return out.reshape(shape)


def _inproj_kernel(x_ref, sh_ref, sc_ref, w_ref, o_ref, h_ref):
    @pl.when(pl.program_id(1) == 0)
    def _():
        h_ref[...] = (x_ref[...] * (1.0 + sc_ref[...]) + sh_ref[...]).astype(BF16)

    o_ref[...] = _dot(h_ref[...], w_ref[...])


def _inproj(x, mod, w_bf, layer):
    tm, tn = 1024, 1024
    n_out = w_bf.shape[1]
    return pl.pallas_call(
        _inproj_kernel,
        out_shape=jax.ShapeDtypeStruct((N_TOK, n_out), F32),
        grid=(N_TOK // tm, n_out // tn),
        in_specs=[pl.BlockSpec((tm, D_MODEL), lambda i, n: (i, 0)),
                  _mod_spec(layer, 0, tm), _mod_spec(layer, 1, tm),
                  pl.BlockSpec((D_MODEL, tn), lambda i, n: (0, n))],
        out_specs=pl.BlockSpec((tm, tn), lambda i, n: (i, n)),
        scratch_shapes=[pltpu.VMEM((tm, D_MODEL), BF16)],
        compiler_params=_params(("parallel", "arbitrary")),
        name="in_proj",
    )(x, mod, mod, w_bf)


def _cpow_table(re, im, n):
    out = [(jnp.ones_like(re), jnp.zeros_like(im))]
    for _ in range(n):
        pr, pi = out[-1]
        out.append((pr * re - pi * im, pr * im + pi * re))
    return out


def _lam_bar(lam):
    re = jnp.minimum(lam[0], -1e-4)
    im = lam[1]
    dt = jnp.exp(lam[2])
    mag = jnp.exp(re * dt)
    return re, im, mag * jnp.cos(im * dt), mag * jnp.sin(im * dt)


def _s5prep_kernel(lamc_ref, lamr_ref, ct_ref, bt_ref, b12_ref, toep_ref, s_ref, in_ref, co_ref):
    fwd = pl.program_id(0) == 0
    P, G, T = S5_STATE, S5_GROUP, S5_CHUNK

    lc = lamc_ref[...]
    _, _, lbr_c, lbi_c = _lam_bar((lc[:, 0:1], lc[:, 1:2], lc[:, 2:3]))
    pw_c = _cpow_table(lbr_c, lbi_c, T)

    half = T // 2
    kblk = lax.broadcasted_iota(jnp.int32, (P, half * G), 1) // G
    zero = jnp.zeros((P, half * G), F32)
    up = [zero, zero]
    down = [zero, zero]
    for k in range(half):
        hit = kblk == k
        up = [jnp.where(hit, pw_c[k][c], up[c]) for c in range(2)]
        down = [jnp.where(hit, pw_c[half - k][c], down[c]) for c in range(2)]

    def cmul(a, b):
        return a[0] * b[0] - a[1] * b[1], a[0] * b[1] + a[1] * b[0]

    def either(a, b):
        return tuple(jnp.where(fwd, a[c], b[c]) for c in range(2))

    def lanes_cat(cols):
        return tuple(jnp.concatenate([col[c] for col in cols], axis=1) for c in range(2))

    none = (zero, zero)
    one0 = (jnp.where(kblk == 0, 1.0, 0.0), zero)
    up8 = cmul(pw_c[half], up)
    down8 = cmul(pw_c[half], down)

    wide = 2 * T * G
    lane = lax.broadcasted_iota(jnp.int32, (G, wide), 1)
    row = lax.broadcasted_iota(jnp.int32, (G, wide), 0)
    expand = (lane % G == row).astype(F32)
    ct_re = _dot_exact(ct_ref[0], expand)
    ct_im = _dot_exact(ct_ref[1], expand)

    pr, pi = lanes_cat([either(none, down8), either(none, down), either(up, one0), either(up8, none)])
    r_re = pr * ct_re - pi * ct_im
    r_im = pr * ct_im + pi * ct_re

    lr = lamr_ref[...]
    re_r, im_r, lbr_r, lbi_r = _lam_bar((lr[0:1, :], lr[1:2, :], lr[2:3, :]))
    den = re_r * re_r + im_r * im_r
    cf_re = ((lbr_r - 1.0) * re_r + lbi_r * im_r) / den
    cf_im = (lbi_r * re_r - (lbr_r - 1.0) * im_r) / den
    bb_re = cf_re[:, :P] * bt_ref[0] - cf_im[:, :P] * bt_ref[1]
    bb_im = cf_re[:, :P] * bt_ref[1] + cf_im[:, :P] * bt_ref[0]
    kext = _dot_exact(bb_re, r_re) - _dot_exact(bb_im, r_im)
    for i in range(T):
        off = (T - i) * G
        toep_ref[i * G:(i + 1) * G, :] = kext[:, off:off + T * G].astype(BF16)

    qr, qi = lanes_cat([either(cmul(pw_c[1], up), down8), either(cmul(pw_c[half + 1], up), down)])
    c_re = ct_re[:, :T * G]
    c_im = ct_im[:, :T * G]
    in_ref[0:P, :] = (qr * c_re - qi * c_im).astype(BF16)
    in_ref[P:2 * P, :] = (-(qr * c_im + qi * c_re)).astype(BF16)

    pw_r = _cpow_table(lbr_r, lbi_r, T)
    b1 = b12_ref[0]
    b2 = b12_ref[1]
    for i in range(T):
        wr = jnp.where(fwd, pw_r[T - 1 - i][0], pw_r[i][0])
        wi = jnp.where(fwd, pw_r[T - 1 - i][1], pw_r[i][1])
        sr = wr * cf_re - wi * cf_im
        si = wr * cf_im + wi * cf_re
        s_ref[i * G:(i + 1) * G, :] = (sr * b1 + si * b2).astype(BF16)

    sgn = jnp.where(lax.broadcasted_iota(jnp.int32, (1, S5_SW), 1) < P, -1.0, 1.0)
    r16, i16 = pw_r[T]
    r256, i256 = r16, i16
    for _ in range(4):
        r256, i256 = r256 * r256 - i256 * i256, 2.0 * r256 * i256
    for k, rowv in enumerate((r16, sgn * i16, r256, sgn * i256)):
        co_ref[k:k + 1, :] = rowv
    co_ref[4:8, :] = jnp.zeros((4, S5_SW), F32)


def _s5_prep(lam_re, lam_im, log_dt, b_re, b_im, c_re, c_im):
    P, G = S5_STATE, S5_GROUP
    ldt = jnp.broadcast_to(log_dt[..., None], lam_re.shape)
    lamc = jnp.stack([lam_re, lam_im, ldt], axis=-1)
    dup = lambda a: jnp.concatenate([a, a], axis=-1)
    lamr = jnp.stack([dup(lam_re), dup(lam_im), dup(ldt)], axis=-2)
    ct = jnp.stack([jnp.swapaxes(c_re, -1, -2), jnp.swapaxes(c_im, -1, -2)], axis=2)
    bt_re = jnp.swapaxes(b_re, -1, -2)
    bt_im = jnp.swapaxes(b_im, -1, -2)
    bt = jnp.stack([bt_re, bt_im], axis=2)
    b12 = jnp.stack([jnp.concatenate([bt_re, bt_im], -1),
                     jnp.concatenate([-bt_im, bt_re], -1)], axis=2)
    blk = lambda *s: pl.BlockSpec((None, None) + s, lambda d, g: (d, g) + (0,) * len(s))
    return pl.pallas_call(
        _s5prep_kernel,
        out_shape=(jax.ShapeDtypeStruct((2, S5_GROUPS, S5_CW, S5_CW), BF16),
                   jax.ShapeDtypeStruct((2, S5_GROUPS, S5_CW, S5_SW), BF16),
                   jax.ShapeDtypeStruct((2, S5_GROUPS, S5_SW, S5_CW), BF16),
                   jax.ShapeDtypeStruct((2, S5_GROUPS, 8, S5_SW), F32)),
        grid=(2, S5_GROUPS),
        in_specs=[blk(P, 3), blk(3, 2 * P), blk(2, P, G), blk(2, G, P), blk(2, G, 2 * P)],
        out_specs=(blk(S5_CW, S5_CW), blk(S5_CW, S5_SW), blk(S5_SW, S5_CW), blk(8, S5_SW)),
        compiler_params=_params(("parallel", "parallel")),
        name="s5_prep",
    )(lamc, lamr, ct, bt, b12)


S5_GPS = 4
S5_LT = 1024


def _s5z_kernel(u_ref, s_ref, z_ref):
    for gg in range(S5_GPS):
        ug = u_ref[:, gg * S5_CW:(gg + 1) * S5_CW]
        for d in range(2):
            z_ref[d, :, gg * S5_SW:(gg + 1) * S5_SW] = _dot(ug, s_ref[d, gg])


def _s5_z(u_chunks, s_mat):
    return pl.pallas_call(
        _s5z_kernel,
        out_shape=jax.ShapeDtypeStruct((2, S5_ROWS, S5_GROUPS * S5_SW), F32),
        grid=(S5_GROUPS // S5_GPS,),
        in_specs=[pl.BlockSpec((S5_ROWS, S5_GPS * S5_CW), lambda g: (0, g)),
                  pl.BlockSpec((2, S5_GPS, S5_CW, S5_SW), lambda g: (0, g, 0, 0))],
        out_specs=pl.BlockSpec((2, S5_ROWS, S5_GPS * S5_SW), lambda g: (0, 0, g)),
        compiler_params=_params(("parallel",)),
        name="s5_chunk_state",
    )(u_chunks, s_mat)


def _s5scan_kernel(z_ref, co_ref, h0_ref, hin_ref, e_ref, g_ref):
    ns = S5_LT // S5_SW
    nq = S5_NSEQ
    lanes = [slice(j * S5_SW, (j + 1) * S5_SW) for j in range(ns)]

    def cmul(a1, a2, h):
        return a1 * h + a2 * pltpu.roll(h, S5_STATE, 1)

    for d in range(2):
        a16 = [(co_ref[d, 0:1, l], co_ref[d, 1:2, l]) for l in lanes]
        a256 = [(co_ref[d, 2:3, l], co_ref[d, 3:4, l]) for l in lanes]
        order = list(range(S5_NCHUNK)) if d == 0 else list(range(S5_NCHUNK - 1, -1, -1))

        def run(h, write):
            for n in order:
                rows = slice(n * nq, (n + 1) * nq)
                if write:
                    for j, l in enumerate(lanes):
                        hin_ref[d, rows, l] = h[j]
                h = [cmul(a16[j][0], a16[j][1], h[j]) + z_ref[d, rows, l] for j, l in enumerate(lanes)]
            return h

        end = run([jnp.zeros((nq, S5_SW), F32)] * ns, False)
        for j, l in enumerate(lanes):
            e_ref[d, :, l] = end[j]

        g_ref[0:BATCH, :] = jnp.zeros((BATCH, S5_LT), F32)
        segs = list(range(S5_SEGS)) if d == 0 else list(range(S5_SEGS - 1, -1, -1))
        for j, l in enumerate(lanes):
            for b in range(DEC_BATCH):
                g = jnp.broadcast_to(h0_ref[d, b:b + 1, l], (8, S5_SW))
                for k, s in enumerate(segs):
                    r = BATCH + b * S5_SEGS + s
                    g_ref[r:r + 1, l] = g[0:1, :]
                    if k + 1 < S5_SEGS:
                        g = cmul(a256[j][0], a256[j][1], g) + jnp.broadcast_to(e_ref[d, r:r + 1, l], (8, S5_SW))

        run([g_ref[:, l] for l in lanes], True)


def _s5_scan(z, coef, h0):
    nl = S5_GROUPS * S5_SW
    return pl.pallas_call(
        _s5scan_kernel,
        out_shape=(jax.ShapeDtypeStruct((2, S5_ROWS, nl), F32),
                   jax.ShapeDtypeStruct((2, S5_NSEQ, nl), F32)),
        grid=(nl // S5_LT,),
        in_specs=[pl.BlockSpec((2, S5_ROWS, S5_LT), lambda t: (0, 0, t)),
                  pl.BlockSpec((2, 8, S5_LT), lambda t: (0, 0, t)),
                  pl.BlockSpec((2, DEC_BATCH, S5_LT), lambda t: (0, 0, t))],
        out_specs=(pl.BlockSpec((2, S5_ROWS, S5_LT), lambda t: (0, 0, t)),
                   pl.BlockSpec((2, S5_NSEQ, S5_LT), lambda t: (0, 0, t))),
        scratch_shapes=[pltpu.VMEM((S5_NSEQ, S5_LT), F32)],
        compiler_params=_params(("parallel",)),
        name="s5_scan",
    )(z, coef, h0)


S5_YG = 8


def _s5y_kernel(u_ref, t_ref, hin_ref, in_ref, o_ref, y_ref, sel_ref):
    kw = S5_YG * S5_CW
    lanes = S5_YG * S5_GROUP
    pad = (S5_CHUNK - 1) * S5_GROUP

    @pl.when(pl.program_id(0) == 0)
    def _():
        lane2 = lax.broadcasted_iota(jnp.int32, sel_ref.shape, 1)
        lane = lane2 % lanes
        k = lax.broadcasted_iota(jnp.int32, sel_ref.shape, 0) - pad - (lane2 // lanes) * S5_GROUP
        hit = (k >= 0) & ((k % S5_CW) // S5_GROUP == 0) & (lane == (k // S5_CW) * S5_GROUP + k % S5_GROUP)
        sel_ref[...] = jnp.where(hit, 1.0, 0.0).astype(BF16)

    for gg in range(S5_YG):
        ug = u_ref[:, gg * S5_CW:(gg + 1) * S5_CW]
        acc = None
        for d in range(2):
            h = hin_ref[d, :, gg * S5_SW:(gg + 1) * S5_SW].astype(BF16)
            t = _dot(ug, t_ref[d, gg]) + _dot(h, in_ref[d, gg])
            acc = t if acc is None else acc + t
        y_ref[:, gg * S5_CW:(gg + 1) * S5_CW] = acc

    y = y_ref[...]
    hi, lo = _split_bf16(y)
    for j in range(0, S5_CHUNK, 2):
        off = pad - j * S5_GROUP
        pick = sel_ref[off:off + kw, :]
        yj = _dot(hi, pick) + _dot(lo, pick)
        for n in range(S5_NCHUNK):
            for e in range(2):
                o_ref[pl.ds(n * S5_CHUNK + j + e, S5_NSEQ, stride=S5_SEQ), :] = (
                    yj[n * S5_NSEQ:(n + 1) * S5_NSEQ, e * lanes:(e + 1) * lanes])


def _s5_y(u_chunks, toep, hin, in_mat):
    kw = S5_YG * S5_CW
    lanes = S5_YG * S5_GROUP
    return pl.pallas_call(
        _s5y_kernel,
        out_shape=jax.ShapeDtypeStruct((N_TOK, S5_WIDTH), F32),
        grid=(S5_GROUPS // S5_YG,),
        in_specs=[pl.BlockSpec((S5_ROWS, kw), lambda g: (0, g)),
                  pl.BlockSpec((2, S5_YG, S5_CW, S5_CW), lambda g: (0, g, 0, 0)),
                  pl.BlockSpec((2, S5_ROWS, S5_YG * S5_SW), lambda g: (0, 0, g)),
                  pl.BlockSpec((2, S5_YG, S5_SW, S5_CW), lambda g: (0, g, 0, 0))],
        out_specs=pl.BlockSpec((N_TOK, lanes), lambda g: (0, g)),
        scratch_shapes=[pltpu.VMEM((S5_ROWS, kw), F32), pltpu.VMEM((kw + S5_CW, 2 * lanes), BF16)],
        compiler_params=_params(("arbitrary",)),
        name="s5_chunk_out",
    )(u_chunks, toep, hin, in_mat)


def _s5out_kernel(y_ref, u_ref, d_ref, w_ref, b_ref, o_ref):
    y = y_ref[...] + u_ref[...] * d_ref[...]
    y = y * (0.5 * (1.0 + jnp.tanh(math.sqrt(2.0 / math.pi) * (y + 0.044715 * (y * y * y)))))
    z = _dot(y.astype(BF16), w_ref[...]) + b_ref[...]
    o_ref[...] = (y * _sigmoid(z)).astype(BF16)


def _s5_out(y_scan, proj, s5_d, glu_w_bf, glu_b):
    tm = 512
    return pl.pallas_call(
        _s5out_kernel,
        out_shape=jax.ShapeDtypeStruct((N_TOK, S5_WIDTH), BF16),
        grid=(N_TOK // tm,),
        in_specs=[pl.BlockSpec((tm, S5_WIDTH), lambda i: (i, 0)),
                  pl.BlockSpec((tm, S5_WIDTH), lambda i: (i, 0)),
                  pl.BlockSpec((1, S5_WIDTH), lambda i: (0, 0)),
                  pl.BlockSpec((S5_WIDTH, S5_WIDTH), lambda i: (0, 0)),
                  pl.BlockSpec((1, S5_WIDTH), lambda i: (0, 0))],
        out_specs=pl.BlockSpec((tm, S5_WIDTH), lambda i: (i, 0)),
        compiler_params=_params(("parallel",)),
        name="s5_gelu_glu",
    )(y_scan, proj, s5_d.reshape(1, S5_WIDTH), glu_w_bf, glu_b.reshape(1, S5_WIDTH))


S5_LB = 128
S5_LBG = S5_LB // S5_GROUP


def _s5chunks_kernel(p_ref, o_ref, x_ref, sel_ref):
    kw = S5_CHUNK * S5_LB
    pad = (S5_LBG - 1) * S5_GROUP

    @pl.when(pl.program_id(0) == 0)
    def _():
        k = lax.broadcasted_iota(jnp.int32, sel_ref.shape, 0) - pad
        lane = lax.broadcasted_iota(jnp.int32, sel_ref.shape, 1)
        hit = (k >= 0) & ((k % S5_LB) // S5_GROUP == 0) & (lane == (k // S5_LB) * S5_GROUP + k % S5_GROUP)
        sel_ref[...] = jnp.where(hit, 1.0, 0.0).astype(BF16)

    for n in range(S5_NCHUNK):
        for i in range(S5_CHUNK):
            rows = p_ref[pl.ds(n * S5_CHUNK + i, S5_NSEQ, stride=S5_SEQ), :]
            x_ref[n * S5_NSEQ:(n + 1) * S5_NSEQ, i * S5_LB:(i + 1) * S5_LB] = rows.astype(BF16)
    x = x_ref[...]
    for gl in range(S5_LBG):
        off = pad - gl * S5_GROUP
        o_ref[:, gl * S5_CW:(gl + 1) * S5_CW] = _dot(x, sel_ref[off:off + kw, :]).astype(BF16)


def _s5_chunks(proj):
    kw = S5_CHUNK * S5_LB
    return pl.pallas_call(
        _s5chunks_kernel,
        out_shape=jax.ShapeDtypeStruct((S5_ROWS, S5_GROUPS * S5_CW), BF16),
        grid=(S5_WIDTH // S5_LB,),
        in_specs=[pl.BlockSpec((N_TOK, S5_LB), lambda b: (0, b))],
        out_specs=pl.BlockSpec((S5_ROWS, kw), lambda b: (0, b)),
        scratch_shapes=[pltpu.VMEM((S5_ROWS, kw), BF16), pltpu.VMEM((kw + S5_LB, S5_CW), BF16)],
        compiler_params=_params(("arbitrary",)),
        name="s5_chunk_layout",
    )(proj)


def _s5_mixer(proj, state_re, state_im, prep, s5_d, glu_w_bf, glu_b):
    toep, s_mat, in_mat, coef = prep
    u_chunks = _s5_chunks(proj)
    z = _s5_z(u_chunks, s_mat)
    coef2 = coef.transpose(0, 2, 1, 3).reshape(2, 8, S5_GROUPS * S5_SW)
    h0 = jnp.concatenate([state_re, state_im], axis=-1)
    h0 = h0.transpose(1, 0, 2, 3).reshape(2, DEC_BATCH, S5_GROUPS * S5_SW)
    hin, ends = _s5_scan(z, coef2, h0)
    y_tok = _s5_y(u_chunks, toep, hin, in_mat)
    out = _s5_out(y_tok, proj, s5_d, glu_w_bf, glu_b)
    fin = ends[:, :BATCH].reshape(2, BATCH, S5_GROUPS, S5_SW).transpose(1, 0, 2, 3)
    return out, fin[..., :S5_STATE], fin[..., S5_STATE:]


def _softmax_rows(parts):
    m = parts[0].max(axis=-1, keepdims=True)
    for s in parts[1:]:
        m = jnp.maximum(m, s.max(axis=-1, keepdims=True))
    ps = [jnp.exp(s - m) for s in parts]
    tot = ps[0].sum(axis=-1, keepdims=True)
    for p in ps[1:]:
        tot = tot + p.sum(axis=-1, keepdims=True)
    inv = 1.0 / tot
    return [(p * inv).astype(BF16) for p in ps]


def _ctxattn_kernel(q_ref, k_ref, v_ref, o_ref, nk_ref, nv_ref):
    for h in range(NA_HEADS):
        sl = slice(h * NA_HEAD_DIM, (h + 1) * NA_HEAD_DIM)
        k = k_ref[:, sl]
        v = v_ref[:, sl]
        nk_ref[h] = k
        nv_ref[h] = v
        q = (q_ref[:, sl] * QK_SCALE).astype(BF16)
        (p,) = _softmax_rows([_dot_nt(q, k.astype(BF16))])
        o_ref[:, sl] = _dot(p, v.astype(BF16)).astype(BF16)


def _ctx_attention(proj):
    col = lambda c: pl.BlockSpec((SEQ, NA_WIDTH), lambda b: (b, c))
    cache = jax.ShapeDtypeStruct((BATCH, 1, NA_HEADS, SEQ, NA_HEAD_DIM), F32)
    cache_spec = pl.BlockSpec((None, None, NA_HEADS, SEQ, NA_HEAD_DIM), lambda b: (b, 0, 0, 0, 0))
    return pl.pallas_call(
        _ctxattn_kernel,
        out_shape=(jax.ShapeDtypeStruct((N_PROMPT, NA_WIDTH), BF16), cache, cache),
        grid=(BATCH,),
        in_specs=[col(1), col(2), col(3)],
        out_specs=(pl.BlockSpec((SEQ, NA_WIDTH), lambda b: (b, 0)), cache_spec, cache_spec),
        compiler_params=_params(("parallel",)),
        name="ctx_attention",
    )(proj, proj, proj)


def _na_geometry(r0):
    ks = min(max(r0 - NA_WIN_R // 2, 0), GRID_ROWS - NA_KROWS)
    tiles = []
    for a in range(NA_QROWS):
        rq = r0 + a
        rs = min(max(rq - NA_WIN_R // 2, 0), GRID_ROWS - NA_WIN_R)
        row = []
        for rl in range(NA_KROWS):
            rk = ks + rl
            row.append(rk - rq + NA_WIN_R - 1 if rs <= rk < rs + NA_WIN_R else None)
        tiles.append(row)
    return ks, tiles


NA_VARIANT_ROW0 = (0, NA_QROWS, GRID_ROWS - NA_QROWS)


def _nabias_kernel(rpb_ref, o_ref, t_ref):
    h = pl.program_id(0)
    shape = (GRID_W, 2 * GRID_W)
    cq = lax.broadcasted_iota(jnp.int32, shape, 0)
    lane = lax.broadcasted_iota(jnp.int32, shape, 1)
    ck = lane % GRID_W
    dc = jnp.clip(ck - cq + (NA_WIN_C - 1), 0, NA_NDC - 1)
    c0 = jnp.clip(cq - NA_WIN_C // 2, 0, GRID_W - NA_WIN_C)
    in_cols = (ck >= c0) & (ck < c0 + NA_WIN_C)
    neg = jnp.full(shape, NEG_INF, F32)
    for dr in range(NA_NDR):
        t = neg
        for v in range(NA_NDC):
            t = jnp.where(dc == v, rpb_ref[h, dr * NA_NDC + v], t)
        t_ref[dr] = jnp.where(in_cols, t, neg)
    left = lane < GRID_W
    for var, r0 in enumerate(NA_VARIANT_ROW0):
        _, tiles = _na_geometry(r0)
        for a in range(NA_QROWS):
            for m in range(NA_KROWS // 2):
                dl, dr_ = tiles[a][2 * m], tiles[a][2 * m + 1]
                tl = neg if dl is None else t_ref[dl]
                tr = neg if dr_ is None else t_ref[dr_]
                o_ref[var, a * GRID_W:(a + 1) * GRID_W, m * 2 * GRID_W:(m + 1) * 2 * GRID_W] = (
                    jnp.where(left, tl, tr))


def _na_bias(rpb):
    nq, nk = NA_QROWS * GRID_W, NA_KROWS * GRID_W
    return pl.pallas_call(
        _nabias_kernel,
        out_shape=jax.ShapeDtypeStruct((NA_HEADS, 3, nq, nk), F32),
        grid=(NA_HEADS,),
        in_specs=[pl.BlockSpec(memory_space=pltpu.SMEM)],
        out_specs=pl.BlockSpec((None, 3, nq, nk), lambda h: (h, 0, 0, 0)),
        scratch_shapes=[pltpu.VMEM((NA_NDR, GRID_W, 2 * GRID_W), F32)],
        compiler_params=_params(("parallel",)),
        name="na_bias",
    )(rpb.reshape(NA_HEADS, NA_NDR * NA_NDC))


def _naattn_kernel(q_ref, k_ref, v_ref, kc_ref, vc_ref, bias_ref, o_ref):
    kb = k_ref[...].astype(BF16)
    vb = v_ref[...].astype(BF16)
    kc = kc_ref[...].astype(BF16)
    vc = vc_ref[...].astype(BF16)
    nq = NA_QROWS * GRID_W
    for blk in range(GRID_ROWS // NA_QROWS):
        r0 = blk * NA_QROWS
        var = 0 if blk == 0 else (2 if r0 == NA_VARIANT_ROW0[2] else 1)
        ks, _ = _na_geometry(r0)
        keys = slice(ks * GRID_W, (ks + NA_KROWS) * GRID_W)
        q = (q_ref[r0 * GRID_W:r0 * GRID_W + nq, :] * QK_SCALE).astype(BF16)
        s_loc = _dot_nt(q, kb[keys]) + bias_ref[var]
        s_ctx = _dot_nt(q, kc)
        p_loc, p_ctx = _softmax_rows([s_loc, s_ctx])
        o = _dot(p_loc, vb[keys]) + _dot(p_ctx, vc)
        o_ref[r0 * GRID_W:r0 * GRID_W + nq, :] = o.astype(BF16)


def _na_attention(proj, cache_k, cache_v, bias):
    first = N_PROMPT // DEC_SEQ
    col = lambda c: pl.BlockSpec((DEC_SEQ, NA_HEAD_DIM), lambda b, h: (first + b, c * NA_HEADS + h))
    cache_spec = pl.BlockSpec((None, None, None, PAST_LEN, NA_HEAD_DIM), lambda b, h: (b, 0, h, 0, 0))
    return pl.pallas_call(
        _naattn_kernel,
        out_shape=jax.ShapeDtypeStruct((DEC_BATCH * DEC_SEQ, NA_WIDTH), BF16),
        grid=(DEC_BATCH, NA_HEADS),
        in_specs=[col(1), col(2), col(3), cache_spec, cache_spec,
                  pl.BlockSpec((None, 3, NA_QROWS * GRID_W, NA_KROWS * GRID_W), lambda b, h: (h, 0, 0, 0))],
        out_specs=pl.BlockSpec((DEC_SEQ, NA_HEAD_DIM), lambda b, h: (b, h)),
        compiler_params=_params(("parallel", "parallel")),
        name="na_attention",
    )(proj, proj, proj, cache_k, cache_v, bias)


def _outproj_kernel(a_ref, b_ref, w_ref, x_ref, gate_ref, g_ref, beta_ref, o_ref):
    half = a_ref.shape[1]
    y = _dot(a_ref[...], w_ref[0:half, :]) + _dot(b_ref[...], w_ref[half:2 * half, :])
    o_ref[...] = _post_norm(x_ref[...], y, gate_ref[...], g_ref[...], beta_ref[...])


def _out_proj(y_s5, y_att, w_bf, x, mod, ln_g, ln_b, layer):
    tm = 512
    half = y_s5.shape[1]
    return pl.pallas_call(
        _outproj_kernel,
        out_shape=jax.ShapeDtypeStruct((N_TOK, D_MODEL), F32),
        grid=(N_TOK // tm,),
        in_specs=[pl.BlockSpec((tm, half), lambda i: (i, 0)),
                  pl.BlockSpec((tm, half), lambda i: (i, 0)),
                  pl.BlockSpec((D_MODEL, D_MODEL), lambda i: (0, 0)),
                  pl.BlockSpec((tm, D_MODEL), lambda i: (i, 0)),
                  _mod_spec(layer, 2, tm), _row_spec(layer, 0), _row_spec(layer, 0)],
        out_specs=pl.BlockSpec((tm, D_MODEL), lambda i: (i, 0)),
        compiler_params=_params(("parallel",)),
        name="out_proj_norm",
    )(y_s5, y_att, w_bf, x, mod, ln_g, ln_b)


def _ffn_kernel(x_ref, sh_ref, sc_ref, gate_ref, g_ref, beta_ref, wg_ref, wu_ref, wd_ref, o_ref, h_ref, acc_ref):
    f = pl.program_id(1)

    @pl.when(f == 0)
    def _():
        h_ref[...] = (x_ref[...] * (1.0 + sc_ref[...]) + sh_ref[...]).astype(BF16)
        acc_ref[...] = jnp.zeros_like(acc_ref)

    h = h_ref[...]
    a = _dot(h, wg_ref[...])
    b = _dot(h, wu_ref[...])
    acc_ref[...] += _dot((a * _sigmoid(a) * b).astype(BF16), wd_ref[...])

    @pl.when(f == pl.num_programs(1) - 1)
    def _():
        o_ref[...] = _post_norm(x_ref[...], acc_ref[...], gate_ref[...], g_ref[...], beta_ref[...])


def _ffn(x, mod, ln_g, ln_b, wg_bf, wu_bf, wd_bf, layer):
    tm, tf = 512, 512
    return pl.pallas_call(
        _ffn_kernel,
        out_shape=jax.ShapeDtypeStruct((N_TOK, D_MODEL), F32),
        grid=(N_TOK // tm, FFN_DIM // tf),
        in_specs=[pl.BlockSpec((tm, D_MODEL), lambda i, f: (i, 0)),
                  _mod_spec(layer, 3, tm), _mod_spec(layer, 4, tm), _mod_spec(layer, 5, tm),
                  _row_spec(layer, 1), _row_spec(layer, 1),
                  pl.BlockSpec((D_MODEL, tf), lambda i, f: (0, f)),
                  pl.BlockSpec((D_MODEL, tf), lambda i, f: (0, f)),
                  pl.BlockSpec((tf, D_MODEL), lambda i, f: (f, 0))],
        out_specs=pl.BlockSpec((tm, D_MODEL), lambda i, f: (i, 0)),
        scratch_shapes=[pltpu.VMEM((tm, D_MODEL), BF16), pltpu.VMEM((tm, D_MODEL), F32)],
        compiler_params=_params(("parallel", "arbitrary")),
        name="ffn_norm",
    )(x, mod, mod, mod, ln_g, ln_b, wg_bf, wu_bf, wd_bf)


def _pool_kernel(x_ref, prev_ref, next_ref, sh_ref, sc_ref, gate_ref, g_ref, beta_ref, w_ref, ps_ref,
                 o_ref, ext_ref, y_ref):
    q = pl.program_id(0)
    latent = q >= BATCH
    seg = (q - BATCH) % S5_SEGS
    has_prev = latent & (seg > 0)
    has_next = latent & (seg < S5_SEGS - 1)
    seq_len = jnp.where(latent, DEC_SEQ, SEQ)
    t = jnp.where(latent, seg * SEQ, 0) + lax.broadcasted_iota(jnp.int32, (SEQ, 1), 0)

    scale = 1.0 + sc_ref[...]
    shift = sh_ref[...]
    x = x_ref[...]
    halo = POOL_HALO
    ext_ref[0:halo, :] = jnp.where(has_prev, prev_ref[...] * scale + shift, 0.0)
    ext_ref[halo:halo + SEQ, :] = x * scale + shift
    ext_ref[halo + SEQ:2 * halo + SEQ, :] = jnp.where(has_next, next_ref[...] * scale + shift, 0.0)

    for g, w in enumerate(POOL_SIZES):
        cols = slice(g * POOL_GROUP_DIM, (g + 1) * POOL_GROUP_DIM)
        total = None
        for k in range(-(w // 2), w - w // 2):
            part = ext_ref[halo + k:halo + k + SEQ, cols]
            total = part if total is None else total + part
        count = jnp.minimum(t + (w - w // 2), seq_len) - jnp.maximum(t - w // 2, 0)
        pooled = total / count.astype(F32) - ext_ref[halo:halo + SEQ, cols]
        y_ref[:, cols] = _dot(pooled.astype(BF16), w_ref[g])
    y = y_ref[...] * ps_ref[...]
    o_ref[...] = _post_norm(x, y, gate_ref[...], g_ref[...], beta_ref[...])


def _pool(x, mod, ln_g, ln_b, w_bf, pool_scale, layer):
    tm = SEQ
    nhb = N_TOK // POOL_HALO
    per = tm // POOL_HALO
    return pl.pallas_call(
        _pool_kernel,
        out_shape=jax.ShapeDtypeStruct((N_TOK, D_MODEL), F32),
        grid=(N_TOK // tm,),
        in_specs=[pl.BlockSpec((tm, D_MODEL), lambda i: (i, 0)),
                  pl.BlockSpec((POOL_HALO, D_MODEL), lambda i: (jnp.maximum(i * per - 1, 0), 0)),
                  pl.BlockSpec((POOL_HALO, D_MODEL), lambda i: (jnp.minimum((i + 1) * per, nhb - 1), 0)),
                  _mod_spec(layer, 0, tm), _mod_spec(layer, 1, tm), _mod_spec(layer, 2, tm),
                  _row_spec(layer, 0), _row_spec(layer, 0),
                  pl.BlockSpec((len(POOL_SIZES), POOL_GROUP_DIM, POOL_GROUP_DIM), lambda i: (0, 0, 0)),
                  pl.BlockSpec((1, D_MODEL), lambda i: (0, 0))],
        out_specs=pl.BlockSpec((tm, D_MODEL), lambda i: (i, 0)),
        scratch_shapes=[pltpu.VMEM((tm + 2 * POOL_HALO, D_MODEL), F32), pltpu.VMEM((tm, D_MODEL), F32)],
        compiler_params=_params(("parallel",)),
        name="pool_norm",
    )(x, x, x, mod, mod, mod, ln_g, ln_b, w_bf, pool_scale.reshape(1, D_MODEL))


ROUTER_LANES = 128


def _split_bf16(a):
    hi = a.astype(BF16)
    return hi, (a - hi.astype(F32)).astype(BF16)


def _router_kernel(x_ref, sh_ref, sc_ref, w_ref, b_ref, h_ref, info_ref):
    h = x_ref[...] * (1.0 + sc_ref[...]) + sh_ref[...]
    h_ref[...] = h
    hh, hl = _split_bf16(h)
    wh, wl = _split_bf16(w_ref[...])
    logits = _dot(hh, wh) + _dot(hl, wh) + _dot(hh, wl) + b_ref[...]
    lane = lax.broadcasted_iota(jnp.int32, logits.shape, 1)
    logits = jnp.where(lane < N_EXPERTS, logits, -jnp.inf)
    m1 = logits.max(axis=-1, keepdims=True)
    i1 = jnp.where(logits == m1, lane, ROUTER_LANES).min(axis=-1, keepdims=True)
    rest = jnp.where(lane == i1, -jnp.inf, logits)
    m2 = rest.max(axis=-1, keepdims=True)
    i2 = jnp.where(rest == m2, lane, ROUTER_LANES).min(axis=-1, keepdims=True)
    e = jnp.exp(m2 - m1)
    g1 = 1.0 / (1.0 + e)
    g2 = e / (1.0 + e)
    info = jnp.where(lane == 0, i1.astype(F32), jnp.where(lane == 1, i2.astype(F32),
                     jnp.where(lane == 2, g1, jnp.where(lane == 3, g2, 0.0))))
    info_ref[...] = info


def _router(x, mod, router_w, router_b, layer):
    tm = 512
    w = jnp.zeros((D_MODEL, ROUTER_LANES), F32).at[:, :N_EXPERTS].set(router_w)
    b = jnp.zeros((1, ROUTER_LANES), F32).at[0, :N_EXPERTS].set(router_b)
    return pl.pallas_call(
        _router_kernel,
        out_shape=(jax.ShapeDtypeStruct((N_TOK, D_MODEL), F32),
                   jax.ShapeDtypeStruct((N_TOK, ROUTER_LANES), F32)),
        grid=(N_TOK // tm,),
        in_specs=[pl.BlockSpec((tm, D_MODEL), lambda i: (i, 0)),
                  _mod_spec(layer, 3, tm), _mod_spec(layer, 4, tm),
                  pl.BlockSpec((D_MODEL, ROUTER_LANES), lambda i: (0, 0)),
                  pl.BlockSpec((1, ROUTER_LANES), lambda i: (0, 0))],
        out_specs=(pl.BlockSpec((tm, D_MODEL), lambda i: (i, 0)),
                   pl.BlockSpec((tm, ROUTER_LANES), lambda i: (i, 0))),
        compiler_params=_params(("parallel",)),
        name="moe_router",
    )(x, mod, mod, w, b)


def _routing_tables(info):
    experts = info[:, :2].astype(jnp.int32).reshape(-1)
    onehot = (experts[:, None] == jnp.arange(N_EXPERTS)[None, :]).astype(jnp.int32)
    counts = onehot.sum(axis=0)
    rank = ((jnp.cumsum(onehot, axis=0) - onehot) * onehot).sum(axis=1)
    rows = (counts + MOE_TILE - 1) // MOE_TILE * MOE_TILE
    span = (rows + MOE_SPAN - 1) // MOE_SPAN * MOE_SPAN
    ends = jnp.cumsum(span)
    starts = ends - span
    pos = (starts[experts] + rank).astype(jnp.int32)
    src = jnp.zeros((MOE_ROWS,), jnp.int32).at[pos].set(jnp.arange(2 * N_TOK, dtype=jnp.int32) // 2)
    tile_row = jnp.arange(MOE_NTILES, dtype=jnp.int32) * MOE_TILE
    owner = jnp.minimum(jnp.sum(tile_row[:, None] >= ends[None, :], axis=1), N_EXPERTS - 1)
    used = (tile_row < (starts + rows)[owner]) & (tile_row < ends[-1])
    fetch = lax.cummax(jnp.where(used, jnp.arange(MOE_NTILES, dtype=jnp.int32), 0))
    tile_tab = (used.astype(jnp.int32), fetch.astype(jnp.int32), owner[fetch].astype(jnp.int32))
    pos = pos.reshape(N_TOK, 2)
    return src, tile_tab, pos[:, 0], pos[:, 1]


def _row_copy(src_hbm, row, dst, r, sem):
    return pltpu.make_async_copy(src_hbm.at[pl.ds(row, 1), :], dst.at[pl.ds(r, 1), :], sem)


def _dispatch_kernel(src_ref, used_ref, h_hbm, o_ref, buf, sem):
    m = pl.program_id(0)
    slot = m % 2
    nxt = jnp.minimum(m + 1, MOE_NTILES - 1)

    def start_tile(tile, s):
        def issue(g, c):
            for u in range(DMA_UNROLL):
                r = g * DMA_UNROLL + u
                _row_copy(h_hbm, src_ref[tile * MOE_TILE + r], buf.at[s], r, sem.at[s]).start()
            return c

        lax.fori_loop(0, MOE_TILE // DMA_UNROLL, issue, 0)

    @pl.when(jnp.logical_and(m == 0, used_ref[0] == 1))
    def _():
        start_tile(0, 0)

    @pl.when(jnp.logical_and(m + 1 < MOE_NTILES, used_ref[nxt] == 1))
    def _():
        start_tile(m + 1, 1 - slot)

    used = used_ref[m] == 1

    @pl.when(used)
    def _():
        def drain(r, c):
            _row_copy(h_hbm, 0, buf.at[slot], r, sem.at[slot]).wait()
            return c

        lax.fori_loop(0, MOE_TILE, drain, 0, unroll=DMA_UNROLL)
        o_ref[...] = buf[slot].astype(BF16)

    @pl.when(jnp.logical_not(used))
    def _():
        o_ref[...] = jnp.zeros_like(o_ref)


def _dispatch(h, src, tile_tab):
    return pl.pallas_call(
        _dispatch_kernel,
        out_shape=jax.ShapeDtypeStruct((MOE_ROWS, D_MODEL), BF16),
        grid_spec=pltpu.PrefetchScalarGridSpec(
            num_scalar_prefetch=2,
            grid=(MOE_NTILES,),
            in_specs=[pl.BlockSpec(memory_space=pl.ANY)],
            out_specs=pl.BlockSpec((MOE_TILE, D_MODEL), lambda m, *_: (m, 0)),
            scratch_shapes=[pltpu.VMEM((2, MOE_TILE, D_MODEL), F32), pltpu.SemaphoreType.DMA((2,))],
        ),
        compiler_params=_params(("arbitrary",)),
        name="moe_dispatch",
    )(src, tile_tab[0], h)


def _gmm_up_kernel(used_ref, fetch_ref, exp_ref, x_ref, wg_ref, wu_ref, o_ref, wgb_ref, wub_ref):
    j = pl.program_id(1)
    t0 = j * MOE_PAIR
    used = used_ref[t0] == 1
    pair = used_ref[t0 + 1] == 1
    fresh = jnp.logical_or(j == 0, exp_ref[t0] != exp_ref[jnp.maximum(t0 - MOE_PAIR, 0)])

    @pl.when(jnp.logical_and(used, fresh))
    def _():
        wgb_ref[...] = wg_ref[...].astype(BF16)
        wub_ref[...] = wu_ref[...].astype(BF16)

    def swiglu_up(x):
        a = _dot(x, wgb_ref[...])
        b = _dot(x, wub_ref[...])
        return (a * _sigmoid(a) * b).astype(BF16)

    @pl.when(jnp.logical_and(used, pair))
    def _():
        o_ref[...] = swiglu_up(x_ref[...])

    @pl.when(jnp.logical_and(used, jnp.logical_not(pair)))
    def _():
        o_ref[0:MOE_TILE, :] = swiglu_up(x_ref[0:MOE_TILE, :])
        o_ref[MOE_TILE:MOE_SPAN, :] = jnp.zeros((MOE_SPAN - MOE_TILE, o_ref.shape[1]), BF16)

    @pl.when(jnp.logical_not(used))
    def _():
        o_ref[...] = jnp.zeros_like(o_ref)


def _gmm_up(xs, w_gate, w_up, tile_tab):
    tf = 1024
    blk = lambda j, fetch: fetch[j * MOE_PAIR] // MOE_PAIR
    wsp = pl.BlockSpec((None, D_MODEL, tf), lambda f, j, used, fetch, exp: (exp[j * MOE_PAIR], 0, f))
    return pl.pallas_call(
        _gmm_up_kernel,
        out_shape=jax.ShapeDtypeStruct((MOE_ROWS, EXPERT_DIM), BF16),
        grid_spec=pltpu.PrefetchScalarGridSpec(
            num_scalar_prefetch=3,
            grid=(EXPERT_DIM // tf, MOE_NTILES // MOE_PAIR),
            in_specs=[pl.BlockSpec((MOE_SPAN, D_MODEL), lambda f, j, used, fetch, exp: (blk(j, fetch), 0)),
                      wsp, wsp],
            out_specs=pl.BlockSpec((MOE_SPAN, tf), lambda f, j, used, fetch, exp: (j, f)),
            scratch_shapes=[pltpu.VMEM((D_MODEL, tf), BF16), pltpu.VMEM((D_MODEL, tf), BF16)],
        ),
        compiler_params=_params(("parallel", "arbitrary")),
        name="moe_gate_up",
    )(*tile_tab, xs, w_gate, w_up)


def _gmm_down_kernel(used_ref, fetch_ref, exp_ref, x_ref, w_ref, o_ref, wbf_ref):
    j = pl.program_id(1)
    t0 = j * MOE_PAIR
    used = used_ref[t0] == 1
    pair = used_ref[t0 + 1] == 1
    fresh = jnp.logical_or(j == 0, exp_ref[t0] != exp_ref[jnp.maximum(t0 - MOE_PAIR, 0)])

    @pl.when(jnp.logical_and(used, fresh))
    def _():
        wbf_ref[...] = w_ref[...].astype(BF16)

    @pl.when(jnp.logical_and(used, pair))
    def _():
        o_ref[...] = _dot(x_ref[...], wbf_ref[...])

    @pl.when(jnp.logical_and(used, jnp.logical_not(pair)))
    def _():
        o_ref[0:MOE_TILE, :] = _dot(x_ref[0:MOE_TILE, :], wbf_ref[...])
        o_ref[MOE_TILE:MOE_SPAN, :] = jnp.zeros((MOE_SPAN - MOE_TILE, o_ref.shape[1]), F32)

    @pl.when(jnp.logical_not(used))
    def _():
        o_ref[...] = jnp.zeros_like(o_ref)


def _gmm_down(g, w_down, tile_tab):
    tn = 512
    blk = lambda j, fetch: fetch[j * MOE_PAIR] // MOE_PAIR
    return pl.pallas_call(
        _gmm_down_kernel,
        out_shape=jax.ShapeDtypeStruct((MOE_ROWS, D_MODEL), F32),
        grid_spec=pltpu.PrefetchScalarGridSpec(
            num_scalar_prefetch=3,
            grid=(D_MODEL // tn, MOE_NTILES // MOE_PAIR),
            in_specs=[pl.BlockSpec((MOE_SPAN, EXPERT_DIM), lambda n, j, used, fetch, exp: (blk(j, fetch), 0)),
                      pl.BlockSpec((None, EXPERT_DIM, tn), lambda n, j, used, fetch, exp: (exp[j * MOE_PAIR], 0, n))],
            out_specs=pl.BlockSpec((MOE_SPAN, tn), lambda n, j, used, fetch, exp: (j, n)),
            scratch_shapes=[pltpu.VMEM((EXPERT_DIM, tn), BF16)],
        ),
        compiler_params=_params(("parallel", "arbitrary"), vmem=VMEM_LIMIT_MAX),
        name="moe_down",
    )(*tile_tab, g, w_down)


def _combine_kernel(p1_ref, p2_ref, y_hbm, x_ref, info_ref, gate_ref, g_ref, beta_ref, o_ref, b1, b2, sem):
    i = pl.program_id(0)
    tm = x_ref.shape[0]
    slot = i % 2

    def start_tile(tile, s):
        def issue(g, c):
            for u in range(DMA_UNROLL):
                r = g * DMA_UNROLL + u
                _row_copy(y_hbm, p1_ref[tile * tm + r], b1.at[s], r, sem.at[s]).start()
                _row_copy(y_hbm, p2_ref[tile * tm + r], b2.at[s], r, sem.at[s]).start()
            return c

        lax.fori_loop(0, tm // DMA_UNROLL, issue, 0)

    @pl.when(i == 0)
    def _():
        start_tile(0, 0)

    @pl.when(i + 1 < pl.num_programs(0))
    def _():
        start_tile(i + 1, 1 - slot)

    def drain(r, c):
        _row_copy(y_hbm, 0, b1.at[slot], r, sem.at[slot]).wait()
        _row_copy(y_hbm, 0, b2.at[slot], r, sem.at[slot]).wait()
        return c

    lax.fori_loop(0, tm, drain, 0, unroll=DMA_UNROLL)
    info = info_ref[...]
    y = info[:, 2:3] * b1[slot] + info[:, 3:4] * b2[slot]
    o_ref[...] = _post_norm(x_ref[...], y, gate_ref[...], g_ref[...], beta_ref[...])


def _combine(y_sorted, pos1, pos2, x, info, mod, ln_g, ln_b, layer):
    tm = 256
    return pl.pallas_call(
        _combine_kernel,
        out_shape=jax.ShapeDtypeStruct((N_TOK, D_MODEL), F32),
        grid_spec=pltpu.PrefetchScalarGridSpec(
            num_scalar_prefetch=2,
            grid=(N_TOK // tm,),
            in_specs=[pl.BlockSpec(memory_space=pl.ANY),
                      pl.BlockSpec((tm, D_MODEL), lambda i, *_: (i, 0)),
                      pl.BlockSpec((tm, ROUTER_LANES), lambda i, *_: (i, 0)),
                      _mod_spec(layer, 5, tm), _row_spec(layer, 1), _row_spec(layer, 1)],
            out_specs=pl.BlockSpec((tm, D_MODEL), lambda i, *_: (i, 0)),
            scratch_shapes=[pltpu.VMEM((2, tm, D_MODEL), F32), pltpu.VMEM((2, tm, D_MODEL), F32),
                            pltpu.SemaphoreType.DMA((2,))],
        ),
        compiler_params=_params(("arbitrary",)),
        name="moe_combine_norm",
    )(pos1, pos2, y_sorted, x, info, mod, ln_g, ln_b)


def _moe(x, mod, ln_g, ln_b, router_w, router_b, w_gate, w_up, w_down, layer):
    h, info = _router(x, mod, router_w, router_b, layer)
    src, tile_tab, pos1, pos2 = _routing_tables(info)
    xs = _dispatch(h, src, tile_tab)
    g = _gmm_up(xs, w_gate, w_up, tile_tab)
    y = _gmm_down(g, w_down, tile_tab)
    return _combine(y, pos1, pos2, x, info, mod, ln_g, ln_b, layer)


def kernel(x_prompt, x_sample, state_s5_re, state_s5_im, cache_k, cache_v, c, c_ctx, ada_w, ada_b, ln_g, ln_b, ab_w_in, ab_w_out, s5_lambda_re, s5_lambda_im, s5_log_dt, s5_b_re, s5_b_im, s5_c_re, s5_c_im, s5_d, s5_glu_w, s5_glu_b, na_rpb, ffn_w_gate, ffn_w_up, ffn_w_down, pool_w, pool_scale, moe_router_w, moe_router_b, moe_w_gate, moe_w_up, moe_w_down):
    x = jnp.concatenate([x_prompt.reshape(N_PROMPT, D_MODEL), x_sample.reshape(-1, D_MODEL)], axis=0)

    cond8 = jnp.zeros((8, D_MODEL), F32).at[0].set(c_ctx).at[1:N_COND].set(c)
    mod = _ada(cond8, ada_w, ada_b)
    mod = mod[:, :N_COND].reshape(DEPTH, N_COND, N_MOD, 1, D_MODEL).transpose(0, 2, 1, 3, 4)
    ln_g4 = ln_g.reshape(DEPTH, 2, 1, D_MODEL)
    ln_b4 = ln_b.reshape(DEPTH, 2, 1, D_MODEL)

    proj = _inproj(x, mod, _cast_bf16(ab_w_in[0]), 0)
    prep = _s5_prep(s5_lambda_re[0], s5_lambda_im[0], s5_log_dt[0], s5_b_re[0], s5_b_im[0],
                    s5_c_re[0], s5_c_im[0])
    y_s5, fin_re, fin_im = _s5_mixer(proj, state_s5_re[:, 0], state_s5_im[:, 0], prep, s5_d[0],
                                     _cast_bf16(s5_glu_w[0]), s5_glu_b[0])
    y_ctx, new_k, new_v = _ctx_attention(proj)
    y_na = _na_attention(proj, cache_k, cache_v, _na_bias(na_rpb[0]))
    y_att = jnp.concatenate([y_ctx, y_na], axis=0)
    x = _out_proj(y_s5, y_att, _cast_bf16(ab_w_out[0]), x, mod, ln_g4, ln_b4, 0)
    x = _ffn(x, mod, ln_g4, ln_b4, _cast_bf16(ffn_w_gate[0]), _cast_bf16(ffn_w_up[0]),
             _cast_bf16(ffn_w_down[0]), 0)

    x = _pool(x, mod, ln_g4, ln_b4, _cast_bf16(pool_w[0]), pool_scale[0], 1)
    x = _moe(x, mod, ln_g4, ln_b4, moe_router_w[0], moe_router_b[0], moe_w_gate[0], moe_w_up[0],
             moe_w_down[0], 1)

    y_prompt = x[:N_PROMPT].reshape(BATCH, SEQ, D_MODEL)
    y_sample = x[N_PROMPT:].reshape(DEC_BATCH, DEC_SEQ, D_MODEL)
    return (y_prompt, y_sample, fin_re[:, None], fin_im[:, None], new_k, new_v)
```
